```python
import math
import jax
import jax.numpy as jnp
from jax import lax
import numpy as np

D_MODEL = 2048
BATCH = 2
SEQ = 4096
DEPTH = 4
DEC_BATCH = 32
DEC_SEQ = 8
PAST_LEN = 16384
PAGE_SIZE = 128

N_MIXERS = 3
N_A = (DEPTH + 2) // 3
N_B = (DEPTH + 1) // 3
N_C = DEPTH // 3
RMS_EPS = 1e-6
D_FF = 5632
POOL_WINDOWS = (2, 4, 8, 16)
POOL_GROUPS = 4
POOL_GROUP_DIM = D_MODEL // POOL_GROUPS
POOL_BUF = max(POOL_WINDOWS) - 1
HEAD_DIM = 64
N_HEADS = D_MODEL // HEAD_DIM
N_KV_HEADS = 4
GQA_GROUP = N_HEADS // N_KV_HEADS
ATT_DIM = N_HEADS * HEAD_DIM
KV_DIM = N_KV_HEADS * HEAD_DIM
QKV_DIM = ATT_DIM + 2 * KV_DIM
WINDOW = 128
ATT_SCALE = HEAD_DIM ** -0.5
T5_BUCKETS = 32
T5_MAX_DISTANCE = 128
NEG_INF = -1e30
RWKV_N = 64
RWKV_HEADS = D_MODEL // RWKV_N
DECAY_LORA = 96
AAA_LORA = 96
GATE_LORA = 256
GN_EPS = 64e-5

kernel_name = "hybrid_pool_swa_rwkv7_macaron_step"


def rms_norm(x, g):
    xf = x.astype(jnp.float32)
    y = xf * lax.rsqrt(jnp.mean(xf * xf, axis=-1, keepdims=True) + RMS_EPS)
    return (y * g).astype(x.dtype)


def swiglu(x, w_gate, w_up, w_down):
    return (jax.nn.silu(x @ w_gate) * (x @ w_up)) @ w_down


def pool_mix(u, prefix, pos0, w_pool, scale):
    B, T, D = u.shape
    xe = u if prefix is None else jnp.concatenate([prefix.astype(u.dtype), u], axis=1)
    P = xe.shape[1] - T
    cs = jnp.cumsum(xe.astype(jnp.float32), axis=1)
    cs = jnp.concatenate([jnp.zeros((B, 1, D), jnp.float32), cs], axis=1)
    hi = P + 1 + jnp.arange(T)
    pos = pos0 + jnp.arange(T)
    means = []
    for gi, w in enumerate(POOL_WINDOWS):
        cs_g = cs[..., gi * POOL_GROUP_DIM:(gi + 1) * POOL_GROUP_DIM]
        lo = jnp.maximum(hi - w, 0)
        cnt = jnp.minimum(w, pos + 1).astype(jnp.float32)
        means.append((cs_g[:, hi] - cs_g[:, lo]) / cnt[None, :, None])
    mean = jnp.stack(means, axis=2)
    diff = mean - u.reshape(B, T, POOL_GROUPS, POOL_GROUP_DIM).astype(jnp.float32)
    out = jnp.einsum('btgc,gcd->btgd', diff.astype(u.dtype), w_pool).reshape(B, T, D)
    return out * scale, xe[:, -POOL_BUF:]


def t5_bucket(dist):
    dist = jnp.maximum(dist, 0)
    exact = T5_BUCKETS // 2
    ratio = jnp.log(jnp.maximum(dist, 1).astype(jnp.float32) / exact) / math.log(T5_MAX_DISTANCE / exact)
    large = jnp.minimum(exact + (ratio * (T5_BUCKETS - exact)).astype(jnp.int32), T5_BUCKETS - 1)
    return jnp.where(dist < exact, dist, large)


def rel_bias_for(dist, rel_bias):
    b = rel_bias.astype(jnp.float32)[t5_bucket(dist)]
    return jnp.transpose(b, (2, 0, 1)).reshape(N_KV_HEADS, GQA_GROUP, *dist.shape)


def swa_qkv(u, w_qkv, b_qkv):
    B, T, _ = u.shape
    qkv = u @ w_qkv + b_qkv
    q = qkv[..., :ATT_DIM].reshape(B, T, N_KV_HEADS, GQA_GROUP, HEAD_DIM)
    k = qkv[..., ATT_DIM:ATT_DIM + KV_DIM].reshape(B, T, N_KV_HEADS, HEAD_DIM)
    v = qkv[..., ATT_DIM + KV_DIM:].reshape(B, T, N_KV_HEADS, HEAD_DIM)
    return q, k, v


def sink_attend(q, k, v, bias, valid, sinks):
    s = jnp.einsum('...qhgd,...khd->...hgqk', q, k).astype(jnp.float32) * ATT_SCALE + bias
    s = jnp.where(valid, s, NEG_INF)
    sink = sinks.astype(jnp.float32).reshape(N_KV_HEADS, GQA_GROUP, 1, 1)
    m = jnp.maximum(jnp.max(s, axis=-1, keepdims=True), sink)
    p = jnp.exp(s - m)
    p = p / (jnp.sum(p, axis=-1, keepdims=True) + jnp.exp(sink - m))
    return jnp.einsum('...hgqk,...khd->...qhgd', p.astype(v.dtype), v)


def swa_prompt(u, w_qkv, b_qkv, w_o, b_o, sinks, rel_bias):
    B, T, _ = u.shape
    q, k, v = swa_qkv(u, w_qkv, b_qkv)
    nb = T // WINDOW
    qb = q.reshape(B, nb, WINDOW, N_KV_HEADS, GQA_GROUP, HEAD_DIM)

    def band(t):
        tb = t.reshape(B, nb, WINDOW, N_KV_HEADS, HEAD_DIM)
        prev = jnp.concatenate([jnp.zeros_like(tb[:, :1]), tb[:, :-1]], axis=1)
        return jnp.concatenate([prev, tb], axis=2)

    qi = jnp.arange(WINDOW)[:, None]
    kj = jnp.arange(2 * WINDOW)[None, :]
    dist = qi + WINDOW - kj
    blk = jnp.arange(nb)[:, None, None]
    valid = (dist >= 0) & (dist < WINDOW) & ((blk > 0) | (kj >= WINDOW))
    o = sink_attend(qb, band(k), band(v), rel_bias_for(dist, rel_bias), valid[:, None, None], sinks)
    o = o.reshape(B, T, ATT_DIM) @ w_o + b_o
    return o, k[:, -WINDOW:], v[:, -WINDOW:]


def swa_sample(u, k_buf, v_buf, w_qkv, b_qkv, w_o, b_o, sinks, rel_bias):
    B, T, _ = u.shape
    q, k, v = swa_qkv(u, w_qkv, b_qkv)
    kf = jnp.concatenate([k_buf.astype(k.dtype), k], axis=1)
    vf = jnp.concatenate([v_buf.astype(v.dtype), v], axis=1)
    nbuf = k_buf.shape[1]
    qi = jnp.arange(T)[:, None]
    kj = jnp.arange(nbuf + T)[None, :]
    dist = qi + nbuf - kj
    key_pos = PAST_LEN - nbuf + kj
    valid = (dist >= 0) & (dist < WINDOW) & (key_pos >= 0)
    o = sink_attend(q, kf, vf, rel_bias_for(dist, rel_bias), valid, sinks)
    o = o.reshape(B, T, ATT_DIM) @ w_o + b_o
    return o, kf[:, -WINDOW:], vf[:, -WINDOW:]


def rwkv7_mix(u, shift_prev, wkv0, mu, w_r, w_k, w_v, w_o, w0, w1, w2, a0, a1, a2, g1, g2,
              k_k, k_a, r_k, ln_w, ln_b):
    B, T, D = u.shape
    u_prev = jnp.concatenate([shift_prev[:, None].astype(u.dtype), u[:, :-1]], axis=1)
    dx = u_prev - u
    xr, xw, xk, xv, xa, xg = (u + dx * mu[i] for i in range(6))
    r = xr @ w_r
    k = xk @ w_k
    v = xv @ w_v
    logw = -jax.nn.softplus(-(w0 + jnp.tanh(xw @ w1) @ w2).astype(jnp.float32)) - 0.5
    decay = jnp.exp(-jnp.exp(logw))
    a = jax.nn.sigmoid((a0 + (xa @ a1) @ a2).astype(jnp.float32))
    g = jax.nn.sigmoid(xg @ g1) @ g2

    def heads(t):
        return t.astype(jnp.float32).reshape(*t.shape[:-1], RWKV_HEADS, RWKV_N)

    r, k, v, decay, a = heads(r), heads(k), heads(v), heads(decay), heads(a)
    kk = k * heads(k_k)
    kk = kk / jnp.maximum(jnp.sqrt(jnp.sum(kk * kk, axis=-1, keepdims=True)), 1e-12)
    kmod = k * (1.0 + (a - 1.0) * heads(k_a))

    def step(S, inp):
        r_t, w_t, k_t, v_t, kk_t, b_t = inp
        sa = jnp.einsum('bhvk,bhk->bhv', S, -kk_t)
        S = S * w_t[:, :, None, :] + sa[..., None] * b_t[:, :, None, :] + v_t[..., None] * k_t[:, :, None, :]
        return S, jnp.einsum('bhvk,bhk->bhv', S, r_t)

    tm = lambda t: jnp.swapaxes(t, 0, 1)
    S_final, y = lax.scan(step, wkv0.astype(jnp.float32),
                          (tm(r), tm(decay), tm(kmod), tm(v), tm(kk), tm(kk * a)))
    y = tm(y)
    mean = jnp.mean(y, axis=-1, keepdims=True)
    var = jnp.mean(jnp.square(y - mean), axis=-1, keepdims=True)
    y = ((y - mean) * lax.rsqrt(var + GN_EPS)).reshape(B, T, D) * ln_w + ln_b
    y = y + (jnp.sum(r * kmod * r_k, axis=-1, keepdims=True) * v).reshape(B, T, D)
    out = (y.astype(u.dtype) * g) @ w_o
    return out, u[:, -1], S_final


def setup_inputs(seed: int = 0) -> dict:
    keys = iter(jax.random.split(jax.random.key(seed), 48))

    def nrm(shape, scale):
        return jax.random.normal(next(keys), shape, jnp.float32) * scale

    def gain(shape):
        return 1.0 + nrm(shape, 0.05)

    D = D_MODEL
    return {
        "x_prompt": nrm((BATCH, SEQ, D), 1.0),
        "x_sample": nrm((DEC_BATCH, DEC_SEQ, D), 1.0),
        "state_pool": nrm((N_A, DEC_BATCH, POOL_BUF, D), 1.0),
        "cache_win_k": nrm((N_B, DEC_BATCH, WINDOW, N_KV_HEADS, HEAD_DIM), 1.0),
        "cache_win_v": nrm((N_B, DEC_BATCH, WINDOW, N_KV_HEADS, HEAD_DIM), 1.0),
        "state_shift": nrm((N_C, DEC_BATCH, D), 1.0),
        "state_wkv": nrm((N_C, DEC_BATCH, RWKV_HEADS, RWKV_N, RWKV_N), 0.3),
        "norm_ffn1": gain((DEPTH, D)),
        "norm_mix": gain((DEPTH, D)),
        "norm_ffn2": gain((DEPTH, D)),
        "norm_final": gain((D,)),
        "ffn_w_gate": nrm((DEPTH, 2, D, D_FF), D ** -0.5),
        "ffn_w_up": nrm((DEPTH, 2, D, D_FF), D ** -0.5),
        "ffn_w_down": nrm((DEPTH, 2, D_FF, D), D_FF ** -0.5),
        "pool_w": nrm((N_A, POOL_GROUPS, POOL_GROUP_DIM, POOL_GROUP_DIM), POOL_GROUP_DIM ** -0.5),
        "pool_scale": 0.5 + nrm((N_A, D), 0.05),
        "att_w_qkv": nrm((N_B, D, QKV_DIM), D ** -0.5),
        "att_b_qkv": nrm((N_B, QKV_DIM), 0.02),
        "att_w_o": nrm((N_B, ATT_DIM, D), ATT_DIM ** -0.5),
        "att_b_o": nrm((N_B, D), 0.02),
        "att_sinks": nrm((N_B, N_HEADS), 1.0),
        "rel_bias": nrm((T5_BUCKETS, N_HEADS), 0.5),
        "rwkv_mu": jax.random.uniform(next(keys), (N_C, 6, D), jnp.float32),
        "rwkv_w_r": nrm((N_C, D, D), D ** -0.5),
        "rwkv_w_k": nrm((N_C, D, D), D ** -0.5),
        "rwkv_w_v": nrm((N_C, D, D), D ** -0.5),
        "rwkv_w_o": nrm((N_C, D, D), D ** -0.5),
        "rwkv_w0": nrm((N_C, D), 0.5),
        "rwkv_w1": nrm((N_C, D, DECAY_LORA), D ** -0.5),
        "rwkv_w2": nrm((N_C, DECAY_LORA, D), 0.5 * DECAY_LORA ** -0.5),
        "rwkv_a0": nrm((N_C, D), 0.3),
        "rwkv_a1": nrm((N_C, D, AAA_LORA), D ** -0.5),
        "rwkv_a2": nrm((N_C, AAA_LORA, D), 0.5 * AAA_LORA ** -0.5),
        "rwkv_g1": nrm((N_C, D, GATE_LORA), D ** -0.5),
        "rwkv_g2": nrm((N_C, GATE_LORA, D), GATE_LORA ** -0.5),
        "rwkv_k_k": 0.85 + nrm((N_C, D), 0.05),
        "rwkv_k_a": 1.0 + nrm((N_C, D), 0.05),
        "rwkv_r_k": nrm((N_C, RWKV_HEADS, RWKV_N), 0.1),
        "rwkv_ln_w": gain((N_C, D)),
        "rwkv_ln_b": nrm((N_C, D), 0.02),
    }


def reference(x_prompt, x_sample, state_pool, cache_win_k, cache_win_v, state_shift, state_wkv,
              norm_ffn1, norm_mix, norm_ffn2, norm_final, ffn_w_gate, ffn_w_up, ffn_w_down,
              pool_w, pool_scale, att_w_qkv, att_b_qkv, att_w_o, att_b_o, att_sinks, rel_bias,
              rwkv_mu, rwkv_w_r, rwkv_w_k, rwkv_w_v, rwkv_w_o, rwkv_w0, rwkv_w1, rwkv_w2,
              rwkv_a0, rwkv_a1, rwkv_a2, rwkv_g1, rwkv_g2, rwkv_k_k, rwkv_k_a, rwkv_r_k,
              rwkv_ln_w, rwkv_ln_b):
    xp, xs = x_prompt, x_sample
    pool_p, pool_s, wk_p, wv_p, wk_s, wv_s, sh_p, sh_s, wkv_p, wkv_s = ([] for _ in range(10))
    for l in range(DEPTH):
        j = l // N_MIXERS
        kind = l % N_MIXERS
        xp = xp + 0.5 * swiglu(rms_norm(xp, norm_ffn1[l]), ffn_w_gate[l, 0], ffn_w_up[l, 0], ffn_w_down[l, 0])
        xs = xs + 0.5 * swiglu(rms_norm(xs, norm_ffn1[l]), ffn_w_gate[l, 0], ffn_w_up[l, 0], ffn_w_down[l, 0])
        up = rms_norm(xp, norm_mix[l])
        us = rms_norm(xs, norm_mix[l])
        if kind == 0:
            mp, bp = pool_mix(up, None, 0, pool_w[j], pool_scale[j])
            ms, bs = pool_mix(us, state_pool[j], PAST_LEN, pool_w[j], pool_scale[j])
            pool_p.append(bp)
            pool_s.append(bs)
        elif kind == 1:
            att = (att_w_qkv[j], att_b_qkv[j], att_w_o[j], att_b_o[j], att_sinks[j], rel_bias)
            mp, kp, vp = swa_prompt(up, *att)
            ms, ks_, vs_ = swa_sample(us, cache_win_k[j], cache_win_v[j], *att)
            wk_p.append(kp)
            wv_p.append(vp)
            wk_s.append(ks_)
            wv_s.append(vs_)
        else:
            rw = (rwkv_mu[j], rwkv_w_r[j], rwkv_w_k[j], rwkv_w_v[j], rwkv_w_o[j], rwkv_w0[j], rwkv_w1[j],
                  rwkv_w2[j], rwkv_a0[j], rwkv_a1[j], rwkv_a2[j], rwkv_g1[j], rwkv_g2[j], rwkv_k_k[j],
                  rwkv_k_a[j], rwkv_r_k[j], rwkv_ln_w[j], rwkv_ln_b[j])
            zero_shift = jnp.zeros((up.shape[0], D_MODEL), up.dtype)
            zero_wkv = jnp.zeros((up.shape[0], RWKV_HEADS, RWKV_N, RWKV_N), jnp.float32)
            mp, shp, sp = rwkv7_mix(up, zero_shift, zero_wkv, *rw)
            ms, shs, ss = rwkv7_mix(us, state_shift[j], state_wkv[j], *rw)
            sh_p.append(shp)
            sh_s.append(shs)
            wkv_p.append(sp)
            wkv_s.append(ss)
        xp = xp + mp
        xs = xs + ms
        xp = xp + 0.5 * swiglu(rms_norm(xp, norm_ffn2[l]), ffn_w_gate[l, 1], ffn_w_up[l, 1], ffn_w_down[l, 1])
        xs = xs + 0.5 * swiglu(rms_norm(xs, norm_ffn2[l]), ffn_w_gate[l, 1], ffn_w_up[l, 1], ffn_w_down[l, 1])
    y_prompt = rms_norm(xp, norm_final)
    y_sample = rms_norm(xs, norm_final)
    return (y_prompt, y_sample,
            jnp.stack(pool_p), jnp.stack(pool_s),
            jnp.stack(wk_p), jnp.stack(wv_p), jnp.stack(wk_s), jnp.stack(wv_s),
            jnp.stack(sh_p), jnp.stack(sh_s),
            jnp.stack(wkv_p), jnp.stack(wkv_s))
```

```python
import functools
import math

import numpy as np
import jax
import jax.numpy as jnp
from jax import lax
from jax.experimental import pallas as pl
from jax.experimental.pallas import tpu as pltpu

F32 = jnp.float32
BF16 = jnp.bfloat16

D_MODEL = 2048
BATCH = 2
SEQ = 4096
DEPTH = 4
DEC_BATCH = 32
DEC_SEQ = 8
PAST_LEN = 16384
N_MIXERS = 3
RMS_EPS = 1e-6
D_FF = 5632
POOL_WINDOWS = (2, 4, 8, 16)
POOL_GROUPS = 4
POOL_GROUP_DIM = D_MODEL // POOL_GROUPS
POOL_BUF = max(POOL_WINDOWS) - 1
POOL_HALO = POOL_BUF + 1
HEAD_DIM = 64
N_HEADS = D_MODEL // HEAD_DIM
N_KV_HEADS = 4
GQA_GROUP = N_HEADS // N_KV_HEADS
ATT_DIM = N_HEADS * HEAD_DIM
KV_DIM = N_KV_HEADS * HEAD_DIM
QKV_DIM = ATT_DIM + 2 * KV_DIM
WINDOW = 128
ATT_SCALE = HEAD_DIM ** -0.5
T5_BUCKETS = 32
T5_MAX_DISTANCE = 128
NEG_INF = -1e30
RWKV_N = 64
RWKV_HEADS = D_MODEL // RWKV_N
GN_EPS = 64e-5
LORA_PAD = 128

N_PROMPT = BATCH * SEQ
N_SAMPLE = DEC_BATCH * DEC_SEQ
N_TOK = N_PROMPT + N_SAMPLE

VMEM_LIMIT_BYTES = 56 * 1024 * 1024

TM_TOK = 704
TM_PROMPT = 512
TM_PROJ = 352
TF_FFN = 512
TN_PROJ = 512
SCAN_CHUNK = 64
SCAN_HEADS = 4
SCAN_LANES = SCAN_HEADS * RWKV_N


def _params(n_axes):
    return pltpu.CompilerParams(dimension_semantics=("arbitrary",) * n_axes,
                                vmem_limit_bytes=VMEM_LIMIT_BYTES)


def _dot(a, b):
    return jnp.dot(a, b, preferred_element_type=F32)


def _dot_nt(a, b):
    return lax.dot_general(a, b, (((1,), (1,)), ((), ())), preferred_element_type=F32)


def _dot_tn(a, b):
    return lax.dot_general(a, b, (((0,), (0,)), ((), ())), preferred_element_type=F32)


def _rms_norm(x, g):
    return x * lax.rsqrt(jnp.mean(x * x, axis=-1, keepdims=True) + RMS_EPS) * g


def _norm_kernel(x_ref, g_ref, o_ref):
    o_ref[...] = _rms_norm(x_ref[...], g_ref[...])


def _norm_rows(x, g, *, tm, row0, rows):
    d = x.shape[1]
    off = row0 // tm
    return pl.pallas_call(
        _norm_kernel,
        grid=(rows // tm,),
        in_specs=[pl.BlockSpec((tm, d), lambda i: (i + off, 0)),
                  pl.BlockSpec((1, d), lambda i: (0, 0))],
        out_specs=pl.BlockSpec((tm, d), lambda i: (i, 0)),
        out_shape=jax.ShapeDtypeStruct((rows, d), F32),
        compiler_params=_params(1),
        name="rms_norm",
    )(x, g.reshape(1, d))


def _ffn_kernel(x_ref, g_ref, wg_ref, wu_ref, wd_ref, o_ref, xn_ref, acc_ref):
    j = pl.program_id(1)

    @pl.when(j == 0)
    def _():
        xn_ref[...] = _rms_norm(x_ref[...], g_ref[...]).astype(BF16)
        acc_ref[...] = jnp.zeros_like(acc_ref)

    xn = xn_ref[...]
    gate = _dot(xn, wg_ref[...])
    up = _dot(xn, wu_ref[...])
    h = (gate * jax.nn.sigmoid(gate) * up).astype(BF16)
    acc_ref[...] += _dot(h, wd_ref[...])

    @pl.when(j == pl.num_programs(1) - 1)
    def _():
        o_ref[...] = x_ref[...] + 0.5 * acc_ref[...]


def _ffn(x, g, wg, wu, wd, layer, half):
    m, d = x.shape
    tm, tf = TM_TOK, TF_FFN
    w_in = pl.BlockSpec((None, None, d, tf), lambda i, j: (layer, half, 0, j))
    return pl.pallas_call(
        _ffn_kernel,
        grid=(m // tm, D_FF // tf),
        in_specs=[pl.BlockSpec((tm, d), lambda i, j: (i, 0)),
                  pl.BlockSpec((1, d), lambda i, j: (0, 0)),
                  w_in, w_in,
                  pl.BlockSpec((None, None, tf, d), lambda i, j: (layer, half, j, 0))],
        out_specs=pl.BlockSpec((tm, d), lambda i, j: (i, 0)),
        out_shape=jax.ShapeDtypeStruct((m, d), F32),
        scratch_shapes=[pltpu.VMEM((tm, d), BF16), pltpu.VMEM((tm, d), F32)],
        compiler_params=_params(2),
        name="ffn_half_step",
    )(x, g.reshape(1, d), wg, wu, wd)


def _matmul_kernel(*refs, has_bias, has_res):
    refs = list(refs)
    lhs_ref, w_ref = refs[0], refs[1]
    pos = 2
    b_ref = res_ref = None
    if has_bias:
        b_ref = refs[pos]
        pos += 1
    if has_res:
        res_ref = refs[pos]
        pos += 1
    o_ref, lhs_bf_ref = refs[pos], refs[pos + 1]

    @pl.when(pl.program_id(1) == 0)
    def _():
        lhs_bf_ref[...] = lhs_ref[...].astype(BF16)

    acc = _dot(lhs_bf_ref[...], w_ref[...])
    if has_bias:
        acc = acc + b_ref[...]
    if has_res:
        acc = res_ref[...] + acc
    o_ref[...] = acc


def _matmul(lhs, w, bias=None, res=None, *, tm, tn, res_row0=0):
    m, k = lhs.shape
    n = w.shape[1]
    off = res_row0 // tm
    in_specs = [pl.BlockSpec((tm, k), lambda i, j: (i, 0)),
                pl.BlockSpec((k, tn), lambda i, j: (0, j))]
    args = [lhs, w]
    if bias is not None:
        in_specs.append(pl.BlockSpec((1, tn), lambda i, j: (0, j)))
        args.append(bias.reshape(1, n))
    aliases = {}
    if res is not None:
        in_specs.append(pl.BlockSpec((tm, tn), lambda i, j: (i + off, j)))
        aliases = {len(args): 0}
        args.append(res)
        out_shape = jax.ShapeDtypeStruct(res.shape, F32)
    else:
        out_shape = jax.ShapeDtypeStruct((m, n), F32)
    return pl.pallas_call(
        functools.partial(_matmul_kernel, has_bias=bias is not None, has_res=res is not None),
        grid=(m // tm, n // tn),
        in_specs=in_specs,
        out_specs=pl.BlockSpec((tm, tn), lambda i, j: (i + off, j)),
        out_shape=out_shape,
        scratch_shapes=[pltpu.VMEM((tm, k), BF16)],
        input_output_aliases=aliases,
        compiler_params=_params(2),
        name="matmul_bias_residual",
    )(*args)


def _pool_group_out(diff, gi, x_ref, w_ref, sc_ref, o_ref):
    c = slice(gi * POOL_GROUP_DIM, (gi + 1) * POOL_GROUP_DIM)
    out = _dot(diff.astype(BF16), w_ref[gi])
    o_ref[:, c] = x_ref[:, c] + out * sc_ref[:, c]


def _pool_prompt_kernel(u_ref, halo_ref, x_ref, w_ref, sc_ref, o_ref, ext_ref, *, tm, tiles_per_seq):
    t_in_seq = pl.program_id(0) % tiles_per_seq
    ext_ref[0:POOL_HALO, :] = jnp.where(t_in_seq == 0, 0.0, halo_ref[...])
    ext_ref[POOL_HALO:, :] = u_ref[...]
    pos = t_in_seq * tm + lax.broadcasted_iota(jnp.int32, (tm, 1), 0)
    for gi, w in enumerate(POOL_WINDOWS):
        c = slice(gi * POOL_GROUP_DIM, (gi + 1) * POOL_GROUP_DIM)
        s = ext_ref[pl.ds(POOL_HALO, tm), c]
        for back in range(1, w):
            s = s + ext_ref[pl.ds(POOL_HALO - back, tm), c]
        cnt = jnp.minimum(w, pos + 1).astype(F32)
        _pool_group_out(s / cnt - u_ref[:, c], gi, x_ref, w_ref, sc_ref, o_ref)


def _pool_prompt(u, x, w_pool, scale):
    d = u.shape[1]
    tm = TM_PROMPT
    halo_blocks = tm // POOL_HALO
    return pl.pallas_call(
        functools.partial(_pool_prompt_kernel, tm=tm, tiles_per_seq=SEQ // tm),
        grid=(N_PROMPT // tm,),
        in_specs=[pl.BlockSpec((tm, d), lambda i: (i, 0)),
                  pl.BlockSpec((POOL_HALO, d), lambda i: (jnp.maximum(i * halo_blocks - 1, 0), 0)),
                  pl.BlockSpec((tm, d), lambda i: (i, 0)),
                  pl.BlockSpec((POOL_GROUPS, POOL_GROUP_DIM, POOL_GROUP_DIM), lambda i: (0, 0, 0)),
                  pl.BlockSpec((1, d), lambda i: (0, 0))],
        out_specs=pl.BlockSpec((tm, d), lambda i: (i, 0)),
        out_shape=jax.ShapeDtypeStruct(x.shape, F32),
        scratch_shapes=[pltpu.VMEM((tm + POOL_HALO, d), F32)],
        input_output_aliases={2: 0},
        compiler_params=_params(1),
        name="pool_prompt",
    )(u, u, x, w_pool, scale.reshape(1, d))


def _pool_sample_kernel(ext_ref, x_ref, w_ref, sc_ref, o_ref):
    for gi, w in enumerate(POOL_WINDOWS):
        c = slice(gi * POOL_GROUP_DIM, (gi + 1) * POOL_GROUP_DIM)
        cur = ext_ref[:, pl.ds(POOL_HALO, DEC_SEQ), c]
        s = cur
        for back in range(1, w):
            s = s + ext_ref[:, pl.ds(POOL_HALO - back, DEC_SEQ), c]
        cnt = min(w, PAST_LEN + 1)
        diff = (s / float(cnt) - cur).reshape(N_SAMPLE, POOL_GROUP_DIM)
        _pool_group_out(diff, gi, x_ref, w_ref, sc_ref, o_ref)


def _pool_sample(ext, x, w_pool, scale):
    d = x.shape[1]
    blk = N_PROMPT // N_SAMPLE
    return pl.pallas_call(
        _pool_sample_kernel,
        grid=(1,),
        in_specs=[pl.BlockSpec(ext.shape, lambda i: (0, 0, 0)),
                  pl.BlockSpec((N_SAMPLE, d), lambda i: (blk, 0)),
                  pl.BlockSpec((POOL_GROUPS, POOL_GROUP_DIM, POOL_GROUP_DIM), lambda i: (0, 0, 0)),
                  pl.BlockSpec((1, d), lambda i: (0, 0))],
        out_specs=pl.BlockSpec((N_SAMPLE, d), lambda i: (blk, 0)),
        out_shape=jax.ShapeDtypeStruct(x.shape, F32),
        input_output_aliases={1: 0},
        compiler_params=_params(1),
        name="pool_sample",
    )(ext, x, w_pool, scale.reshape(1, d))


def _t5_bucket_table(tq):
    qi = np.arange(tq)[:, None]
    kj = np.arange(WINDOW + tq)[None, :]
    dist = qi + WINDOW - kj
    exact = T5_BUCKETS // 2
    ratio = np.log(np.maximum(dist, 1) / exact) / math.log(T5_MAX_DISTANCE / exact)
    large = np.minimum(exact + (ratio * (T5_BUCKETS - exact)).astype(np.int64), T5_BUCKETS - 1)
    bucket = np.where(dist < exact, dist, large)
    valid = (dist >= 0) & (dist < WINDOW)
    table = np.where(valid, bucket, -1).astype(np.int32)
    return table[:, :WINDOW], table[:, WINDOW:]


def _attn_kernel(rb_ref, sink_ref, bkt_p_ref, bkt_c_ref, q_ref, kc_ref, vc_ref, kp_ref, vp_ref,
                 o_ref, bias_p_ref, bias_c_ref, sink_col_ref, *, tq, blocks_per_seq):
    step = pl.program_id(0)
    rows = GQA_GROUP * tq

    @pl.when(step == 0)
    def _():
        bkt_p = bkt_p_ref[...]
        bkt_c = bkt_c_ref[...]
        bias_p_ref[...] = jnp.zeros_like(bias_p_ref)
        bias_c_ref[...] = jnp.zeros_like(bias_c_ref)

        def add_bucket(b, carry):
            eq_p = bkt_p == b
            eq_c = bkt_c == b
            for h in range(N_HEADS):
                kvh, g = divmod(h, GQA_GROUP)
                r = slice(g * tq, (g + 1) * tq)
                val = rb_ref[b, h]
                bias_p_ref[kvh, r, :] += jnp.where(eq_p, val, 0.0)
                bias_c_ref[kvh, r, :] += jnp.where(eq_c, val, 0.0)
            return carry

        lax.fori_loop(0, T5_BUCKETS, add_bucket, 0)
        for h in range(N_HEADS):
            kvh, g = divmod(h, GQA_GROUP)
            r = slice(g * tq, (g + 1) * tq)
            bias_p_ref[kvh, r, :] = jnp.where(bkt_p < 0, NEG_INF, bias_p_ref[kvh, r, :])
            bias_c_ref[kvh, r, :] = jnp.where(bkt_c < 0, NEG_INF, bias_c_ref[kvh, r, :])
            sink_col_ref[kvh, r, :] = jnp.full((tq, 1), sink_ref[h], F32)

    no_prev = (step % blocks_per_seq == 0) if blocks_per_seq else None
    for kvh in range(N_KV_HEADS):
        heads = [kvh * GQA_GROUP + g for g in range(GQA_GROUP)]
        qs = jnp.concatenate([q_ref[:, h * HEAD_DIM:(h + 1) * HEAD_DIM] for h in heads], axis=0).astype(BF16)
        c = slice(kvh * HEAD_DIM, (kvh + 1) * HEAD_DIM)
        s_p = _dot_nt(qs, kp_ref[:, c].astype(BF16)) * ATT_SCALE + bias_p_ref[kvh]
        if no_prev is not None:
            s_p = jnp.where(no_prev, NEG_INF, s_p)
        s_c = _dot_nt(qs, kc_ref[:, c].astype(BF16)) * ATT_SCALE + bias_c_ref[kvh]
        sink = sink_col_ref[kvh]
        m = jnp.maximum(jnp.maximum(jnp.max(s_p, axis=-1, keepdims=True),
                                    jnp.max(s_c, axis=-1, keepdims=True)), sink)
        p_p = jnp.exp(s_p - m)
        p_c = jnp.exp(s_c - m)
        den = (jnp.sum(p_p, axis=-1, keepdims=True) + jnp.sum(p_c, axis=-1, keepdims=True)
               + jnp.exp(sink - m))
        o = _dot(p_p.astype(BF16), vp_ref[:, c].astype(BF16)) + _dot(p_c.astype(BF16), vc_ref[:, c].astype(BF16))
        o = o / den
        for g, h in enumerate(heads):
            o_ref[:, h * HEAD_DIM:(h + 1) * HEAD_DIM] = o[g * tq:(g + 1) * tq, :]
    del rows


def _attention(qkv, rel_bias, sinks, *, tq, n_blocks, row0, blocks_per_seq, k_prev=None, v_prev=None):
    bkt_p, bkt_c = _t5_bucket_table(tq)
    off = row0 // tq
    kcol = ATT_DIM // KV_DIM
    smem = pl.BlockSpec(memory_space=pltpu.SMEM)
    if k_prev is None:
        prev_rows = lambda i: jnp.maximum(i - 1, 0)
        kp_spec = pl.BlockSpec((WINDOW, KV_DIM), lambda i: (prev_rows(i), kcol))
        vp_spec = pl.BlockSpec((WINDOW, KV_DIM), lambda i: (prev_rows(i), kcol + 1))
        k_prev = v_prev = qkv
    else:
        kp_spec = vp_spec = pl.BlockSpec((None, WINDOW, KV_DIM), lambda i: (i, 0, 0))
    rows = GQA_GROUP * tq
    return pl.pallas_call(
        functools.partial(_attn_kernel, tq=tq, blocks_per_seq=blocks_per_seq),
        grid=(n_blocks,),
        in_specs=[smem, smem,
                  pl.BlockSpec((tq, WINDOW), lambda i: (0, 0)),
                  pl.BlockSpec((tq, tq), lambda i: (0, 0)),
                  pl.BlockSpec((tq, ATT_DIM), lambda i: (i + off, 0)),
                  pl.BlockSpec((tq, KV_DIM), lambda i: (i + off, kcol)),
                  pl.BlockSpec((tq, KV_DIM), lambda i: (i + off, kcol + 1)),
                  kp_spec, vp_spec],
        out_specs=pl.BlockSpec((tq, ATT_DIM), lambda i: (i, 0)),
        out_shape=jax.ShapeDtypeStruct((n_blocks * tq, ATT_DIM), F32),
        scratch_shapes=[pltpu.VMEM((N_KV_HEADS, rows, WINDOW), F32),
                        pltpu.VMEM((N_KV_HEADS, rows, tq), F32),
                        pltpu.VMEM((N_KV_HEADS, rows, 1), F32)],
        compiler_params=_params(1),
        name="swa_attention",
    )(rel_bias, sinks, jnp.asarray(bkt_p), jnp.asarray(bkt_c), qkv, qkv, qkv, k_prev, v_prev)


def _rwkv_proj_kernel(u_ref, up_ref, mu_ref, wr_ref, wk_ref, wv_ref, w1_ref, w2_ref, w0_ref,
                      a1_ref, a2_ref, a0_ref, g1_ref, g2_ref,
                      r_ref, k_ref, v_ref, ld_ref, a_ref, g_ref,
                      xr_ref, xk_ref, xv_ref, hw_ref, ha_ref, hg_ref):
    @pl.when(pl.program_id(1) == 0)
    def _():
        u = u_ref[...]
        dx = up_ref[...] - u
        mix = lambda i: (u + dx * mu_ref[i:i + 1, :]).astype(BF16)
        xr_ref[...] = mix(0)
        hw_ref[...] = jnp.tanh(_dot(mix(1), w1_ref[...])).astype(BF16)
        xk_ref[...] = mix(2)
        xv_ref[...] = mix(3)
        ha_ref[...] = _dot(mix(4), a1_ref[...]).astype(BF16)
        hg_ref[...] = jax.nn.sigmoid(_dot(mix(5), g1_ref[...])).astype(BF16)

    r_ref[...] = _dot(xr_ref[...], wr_ref[...])
    k_ref[...] = _dot(xk_ref[...], wk_ref[...])
    v_ref[...] = _dot(xv_ref[...], wv_ref[...])
    z = -(w0_ref[...] + _dot(hw_ref[...], w2_ref[...]))
    softplus = jnp.maximum(z, 0.0) + jnp.log1p(jnp.exp(-jnp.abs(z)))
    ld_ref[...] = -jnp.exp(-softplus - 0.5)
    a_ref[...] = jax.nn.sigmoid(a0_ref[...] + _dot(ha_ref[...], a2_ref[...]))
    g_ref[...] = _dot(hg_ref[...], g2_ref[...])


def _rwkv_proj(u, u_prev, mu, w_r, w_k, w_v, w1, w2, w0, a1, a2, a0, g1, g2):
    m, d = u.shape
    tm, tn = TM_PROJ, TN_PROJ
    gl = g1.shape[1]
    row = lambda i, j: (i, 0)
    col = lambda i, j: (0, j)
    fixed = lambda i, j: (0, 0)
    out = jax.ShapeDtypeStruct((m, d), F32)
    return pl.pallas_call(
        _rwkv_proj_kernel,
        grid=(m // tm, d // tn),
        in_specs=[pl.BlockSpec((tm, d), row), pl.BlockSpec((tm, d), row),
                  pl.BlockSpec((6, d), fixed),
                  pl.BlockSpec((d, tn), col), pl.BlockSpec((d, tn), col), pl.BlockSpec((d, tn), col),
                  pl.BlockSpec((d, LORA_PAD), fixed), pl.BlockSpec((LORA_PAD, tn), col), pl.BlockSpec((1, tn), col),
                  pl.BlockSpec((d, LORA_PAD), fixed), pl.BlockSpec((LORA_PAD, tn), col), pl.BlockSpec((1, tn), col),
                  pl.BlockSpec((d, gl), fixed), pl.BlockSpec((gl, tn), col)],
        out_specs=[pl.BlockSpec((tm, tn), lambda i, j: (i, j))] * 6,
        out_shape=[out] * 6,
        scratch_shapes=[pltpu.VMEM((tm, d), BF16)] * 3
        + [pltpu.VMEM((tm, LORA_PAD), BF16)] * 2 + [pltpu.VMEM((tm, gl), BF16)],
        compiler_params=_params(2),
        name="rwkv_projections",
    )(u, u_prev, mu, w_r, w_k, w_v, w1, w2, w0.reshape(1, d), a1, a2, a0.reshape(1, d), g1, g2)


def _block_diag_rows(x, rows_per_block, cols_per_block):
    tall = jnp.concatenate([x] * SCAN_HEADS, axis=0)
    rh = lax.broadcasted_iota(jnp.int32, tall.shape, 0) // rows_per_block
    ch = lax.broadcasted_iota(jnp.int32, tall.shape, 1) // cols_per_block
    return jnp.where(rh == ch, tall, 0.0).astype(BF16)


def _scan_kernel(*refs, chunk, has_init):
    (r_ref, k_ref, v_ref, ld_ref, a_ref, g_ref, kk_ref, ka_ref, rk_ref, lnw_ref, lnb_ref) = refs[:11]
    refs = refs[11:]
    if has_init:
        s0_ref, refs = refs[0], refs[1:]
    z_ref, sout_ref, s_ref = refs
    c_idx = pl.program_id(1)
    cc, n, w = chunk, RWKV_N, SCAN_LANES
    n_groups = RWKV_HEADS // SCAN_HEADS

    @pl.when(c_idx == 0)
    def _():
        s_ref[...] = jnp.zeros_like(s_ref)
        if has_init:
            for h in range(RWKV_HEADS):
                gi, hh = divmod(h, SCAN_HEADS)
                s_ref[gi, hh * n:(hh + 1) * n, hh * n:(hh + 1) * n] = s0_ref[h]

    def iota(shape, axis):
        return lax.broadcasted_iota(jnp.int32, shape, axis)

    ones_bd = (iota((w, w), 0) // n == iota((w, w), 1) // n).astype(BF16)
    tril = (iota((cc, cc), 0) >= iota((cc, cc), 1)).astype(BF16)
    t_row = iota((cc, SCAN_HEADS * cc), 0)
    s_col = iota((cc, SCAN_HEADS * cc), 1) % cc
    strict = t_row > s_col
    causal = t_row >= s_col
    head_diag = iota((w, w), 0) // n == iota((w, w), 1) // n

    def split2(x):
        hi = x.astype(BF16)
        return hi, (x - hi.astype(F32)).astype(BF16)

    def head_sum(x):
        hi, lo = split2(x)
        both = _dot(jnp.concatenate([hi, lo], axis=0), ones_bd)
        return both[:cc] + both[cc:]

    def bd_lanes(x):
        return _block_diag_rows(x, cc, n)

    def bd_tokens(x):
        return _block_diag_rows(x, cc, cc)

    for gi in range(n_groups):
        c = slice(gi * w, (gi + 1) * w)
        r, k, v, ld, a = r_ref[:, c], k_ref[:, c], v_ref[:, c], ld_ref[:, c], a_ref[:, c]
        kk = k * kk_ref[:, c]
        kap = kk / jnp.maximum(jnp.sqrt(head_sum(kk * kk)), 1e-12)
        kmod = k * (1.0 + (a - 1.0) * ka_ref[:, c])
        b = kap * a

        p1 = ld.astype(BF16)
        r1 = ld - p1.astype(F32)
        p2 = r1.astype(BF16)
        p3 = (r1 - p2.astype(F32)).astype(BF16)
        cum = _dot(tril, jnp.concatenate([p1, p2, p3], axis=1))
        lcum = cum[:, :w] + cum[:, w:2 * w] + cum[:, 2 * w:]
        lend = lcum[cc - 1:cc, :]
        e_inc = jnp.exp(lcum)
        e_exc = jnp.exp(lcum - ld)
        e_neg = jnp.exp(-lcum)
        e_end = jnp.exp(lend - lcum)

        kq = kap * e_exc
        rq = r * e_inc
        qr = jnp.concatenate([kq, rq], axis=0).astype(BF16)
        bk_bd = jnp.concatenate([bd_lanes(b * e_neg), bd_lanes(kmod * e_neg)], axis=0)
        gram = _dot_nt(qr, bk_bd)
        hc = SCAN_HEADS * cc
        a_w = jnp.where(strict, gram[:cc, :hc], 0.0)
        bk_w = jnp.where(strict, gram[:cc, hc:], 0.0)
        cb_w = jnp.where(causal, gram[cc:, :hc], 0.0)
        ck_w = jnp.where(causal, gram[cc:, hc:], 0.0)

        s_bd = s_ref[gi]
        qs = _dot_nt(qr, s_bd.astype(BF16))
        v_bd = bd_lanes(v)

        x = -(qs[:cc] + _dot(bk_w.astype(BF16), v_bd))
        x = x - _dot(a_w.astype(BF16), bd_lanes(x))
        a_pow = a_w
        power = 2
        while power < cc:
            a_pow = _dot(a_pow.astype(BF16), bd_tokens(a_pow))
            x = x + _dot(a_pow.astype(BF16), bd_lanes(x))
            power *= 2
        u = x

        y = qs[cc:] + _dot(jnp.concatenate([cb_w, ck_w], axis=1).astype(BF16),
                           jnp.concatenate([bd_lanes(u), v_bd], axis=0))
        upd = _dot_tn(jnp.concatenate([u, v], axis=0).astype(BF16),
                      jnp.concatenate([b * e_end, kmod * e_end], axis=0).astype(BF16))
        s_new = s_bd * jnp.exp(lend) + jnp.where(head_diag, upd, 0.0)
        s_ref[gi] = s_new

        mean = head_sum(y) * (1.0 / n)
        dev = y - mean
        var = head_sum(dev * dev) * (1.0 / n)
        yn = dev * lax.rsqrt(var + GN_EPS) * lnw_ref[:, c] + lnb_ref[:, c]
        bonus = head_sum(r * kmod * rk_ref[:, c]) * v
        z_ref[:, c] = (yn + bonus) * g_ref[:, c]

        @pl.when(c_idx == pl.num_programs(1) - 1)
        def _():
            for hh in range(SCAN_HEADS):
                sout_ref[gi * SCAN_HEADS + hh] = s_new[hh * n:(hh + 1) * n, hh * n:(hh + 1) * n]


def _rwkv_scan(proj, k_k, k_a, r_k, ln_w, ln_b, *, n_seq, chunk, n_chunks, row0, s0=None):
    d = D_MODEL
    off = row0 // chunk
    tok = pl.BlockSpec((chunk, d), lambda b, c: (off + b * n_chunks + c, 0))
    par = pl.BlockSpec((1, d), lambda b, c: (0, 0))
    state = pl.BlockSpec((None, RWKV_HEADS, RWKV_N, RWKV_N), lambda b, c: (b, 0, 0, 0))
    in_specs = [tok] * 6 + [par] * 5
    args = list(proj) + [p.reshape(1, d) for p in (k_k, k_a, r_k, ln_w, ln_b)]
    if s0 is not None:
        in_specs.append(state)
        args.append(s0)
    n_groups = RWKV_HEADS // SCAN_HEADS
    return pl.pallas_call(
        functools.partial(_scan_kernel, chunk=chunk, has_init=s0 is not None),
        grid=(n_seq, n_chunks),
        in_specs=in_specs,
        out_specs=[pl.BlockSpec((chunk, d), lambda b, c: (b * n_chunks + c, 0)), state],
        out_shape=[jax.ShapeDtypeStruct((n_seq * n_chunks * chunk, d), F32),
                   jax.ShapeDtypeStruct((n_seq, RWKV_HEADS, RWKV_N, RWKV_N), F32)],
        scratch_shapes=[pltpu.VMEM((n_groups, SCAN_LANES, SCAN_LANES), F32)],
        compiler_params=_params(2),
        name="rwkv_scan",
    )(*args)


def _pad_cols(w):
    return jnp.pad(w, ((0, 0), (0, LORA_PAD - w.shape[1])))


def _pad_rows(w):
    return jnp.pad(w, ((0, LORA_PAD - w.shape[0]), (0, 0)))


def kernel(x_prompt, x_sample, state_pool, cache_win_k, cache_win_v, state_shift, state_wkv, norm_ffn1, norm_mix, norm_ffn2, norm_final, ffn_w_gate, ffn_w_up, ffn_w_down, pool_w, pool_scale, att_w_qkv, att_b_qkv, att_w_o, att_b_o, att_sinks, rel_bias, rwkv_mu, rwkv_w_r, rwkv_w_k, rwkv_w_v, rwkv_w_o, rwkv_w0, rwkv_w1, rwkv_w2, rwkv_a0, rwkv_a1, rwkv_a2, rwkv_g1, rwkv_g2, rwkv_k_k, rwkv_k_a, rwkv_r_k, rwkv_ln_w, rwkv_ln_b):
    d = D_MODEL
    x = jnp.concatenate([x_prompt.reshape(N_PROMPT, d), x_sample.reshape(N_SAMPLE, d)], axis=0)
    wg, wu, wd = (w.astype(BF16) for w in (ffn_w_gate, ffn_w_up, ffn_w_down))
    pool_p, pool_s, wk_p, wv_p, wk_s, wv_s, sh_p, sh_s, wkv_p, wkv_s = ([] for _ in range(10))

    for l in range(DEPTH):
        j, kind = divmod(l, N_MIXERS)
        x = _ffn(x, norm_ffn1[l], wg, wu, wd, l, 0)
        u = _norm_rows(x, norm_mix[l], tm=TM_TOK, row0=0, rows=N_TOK)
        u_p = u[:N_PROMPT].reshape(BATCH, SEQ, d)
        u_s = u[N_PROMPT:].reshape(DEC_BATCH, DEC_SEQ, d)
        if kind == 0:
            w_pool = pool_w[j].astype(BF16)
            ext = jnp.concatenate([jnp.zeros((DEC_BATCH, 1, d), F32), state_pool[j], u_s], axis=1)
            x = _pool_prompt(u, x, w_pool, pool_scale[j])
            x = _pool_sample(ext, x, w_pool, pool_scale[j])
            pool_p.append(u_p[:, -POOL_BUF:])
            pool_s.append(ext[:, -POOL_BUF:])
        elif kind == 1:
            qkv = _matmul(u, att_w_qkv[j].astype(BF16), att_b_qkv[j], tm=TM_TOK, tn=QKV_DIM // 5)
            k_buf = cache_win_k[j].reshape(DEC_BATCH, WINDOW, KV_DIM)
            v_buf = cache_win_v[j].reshape(DEC_BATCH, WINDOW, KV_DIM)
            o_p = _attention(qkv, rel_bias, att_sinks[j], tq=WINDOW, n_blocks=N_PROMPT // WINDOW,
                             row0=0, blocks_per_seq=SEQ // WINDOW)
            o_s = _attention(qkv, rel_bias, att_sinks[j], tq=DEC_SEQ, n_blocks=DEC_BATCH,
                             row0=N_PROMPT, blocks_per_seq=0, k_prev=k_buf, v_prev=v_buf)
            w_o = att_w_o[j].astype(BF16)
            x = _matmul(o_p, w_o, att_b_o[j], x, tm=TM_PROMPT, tn=d, res_row0=0)
            x = _matmul(o_s, w_o, att_b_o[j], x, tm=N_SAMPLE, tn=d, res_row0=N_PROMPT)
            k_new = qkv[:, ATT_DIM:ATT_DIM + KV_DIM]
            v_new = qkv[:, ATT_DIM + KV_DIM:]
            kv_shape = (WINDOW, N_KV_HEADS, HEAD_DIM)
            wk_p.append(k_new[:N_PROMPT].reshape(BATCH, SEQ, KV_DIM)[:, -WINDOW:].reshape(BATCH, *kv_shape))
            wv_p.append(v_new[:N_PROMPT].reshape(BATCH, SEQ, KV_DIM)[:, -WINDOW:].reshape(BATCH, *kv_shape))
            k_s = jnp.concatenate([k_buf, k_new[N_PROMPT:].reshape(DEC_BATCH, DEC_SEQ, KV_DIM)], axis=1)
            v_s = jnp.concatenate([v_buf, v_new[N_PROMPT:].reshape(DEC_BATCH, DEC_SEQ, KV_DIM)], axis=1)
            wk_s.append(k_s[:, -WINDOW:].reshape(DEC_BATCH, *kv_shape))
            wv_s.append(v_s[:, -WINDOW:].reshape(DEC_BATCH, *kv_shape))
        else:
            prev_p = jnp.concatenate([jnp.zeros((BATCH, 1, d), F32), u_p[:, :-1]], axis=1)
            prev_s = jnp.concatenate([state_shift[j][:, None], u_s[:, :-1]], axis=1)
            u_prev = jnp.concatenate([prev_p.reshape(N_PROMPT, d), prev_s.reshape(N_SAMPLE, d)], axis=0)
            proj = _rwkv_proj(u, u_prev, rwkv_mu[j],
                              rwkv_w_r[j].astype(BF16), rwkv_w_k[j].astype(BF16), rwkv_w_v[j].astype(BF16),
                              _pad_cols(rwkv_w1[j]).astype(BF16), _pad_rows(rwkv_w2[j]).astype(BF16), rwkv_w0[j],
                              _pad_cols(rwkv_a1[j]).astype(BF16), _pad_rows(rwkv_a2[j]).astype(BF16), rwkv_a0[j],
                              rwkv_g1[j].astype(BF16), rwkv_g2[j].astype(BF16))
            head_params = (rwkv_k_k[j], rwkv_k_a[j], rwkv_r_k[j], rwkv_ln_w[j], rwkv_ln_b[j])
            z_p, s_p = _rwkv_scan(proj, *head_params, n_seq=BATCH, chunk=SCAN_CHUNK,
                                  n_chunks=SEQ // SCAN_CHUNK, row0=0)
            z_s, s_s = _rwkv_scan(proj, *head_params, n_seq=DEC_BATCH, chunk=DEC_SEQ,
                                  n_chunks=1, row0=N_PROMPT, s0=state_wkv[j])
            w_o = rwkv_w_o[j].astype(BF16)
            x = _matmul(z_p, w_o, None, x, tm=TM_PROMPT, tn=d, res_row0=0)
            x = _matmul(z_s, w_o, None, x, tm=N_SAMPLE, tn=d, res_row0=N_PROMPT)
            sh_p.append(u_p[:, -1])
            sh_s.append(u_s[:, -1])
            wkv_p.append(s_p)
            wkv_s.append(s_s)
        x = _ffn(x, norm_ffn2[l], wg, wu, wd, l, 1)

    y_p = _norm_rows(x, norm_final, tm=TM_PROMPT, row0=0, rows=N_PROMPT)
    y_s = _norm_rows(x, norm_final, tm=N_SAMPLE, row0=N_PROMPT, rows=N_SAMPLE)
    return (y_p.reshape(BATCH, SEQ, d), y_s.reshape(DEC_BATCH, DEC_SEQ, d),
            jnp.stack(pool_p), jnp.stack(pool_s),
            jnp.stack(wk_p), jnp.stack(wv_p), jnp.stack(wk_s), jnp.stack(wv_s),
            jnp.stack(sh_p), jnp.stack(sh_s),
            jnp.stack(wkv_p), jnp.stack(wkv_s))
```

```python
import functools
import math

import numpy as np
import jax
import jax.numpy as jnp
from jax import lax
from jax.experimental import pallas as pl
from jax.experimental.pallas import tpu as pltpu

F32 = jnp.float32
BF16 = jnp.bfloat16

D_MODEL = 2048
BATCH = 2
SEQ = 4096
DEPTH = 4
DEC_BATCH = 32
DEC_SEQ = 8
PAST_LEN = 16384
N_MIXERS = 3
RMS_EPS = 1e-6
D_FF = 5632
POOL_WINDOWS = (2, 4, 8, 16)
POOL_GROUPS = 4
POOL_GROUP_DIM = D_MODEL // POOL_GROUPS
POOL_BUF = max(POOL_WINDOWS) - 1
POOL_HALO = POOL_BUF + 1
HEAD_DIM = 64
N_HEADS = D_MODEL // HEAD_DIM
N_KV_HEADS = 4
GQA_GROUP = N_HEADS // N_KV_HEADS
ATT_DIM = N_HEADS * HEAD_DIM
KV_DIM = N_KV_HEADS * HEAD_DIM
QKV_DIM = ATT_DIM + 2 * KV_DIM
WINDOW = 128
ATT_SCALE = HEAD_DIM ** -0.5
T5_BUCKETS = 32
T5_MAX_DISTANCE = 128
NEG_INF = -1e30
RWKV_N = 64
RWKV_HEADS = D_MODEL // RWKV_N
GN_EPS = 64e-5
LORA_PAD = 128

N_PROMPT = BATCH * SEQ
N_SAMPLE = DEC_BATCH * DEC_SEQ
N_TOK = N_PROMPT + N_SAMPLE

VMEM_LIMIT_BYTES = 56 * 1024 * 1024

TM_TOK = 704
TM_PROMPT = 512
TM_PROJ = 352
TF_FFN = 512
TN_PROJ = 512
SCAN_CHUNK = 64
SCAN_HEADS = 4
SCAN_LANES = SCAN_HEADS * RWKV_N


def _params(n_axes):
    return pltpu.CompilerParams(dimension_semantics=("arbitrary",) * n_axes,
                                vmem_limit_bytes=VMEM_LIMIT_BYTES)


def _dot(a, b):
    return jnp.dot(a, b, preferred_element_type=F32)


def _dot_nt(a, b):
    return lax.dot_general(a, b, (((1,), (1,)), ((), ())), preferred_element_type=F32)


def _dot_tn(a, b):
    return lax.dot_general(a, b, (((0,), (0,)), ((), ())), preferred_element_type=F32)


def _round_robin(gens):
    live = list(gens)
    while live:
        still = []
        for gen in live:
            try:
                next(gen)
                still.append(gen)
            except StopIteration:
                pass
        live = still


def _rms_norm(x, g):
    return x * lax.rsqrt(jnp.mean(x * x, axis=-1, keepdims=True) + RMS_EPS) * g


def _norm_kernel(x_ref, g_ref, o_ref):
    o_ref[...] = _rms_norm(x_ref[...], g_ref[...])


def _norm_rows(x, g, *, tm, row0, rows):
    d = x.shape[1]
    off = row0 // tm
    return pl.pallas_call(
        _norm_kernel,
        grid=(rows // tm,),
        in_specs=[pl.BlockSpec((tm, d), lambda i: (i + off, 0)),
                  pl.BlockSpec((1, d), lambda i: (0, 0))],
        out_specs=pl.BlockSpec((tm, d), lambda i: (i, 0)),
        out_shape=jax.ShapeDtypeStruct((rows, d), F32),
        compiler_params=_params(1),
        name="rms_norm",
    )(x, g.reshape(1, d))


def _ffn_kernel(x_ref, g_ref, wg_ref, wu_ref, wd_ref, o_ref, xn_ref, acc_ref):
    j = pl.program_id(1)

    @pl.when(j == 0)
    def _():
        xn_ref[...] = _rms_norm(x_ref[...], g_ref[...]).astype(BF16)
        acc_ref[...] = jnp.zeros_like(acc_ref)

    xn = xn_ref[...]
    gate = _dot(xn, wg_ref[...])
    up = _dot(xn, wu_ref[...])
    h = (gate * jax.nn.sigmoid(gate) * up).astype(BF16)
    acc_ref[...] += _dot(h, wd_ref[...])

    @pl.when(j == pl.num_programs(1) - 1)
    def _():
        o_ref[...] = x_ref[...] + 0.5 * acc_ref[...]


def _ffn(x, g, wg, wu, wd, layer, half):
    m, d = x.shape
    tm, tf = TM_TOK, TF_FFN
    w_in = pl.BlockSpec((None, None, d, tf), lambda i, j: (layer, half, 0, j))
    return pl.pallas_call(
        _ffn_kernel,
        grid=(m // tm, D_FF // tf),
        in_specs=[pl.BlockSpec((tm, d), lambda i, j: (i, 0)),
                  pl.BlockSpec((1, d), lambda i, j: (0, 0)),
                  w_in, w_in,
                  pl.BlockSpec((None, None, tf, d), lambda i, j: (layer, half, j, 0))],
        out_specs=pl.BlockSpec((tm, d), lambda i, j: (i, 0)),
        out_shape=jax.ShapeDtypeStruct((m, d), F32),
        scratch_shapes=[pltpu.VMEM((tm, d), BF16), pltpu.VMEM((tm, d), F32)],
        compiler_params=_params(2),
        name="ffn_half_step",
    )(x, g.reshape(1, d), wg, wu, wd)


def _matmul_kernel(*refs, has_bias, has_res):
    refs = list(refs)
    lhs_ref, w_ref = refs[0], refs[1]
    pos = 2
    b_ref = res_ref = None
    if has_bias:
        b_ref = refs[pos]
        pos += 1
    if has_res:
        res_ref = refs[pos]
        pos += 1
    o_ref, lhs_bf_ref = refs[pos], refs[pos + 1]

    @pl.when(pl.program_id(1) == 0)
    def _():
        lhs_bf_ref[...] = lhs_ref[...].astype(BF16)

    acc = _dot(lhs_bf_ref[...], w_ref[...])
    if has_bias:
        acc = acc + b_ref[...]
    if has_res:
        acc = res_ref[...] + acc
    o_ref[...] = acc


def _matmul(lhs, w, bias=None, res=None, *, tm, tn, res_row0=0):
    m, k = lhs.shape
    n = w.shape[1]
    off = res_row0 // tm
    in_specs = [pl.BlockSpec((tm, k), lambda i, j: (i, 0)),
                pl.BlockSpec((k, tn), lambda i, j: (0, j))]
    args = [lhs, w]
    if bias is not None:
        in_specs.append(pl.BlockSpec((1, tn), lambda i, j: (0, j)))
        args.append(bias.reshape(1, n))
    aliases = {}
    if res is not None:
        in_specs.append(pl.BlockSpec((tm, tn), lambda i, j: (i + off, j)))
        aliases = {len(args): 0}
        args.append(res)
        out_shape = jax.ShapeDtypeStruct(res.shape, F32)
    else:
        out_shape = jax.ShapeDtypeStruct((m, n), F32)
    return pl.pallas_call(
        functools.partial(_matmul_kernel, has_bias=bias is not None, has_res=res is not None),
        grid=(m // tm, n // tn),
        in_specs=in_specs,
        out_specs=pl.BlockSpec((tm, tn), lambda i, j: (i + off, j)),
        out_shape=out_shape,
        scratch_shapes=[pltpu.VMEM((tm, k), BF16)],
        input_output_aliases=aliases,
        compiler_params=_params(2),
        name="matmul_bias_residual",
    )(*args)


def _pool_group_out(diff, gi, x_ref, w_ref, sc_ref, o_ref):
    c = slice(gi * POOL_GROUP_DIM, (gi + 1) * POOL_GROUP_DIM)
    out = _dot(diff.astype(BF16), w_ref[gi])
    o_ref[:, c] = x_ref[:, c] + out * sc_ref[:, c]


def _pool_prompt_kernel(u_ref, halo_ref, x_ref, w_ref, sc_ref, o_ref, ext_ref, *, tm, tiles_per_seq):
    t_in_seq = pl.program_id(0) % tiles_per_seq
    ext_ref[0:POOL_HALO, :] = jnp.where(t_in_seq == 0, 0.0, halo_ref[...])
    ext_ref[POOL_HALO:, :] = u_ref[...]
    pos = t_in_seq * tm + lax.broadcasted_iota(jnp.int32, (tm, 1), 0)
    for gi, w in enumerate(POOL_WINDOWS):
        c = slice(gi * POOL_GROUP_DIM, (gi + 1) * POOL_GROUP_DIM)
        s = ext_ref[pl.ds(POOL_HALO, tm), c]
        for back in range(1, w):
            s = s + ext_ref[pl.ds(POOL_HALO - back, tm), c]
        cnt = jnp.minimum(w, pos + 1).astype(F32)
        _pool_group_out(s / cnt - u_ref[:, c], gi, x_ref, w_ref, sc_ref, o_ref)


def _pool_prompt(u, x, w_pool, scale):
    d = u.shape[1]
    tm = TM_PROMPT
    halo_blocks = tm // POOL_HALO
    return pl.pallas_call(
        functools.partial(_pool_prompt_kernel, tm=tm, tiles_per_seq=SEQ // tm),
        grid=(N_PROMPT // tm,),
        in_specs=[pl.BlockSpec((tm, d), lambda i: (i, 0)),
                  pl.BlockSpec((POOL_HALO, d), lambda i: (jnp.maximum(i * halo_blocks - 1, 0), 0)),
                  pl.BlockSpec((tm, d), lambda i: (i, 0)),
                  pl.BlockSpec((POOL_GROUPS, POOL_GROUP_DIM, POOL_GROUP_DIM), lambda i: (0, 0, 0)),
                  pl.BlockSpec((1, d), lambda i: (0, 0))],
        out_specs=pl.BlockSpec((tm, d), lambda i: (i, 0)),
        out_shape=jax.ShapeDtypeStruct(x.shape, F32),
        scratch_shapes=[pltpu.VMEM((tm + POOL_HALO, d), F32)],
        input_output_aliases={2: 0},
        compiler_params=_params(1),
        name="pool_prompt",
    )(u, u, x, w_pool, scale.reshape(1, d))


def _pool_sample_kernel(ext_ref, x_ref, w_ref, sc_ref, o_ref):
    for gi, w in enumerate(POOL_WINDOWS):
        c = slice(gi * POOL_GROUP_DIM, (gi + 1) * POOL_GROUP_DIM)
        cur = ext_ref[:, pl.ds(POOL_HALO, DEC_SEQ), c]
        s = cur
        for back in range(1, w):
            s = s + ext_ref[:, pl.ds(POOL_HALO - back, DEC_SEQ), c]
        cnt = min(w, PAST_LEN + 1)
        diff = (s / float(cnt) - cur).reshape(N_SAMPLE, POOL_GROUP_DIM)
        _pool_group_out(diff, gi, x_ref, w_ref, sc_ref, o_ref)


def _pool_sample(ext, x, w_pool, scale):
    d = x.shape[1]
    blk = N_PROMPT // N_SAMPLE
    return pl.pallas_call(
        _pool_sample_kernel,
        grid=(1,),
        in_specs=[pl.BlockSpec(ext.shape, lambda i: (0, 0, 0)),
                  pl.BlockSpec((N_SAMPLE, d), lambda i: (blk, 0)),
                  pl.BlockSpec((POOL_GROUPS, POOL_GROUP_DIM, POOL_GROUP_DIM), lambda i: (0, 0, 0)),
                  pl.BlockSpec((1, d), lambda i: (0, 0))],
        out_specs=pl.BlockSpec((N_SAMPLE, d), lambda i: (blk, 0)),
        out_shape=jax.ShapeDtypeStruct(x.shape, F32),
        input_output_aliases={1: 0},
        compiler_params=_params(1),
        name="pool_sample",
    )(ext, x, w_pool, scale.reshape(1, d))


def _t5_bucket_table(tq):
    qi = np.arange(tq)[:, None]
    kj = np.arange(WINDOW + tq)[None, :]
    dist = qi + WINDOW - kj
    exact = T5_BUCKETS // 2
    ratio = np.log(np.maximum(dist, 1) / exact) / math.log(T5_MAX_DISTANCE / exact)
    large = np.minimum(exact + (ratio * (T5_BUCKETS - exact)).astype(np.int64), T5_BUCKETS - 1)
    bucket = np.where(dist < exact, dist, large)
    valid = (dist >= 0) & (dist < WINDOW)
    table = np.where(valid, bucket, -1).astype(np.int32)
    return table[:, :WINDOW], table[:, WINDOW:]


def _attn_kernel(rb_ref, sink_ref, bkt_p_ref, bkt_c_ref, q_ref, kc_ref, vc_ref, kp_ref, vp_ref,
                 o_ref, bias_p_ref, bias_c_ref, sink_col_ref, *, tq, blocks_per_seq):
    step = pl.program_id(0)

    @pl.when(step == 0)
    def _():
        bkt_p = bkt_p_ref[...]
        bkt_c = bkt_c_ref[...]
        bias_p_ref[...] = jnp.zeros_like(bias_p_ref)
        bias_c_ref[...] = jnp.zeros_like(bias_c_ref)

        def add_bucket(b, carry):
            eq_p = bkt_p == b
            eq_c = bkt_c == b
            for h in range(N_HEADS):
                kvh, g = divmod(h, GQA_GROUP)
                r = slice(g * tq, (g + 1) * tq)
                val = rb_ref[b, h]
                bias_p_ref[kvh, r, :] += jnp.where(eq_p, val, 0.0)
                bias_c_ref[kvh, r, :] += jnp.where(eq_c, val, 0.0)
            return carry

        lax.fori_loop(0, T5_BUCKETS, add_bucket, 0)
        for h in range(N_HEADS):
            kvh, g = divmod(h, GQA_GROUP)
            r = slice(g * tq, (g + 1) * tq)
            bias_p_ref[kvh, r, :] = jnp.where(bkt_p < 0, NEG_INF, bias_p_ref[kvh, r, :])
            bias_c_ref[kvh, r, :] = jnp.where(bkt_c < 0, NEG_INF, bias_c_ref[kvh, r, :])
            sink_col_ref[kvh, r, :] = jnp.full((tq, 1), sink_ref[h], F32)

    no_prev = (step % blocks_per_seq == 0) if blocks_per_seq else None
    def kv_head(kvh):
        heads = [kvh * GQA_GROUP + g for g in range(GQA_GROUP)]
        qs = jnp.concatenate([q_ref[:, h * HEAD_DIM:(h + 1) * HEAD_DIM] for h in heads], axis=0).astype(BF16)
        c = slice(kvh * HEAD_DIM, (kvh + 1) * HEAD_DIM)
        qk_p = _dot_nt(qs, kp_ref[:, c].astype(BF16))
        qk_c = _dot_nt(qs, kc_ref[:, c].astype(BF16))
        yield
        s_p = qk_p * ATT_SCALE + bias_p_ref[kvh]
        if no_prev is not None:
            s_p = jnp.where(no_prev, NEG_INF, s_p)
        s_c = qk_c * ATT_SCALE + bias_c_ref[kvh]
        sink = sink_col_ref[kvh]
        m = jnp.maximum(jnp.maximum(jnp.max(s_p, axis=-1, keepdims=True),
                                    jnp.max(s_c, axis=-1, keepdims=True)), sink)
        p_p = jnp.exp(s_p - m)
        p_c = jnp.exp(s_c - m)
        den = (jnp.sum(p_p, axis=-1, keepdims=True) + jnp.sum(p_c, axis=-1, keepdims=True)
               + jnp.exp(sink - m))
        o_p = _dot(p_p.astype(BF16), vp_ref[:, c].astype(BF16))
        o_c = _dot(p_c.astype(BF16), vc_ref[:, c].astype(BF16))
        yield
        o = (o_p + o_c) / den
        for g, h in enumerate(heads):
            o_ref[:, h * HEAD_DIM:(h + 1) * HEAD_DIM] = o[g * tq:(g + 1) * tq, :]

    _round_robin([kv_head(kvh) for kvh in range(N_KV_HEADS)])


def _attention(qkv, rel_bias, sinks, *, tq, n_blocks, row0, blocks_per_seq, k_prev=None, v_prev=None):
    bkt_p, bkt_c = _t5_bucket_table(tq)
    off = row0 // tq
    kcol = ATT_DIM // KV_DIM
    smem = pl.BlockSpec(memory_space=pltpu.SMEM)
    if k_prev is None:
        prev_rows = lambda i: jnp.maximum(i - 1, 0)
        kp_spec = pl.BlockSpec((WINDOW, KV_DIM), lambda i: (prev_rows(i), kcol))
        vp_spec = pl.BlockSpec((WINDOW, KV_DIM), lambda i: (prev_rows(i), kcol + 1))
        k_prev = v_prev = qkv
    else:
        kp_spec = vp_spec = pl.BlockSpec((None, WINDOW, KV_DIM), lambda i: (i, 0, 0))
    rows = GQA_GROUP * tq
    return pl.pallas_call(
        functools.partial(_attn_kernel, tq=tq, blocks_per_seq=blocks_per_seq),
        grid=(n_blocks,),
        in_specs=[smem, smem,
                  pl.BlockSpec((tq, WINDOW), lambda i: (0, 0)),
                  pl.BlockSpec((tq, tq), lambda i: (0, 0)),
                  pl.BlockSpec((tq, ATT_DIM), lambda i: (i + off, 0)),
                  pl.BlockSpec((tq, KV_DIM), lambda i: (i + off, kcol)),
                  pl.BlockSpec((tq, KV_DIM), lambda i: (i + off, kcol + 1)),
                  kp_spec, vp_spec],
        out_specs=pl.BlockSpec((tq, ATT_DIM), lambda i: (i, 0)),
        out_shape=jax.ShapeDtypeStruct((n_blocks * tq, ATT_DIM), F32),
        scratch_shapes=[pltpu.VMEM((N_KV_HEADS, rows, WINDOW), F32),
                        pltpu.VMEM((N_KV_HEADS, rows, tq), F32),
                        pltpu.VMEM((N_KV_HEADS, rows, 1), F32)],
        compiler_params=_params(1),
        name="swa_attention",
    )(rel_bias, sinks, jnp.asarray(bkt_p), jnp.asarray(bkt_c), qkv, qkv, qkv, k_prev, v_prev)


def _rwkv_proj_kernel(u_ref, up_ref, mu_ref, wr_ref, wk_ref, wv_ref, w1_ref, w2_ref, w0_ref,
                      a1_ref, a2_ref, a0_ref, g1_ref, g2_ref,
                      r_ref, k_ref, v_ref, ld_ref, a_ref, g_ref,
                      xr_ref, xk_ref, xv_ref, hw_ref, ha_ref, hg_ref):
    @pl.when(pl.program_id(1) == 0)
    def _():
        u = u_ref[...]
        dx = up_ref[...] - u
        mix = lambda i: (u + dx * mu_ref[i:i + 1, :]).astype(BF16)
        xr_ref[...] = mix(0)
        hw_ref[...] = jnp.tanh(_dot(mix(1), w1_ref[...])).astype(BF16)
        xk_ref[...] = mix(2)
        xv_ref[...] = mix(3)
        ha_ref[...] = _dot(mix(4), a1_ref[...]).astype(BF16)
        hg_ref[...] = jax.nn.sigmoid(_dot(mix(5), g1_ref[...])).astype(BF16)

    r_ref[...] = _dot(xr_ref[...], wr_ref[...])
    k_ref[...] = _dot(xk_ref[...], wk_ref[...])
    v_ref[...] = _dot(xv_ref[...], wv_ref[...])
    z = -(w0_ref[...] + _dot(hw_ref[...], w2_ref[...]))
    softplus = jnp.maximum(z, 0.0) + jnp.log1p(jnp.exp(-jnp.abs(z)))
    ld_ref[...] = -jnp.exp(-softplus - 0.5)
    a_ref[...] = jax.nn.sigmoid(a0_ref[...] + _dot(ha_ref[...], a2_ref[...]))
    g_ref[...] = _dot(hg_ref[...], g2_ref[...])


def _rwkv_proj(u, u_prev, mu, w_r, w_k, w_v, w1, w2, w0, a1, a2, a0, g1, g2):
    m, d = u.shape
    tm, tn = TM_PROJ, TN_PROJ
    gl = g1.shape[1]
    row = lambda i, j: (i, 0)
    col = lambda i, j: (0, j)
    fixed = lambda i, j: (0, 0)
    out = jax.ShapeDtypeStruct((m, d), F32)
    return pl.pallas_call(
        _rwkv_proj_kernel,
        grid=(m // tm, d // tn),
        in_specs=[pl.BlockSpec((tm, d), row), pl.BlockSpec((tm, d), row),
                  pl.BlockSpec((6, d), fixed),
                  pl.BlockSpec((d, tn), col), pl.BlockSpec((d, tn), col), pl.BlockSpec((d, tn), col),
                  pl.BlockSpec((d, LORA_PAD), fixed), pl.BlockSpec((LORA_PAD, tn), col), pl.BlockSpec((1, tn), col),
                  pl.BlockSpec((d, LORA_PAD), fixed), pl.BlockSpec((LORA_PAD, tn), col), pl.BlockSpec((1, tn), col),
                  pl.BlockSpec((d, gl), fixed), pl.BlockSpec((gl, tn), col)],
        out_specs=[pl.BlockSpec((tm, tn), lambda i, j: (i, j))] * 6,
        out_shape=[out] * 6,
        scratch_shapes=[pltpu.VMEM((tm, d), BF16)] * 3
        + [pltpu.VMEM((tm, LORA_PAD), BF16)] * 2 + [pltpu.VMEM((tm, gl), BF16)],
        compiler_params=_params(2),
        name="rwkv_projections",
    )(u, u_prev, mu, w_r, w_k, w_v, w1, w2, w0.reshape(1, d), a1, a2, a0.reshape(1, d), g1, g2)


def _block_diag_rows(x, rows_per_block, cols_per_block):
    tall = jnp.concatenate([x] * SCAN_HEADS, axis=0)
    rh = lax.broadcasted_iota(jnp.int32, tall.shape, 0) // rows_per_block
    ch = lax.broadcasted_iota(jnp.int32, tall.shape, 1) // cols_per_block
    return jnp.where(rh == ch, tall, 0.0).astype(BF16)


def _scan_kernel(*refs, chunk, has_init):
    (r_ref, k_ref, v_ref, ld_ref, a_ref, g_ref, kk_ref, ka_ref, rk_ref, lnw_ref, lnb_ref) = refs[:11]
    refs = refs[11:]
    if has_init:
        s0_ref, refs = refs[0], refs[1:]
    z_ref, sout_ref, s_ref = refs
    c_idx = pl.program_id(1)
    cc, n, w = chunk, RWKV_N, SCAN_LANES
    n_groups = RWKV_HEADS // SCAN_HEADS

    @pl.when(c_idx == 0)
    def _():
        s_ref[...] = jnp.zeros_like(s_ref)
        if has_init:
            for h in range(RWKV_HEADS):
                gi, hh = divmod(h, SCAN_HEADS)
                s_ref[gi, hh * n:(hh + 1) * n, hh * n:(hh + 1) * n] = s0_ref[h]

    def iota(shape, axis):
        return lax.broadcasted_iota(jnp.int32, shape, axis)

    ones_bd = (iota((w, w), 0) // n == iota((w, w), 1) // n).astype(BF16)
    tril = (iota((cc, cc), 0) >= iota((cc, cc), 1)).astype(BF16)
    t_row = iota((cc, SCAN_HEADS * cc), 0)
    s_col = iota((cc, SCAN_HEADS * cc), 1) % cc
    strict = t_row > s_col
    causal = t_row >= s_col
    head_diag = iota((w, w), 0) // n == iota((w, w), 1) // n

    def split2(x):
        hi = x.astype(BF16)
        return hi, (x - hi.astype(F32)).astype(BF16)

    def head_sum(x):
        hi, lo = split2(x)
        both = _dot(jnp.concatenate([hi, lo], axis=0), ones_bd)
        yield
        return both[:cc] + both[cc:]

    def bd_lanes(x):
        return _block_diag_rows(x, cc, n)

    def bd_tokens(x):
        return _block_diag_rows(x, cc, cc)

    def group(gi):
        c = slice(gi * w, (gi + 1) * w)
        r, k, v, ld, a = r_ref[:, c], k_ref[:, c], v_ref[:, c], ld_ref[:, c], a_ref[:, c]

        p1 = ld.astype(BF16)
        r1 = ld - p1.astype(F32)
        p2 = r1.astype(BF16)
        p3 = (r1 - p2.astype(F32)).astype(BF16)
        cum = _dot(tril, jnp.concatenate([p1, p2, p3], axis=1))
        yield
        kk = k * kk_ref[:, c]
        kk_sq = yield from head_sum(kk * kk)
        kap = kk / jnp.maximum(jnp.sqrt(kk_sq), 1e-12)
        kmod = k * (1.0 + (a - 1.0) * ka_ref[:, c])
        b = kap * a

        lcum = cum[:, :w] + cum[:, w:2 * w] + cum[:, 2 * w:]
        lend = lcum[cc - 1:cc, :]
        e_inc = jnp.exp(lcum)
        e_exc = jnp.exp(lcum - ld)
        e_neg = jnp.exp(-lcum)
        e_end = jnp.exp(lend - lcum)

        kq = kap * e_exc
        rq = r * e_inc
        qr = jnp.concatenate([kq, rq], axis=0).astype(BF16)
        bk_bd = jnp.concatenate([bd_lanes(b * e_neg), bd_lanes(kmod * e_neg)], axis=0)
        gram = _dot_nt(qr, bk_bd)
        yield
        s_bd = s_ref[gi]
        qs = _dot_nt(qr, s_bd.astype(BF16))
        yield
        hc = SCAN_HEADS * cc
        a_w = jnp.where(strict, gram[:cc, :hc], 0.0)
        bk_w = jnp.where(strict, gram[:cc, hc:], 0.0)
        cb_w = jnp.where(causal, gram[cc:, :hc], 0.0)
        ck_w = jnp.where(causal, gram[cc:, hc:], 0.0)
        v_bd = bd_lanes(v)

        bv = _dot(bk_w.astype(BF16), v_bd)
        yield
        x = -(qs[:cc] + bv)
        ax = _dot(a_w.astype(BF16), bd_lanes(x))
        yield
        x = x - ax
        a_pow = a_w
        power = 2
        while power < cc:
            a_pow = _dot(a_pow.astype(BF16), bd_tokens(a_pow))
            yield
            ax = _dot(a_pow.astype(BF16), bd_lanes(x))
            yield
            x = x + ax
            power *= 2
        u = x

        y_in = _dot(jnp.concatenate([cb_w, ck_w], axis=1).astype(BF16),
                    jnp.concatenate([bd_lanes(u), v_bd], axis=0))
        yield
        upd = _dot_tn(jnp.concatenate([u, v], axis=0).astype(BF16),
                      jnp.concatenate([b * e_end, kmod * e_end], axis=0).astype(BF16))
        yield
        s_ref[gi] = s_bd * jnp.exp(lend) + jnp.where(head_diag, upd, 0.0)

        y = qs[cc:] + y_in
        mean = (yield from head_sum(y)) * (1.0 / n)
        dev = y - mean
        var = (yield from head_sum(dev * dev)) * (1.0 / n)
        yn = dev * lax.rsqrt(var + GN_EPS) * lnw_ref[:, c] + lnb_ref[:, c]
        bonus = (yield from head_sum(r * kmod * rk_ref[:, c])) * v
        z_ref[:, c] = (yn + bonus) * g_ref[:, c]

    _round_robin([group(gi) for gi in range(n_groups)])

    @pl.when(c_idx == pl.num_programs(1) - 1)
    def _():
        for h in range(RWKV_HEADS):
            gi, hh = divmod(h, SCAN_HEADS)
            sout_ref[h] = s_ref[gi, hh * n:(hh + 1) * n, hh * n:(hh + 1) * n]


def _rwkv_scan(proj, k_k, k_a, r_k, ln_w, ln_b, *, n_seq, chunk, n_chunks, row0, s0=None):
    d = D_MODEL
    off = row0 // chunk
    tok = pl.BlockSpec((chunk, d), lambda b, c: (off + b * n_chunks + c, 0))
    par = pl.BlockSpec((1, d), lambda b, c: (0, 0))
    state = pl.BlockSpec((None, RWKV_HEADS, RWKV_N, RWKV_N), lambda b, c: (b, 0, 0, 0))
    in_specs = [tok] * 6 + [par] * 5
    args = list(proj) + [p.reshape(1, d) for p in (k_k, k_a, r_k, ln_w, ln_b)]
    if s0 is not None:
        in_specs.append(state)
        args.append(s0)
    n_groups = RWKV_HEADS // SCAN_HEADS
    return pl.pallas_call(
        functools.partial(_scan_kernel, chunk=chunk, has_init=s0 is not None),
        grid=(n_seq, n_chunks),
        in_specs=in_specs,
        out_specs=[pl.BlockSpec((chunk, d), lambda b, c: (b * n_chunks + c, 0)), state],
        out_shape=[jax.ShapeDtypeStruct((n_seq * n_chunks * chunk, d), F32),
                   jax.ShapeDtypeStruct((n_seq, RWKV_HEADS, RWKV_N, RWKV_N), F32)],
        scratch_shapes=[pltpu.VMEM((n_groups, SCAN_LANES, SCAN_LANES), F32)],
        compiler_params=_params(2),
        name="rwkv_scan",
    )(*args)


def _pad_cols(w):
    return jnp.pad(w, ((0, 0), (0, LORA_PAD - w.shape[1])))


def _pad_rows(w):
    return jnp.pad(w, ((0, LORA_PAD - w.shape[0]), (0, 0)))


def kernel(x_prompt, x_sample, state_pool, cache_win_k, cache_win_v, state_shift, state_wkv, norm_ffn1, norm_mix, norm_ffn2, norm_final, ffn_w_gate, ffn_w_up, ffn_w_down, pool_w, pool_scale, att_w_qkv, att_b_qkv, att_w_o, att_b_o, att_sinks, rel_bias, rwkv_mu, rwkv_w_r, rwkv_w_k, rwkv_w_v, rwkv_w_o, rwkv_w0, rwkv_w1, rwkv_w2, rwkv_a0, rwkv_a1, rwkv_a2, rwkv_g1, rwkv_g2, rwkv_k_k, rwkv_k_a, rwkv_r_k, rwkv_ln_w, rwkv_ln_b):
    d = D_MODEL
    x = jnp.concatenate([x_prompt.reshape(N_PROMPT, d), x_sample.reshape(N_SAMPLE, d)], axis=0)
    wg, wu, wd = (w.astype(BF16) for w in (ffn_w_gate, ffn_w_up, ffn_w_down))
    pool_p, pool_s, wk_p, wv_p, wk_s, wv_s, sh_p, sh_s, wkv_p, wkv_s = ([] for _ in range(10))

    for l in range(DEPTH):
        j, kind = divmod(l, N_MIXERS)
        x = _ffn(x, norm_ffn1[l], wg, wu, wd, l, 0)
        u = _norm_rows(x, norm_mix[l], tm=TM_TOK, row0=0, rows=N_TOK)
        u_p = u[:N_PROMPT].reshape(BATCH, SEQ, d)
        u_s = u[N_PROMPT:].reshape(DEC_BATCH, DEC_SEQ, d)
        if kind == 0:
            w_pool = pool_w[j].astype(BF16)
            ext = jnp.concatenate([jnp.zeros((DEC_BATCH, 1, d), F32), state_pool[j], u_s], axis=1)
            x = _pool_prompt(u, x, w_pool, pool_scale[j])
            x = _pool_sample(ext, x, w_pool, pool_scale[j])
            pool_p.append(u_p[:, -POOL_BUF:])
            pool_s.append(ext[:, -POOL_BUF:])
        elif kind == 1:
            qkv = _matmul(u, att_w_qkv[j].astype(BF16), att_b_qkv[j], tm=TM_TOK, tn=QKV_DIM // 5)
            k_buf = cache_win_k[j].reshape(DEC_BATCH, WINDOW, KV_DIM)
            v_buf = cache_win_v[j].reshape(DEC_BATCH, WINDOW, KV_DIM)
            o_p = _attention(qkv, rel_bias, att_sinks[j], tq=WINDOW, n_blocks=N_PROMPT // WINDOW,
                             row0=0, blocks_per_seq=SEQ // WINDOW)
            o_s = _attention(qkv, rel_bias, att_sinks[j], tq=DEC_SEQ, n_blocks=DEC_BATCH,
                             row0=N_PROMPT, blocks_per_seq=0, k_prev=k_buf, v_prev=v_buf)
            w_o = att_w_o[j].astype(BF16)
            x = _matmul(o_p, w_o, att_b_o[j], x, tm=TM_PROMPT, tn=d, res_row0=0)
            x = _matmul(o_s, w_o, att_b_o[j], x, tm=N_SAMPLE, tn=d, res_row0=N_PROMPT)
            k_new = qkv[:, ATT_DIM:ATT_DIM + KV_DIM]
            v_new = qkv[:, ATT_DIM + KV_DIM:]
            kv_shape = (WINDOW, N_KV_HEADS, HEAD_DIM)
            wk_p.append(k_new[:N_PROMPT].reshape(BATCH, SEQ, KV_DIM)[:, -WINDOW:].reshape(BATCH, *kv_shape))
            wv_p.append(v_new[:N_PROMPT].reshape(BATCH, SEQ, KV_DIM)[:, -WINDOW:].reshape(BATCH, *kv_shape))
            k_s = jnp.concatenate([k_buf, k_new[N_PROMPT:].reshape(DEC_BATCH, DEC_SEQ, KV_DIM)], axis=1)
            v_s = jnp.concatenate([v_buf, v_new[N_PROMPT:].reshape(DEC_BATCH, DEC_SEQ, KV_DIM)], axis=1)
            wk_s.append(k_s[:, -WINDOW:].reshape(DEC_BATCH, *kv_shape))
            wv_s.append(v_s[:, -WINDOW:].reshape(DEC_BATCH, *kv_shape))
        else:
            prev_p = jnp.concatenate([jnp.zeros((BATCH, 1, d), F32), u_p[:, :-1]], axis=1)
            prev_s = jnp.concatenate([state_shift[j][:, None], u_s[:, :-1]], axis=1)
            u_prev = jnp.concatenate([prev_p.reshape(N_PROMPT, d), prev_s.reshape(N_SAMPLE, d)], axis=0)
            proj = _rwkv_proj(u, u_prev, rwkv_mu[j],
                              rwkv_w_r[j].astype(BF16), rwkv_w_k[j].astype(BF16), rwkv_w_v[j].astype(BF16),
                              _pad_cols(rwkv_w1[j]).astype(BF16), _pad_rows(rwkv_w2[j]).astype(BF16), rwkv_w0[j],
                              _pad_cols(rwkv_a1[j]).astype(BF16), _pad_rows(rwkv_a2[j]).astype(BF16), rwkv_a0[j],
                              rwkv_g1[j].astype(BF16), rwkv_g2[j].astype(BF16))
            head_params = (rwkv_k_k[j], rwkv_k_a[j], rwkv_r_k[j], rwkv_ln_w[j], rwkv_ln_b[j])
            z_p, s_p = _rwkv_scan(proj, *head_params, n_seq=BATCH, chunk=SCAN_CHUNK,
                                  n_chunks=SEQ // SCAN_CHUNK, row0=0)
            z_s, s_s = _rwkv_scan(proj, *head_params, n_seq=DEC_BATCH, chunk=DEC_SEQ,
                                  n_chunks=1, row0=N_PROMPT, s0=state_wkv[j])
            w_o = rwkv_w_o[j].astype(BF16)
            x = _matmul(z_p, w_o, None, x, tm=TM_PROMPT, tn=d, res_row0=0)
            x = _matmul(z_s, w_o, None, x, tm=N_SAMPLE, tn=d, res_row0=N_PROMPT)
            sh_p.append(u_p[:, -1])
            sh_s.append(u_s[:, -1])
            wkv_p.append(s_p)
            wkv_s.append(s_s)
        x = _ffn(x, norm_ffn2[l], wg, wu, wd, l, 1)

    y_p = _norm_rows(x, norm_final, tm=TM_PROMPT, row0=0, rows=N_PROMPT)
    y_s = _norm_rows(x, norm_final, tm=N_SAMPLE, row0=N_PROMPT, rows=N_SAMPLE)
    return (y_p.reshape(BATCH, SEQ, d), y_s.reshape(DEC_BATCH, DEC_SEQ, d),
            jnp.stack(pool_p), jnp.stack(pool_s),
            jnp.stack(wk_p), jnp.stack(wv_p), jnp.stack(wk_s), jnp.stack(wv_s),
            jnp.stack(sh_p), jnp.stack(sh_s),
            jnp.stack(wkv_p), jnp.stack(wkv_s))
```

```python
import functools
import math

import numpy as np
import jax
import jax.numpy as jnp
from jax import lax
from jax.experimental import pallas as pl
from jax.experimental.pallas import tpu as pltpu

F32 = jnp.float32
BF16 = jnp.bfloat16

D_MODEL = 2048
BATCH = 2
SEQ = 4096
DEPTH = 4
DEC_BATCH = 32
DEC_SEQ = 8
PAST_LEN = 16384
N_MIXERS = 3
RMS_EPS = 1e-6
D_FF = 5632
POOL_WINDOWS = (2, 4, 8, 16)
POOL_GROUPS = 4
POOL_GROUP_DIM = D_MODEL // POOL_GROUPS
POOL_BUF = max(POOL_WINDOWS) - 1
POOL_HALO = POOL_BUF + 1
HEAD_DIM = 64
N_HEADS = D_MODEL // HEAD_DIM
N_KV_HEADS = 4
GQA_GROUP = N_HEADS // N_KV_HEADS
ATT_DIM = N_HEADS * HEAD_DIM
KV_DIM = N_KV_HEADS * HEAD_DIM
QKV_DIM = ATT_DIM + 2 * KV_DIM
WINDOW = 128
ATT_SCALE = HEAD_DIM ** -0.5
T5_BUCKETS = 32
T5_MAX_DISTANCE = 128
NEG_INF = -1e30
RWKV_N = 64
RWKV_HEADS = D_MODEL // RWKV_N
GN_EPS = 64e-5
LORA_PAD = 128

N_PROMPT = BATCH * SEQ
N_SAMPLE = DEC_BATCH * DEC_SEQ
N_TOK = N_PROMPT + N_SAMPLE

VMEM_LIMIT_BYTES = 56 * 1024 * 1024

TM_TOK = 704
TM_PROMPT = 512
TM_POOL = 256
SHIFT_HALO = 8
TF_FFN = 512
TN_PROJ = 512
SCAN_CHUNK = 64
SCAN_HEADS = 4
SCAN_LANES = SCAN_HEADS * RWKV_N


def _params(n_axes):
    return pltpu.CompilerParams(dimension_semantics=("arbitrary",) * n_axes,
                                vmem_limit_bytes=VMEM_LIMIT_BYTES)


def _dot(a, b):
    return jnp.dot(a, b, preferred_element_type=F32)


def _dot_nt(a, b):
    return lax.dot_general(a, b, (((1,), (1,)), ((), ())), preferred_element_type=F32)


def _dot_tn(a, b):
    return lax.dot_general(a, b, (((0,), (0,)), ((), ())), preferred_element_type=F32)


def _round_robin(gens):
    live = list(gens)
    while live:
        still = []
        for gen in live:
            try:
                next(gen)
                still.append(gen)
            except StopIteration:
                pass
        live = still


def _rms_norm(x, g):
    return x * lax.rsqrt(jnp.mean(x * x, axis=-1, keepdims=True) + RMS_EPS) * g


def _norm_kernel(x_ref, g_ref, o_ref):
    o_ref[...] = _rms_norm(x_ref[...], g_ref[...])


def _norm_rows(x, g, *, tm, row0, rows):
    d = x.shape[1]
    off = row0 // tm
    return pl.pallas_call(
        _norm_kernel,
        grid=(rows // tm,),
        in_specs=[pl.BlockSpec((tm, d), lambda i: (i + off, 0)),
                  pl.BlockSpec((1, d), lambda i: (0, 0))],
        out_specs=pl.BlockSpec((tm, d), lambda i: (i, 0)),
        out_shape=jax.ShapeDtypeStruct((rows, d), F32),
        compiler_params=_params(1),
        name="rms_norm",
    )(x, g.reshape(1, d))


def _norm_tails(x, g):
    d = x.shape[1]
    blocks_per_seq = SEQ // POOL_HALO
    out = pl.pallas_call(
        _norm_kernel,
        grid=(BATCH,),
        in_specs=[pl.BlockSpec((POOL_HALO, d), lambda b: ((b + 1) * blocks_per_seq - 1, 0)),
                  pl.BlockSpec((1, d), lambda b: (0, 0))],
        out_specs=pl.BlockSpec((POOL_HALO, d), lambda b: (b, 0)),
        out_shape=jax.ShapeDtypeStruct((BATCH * POOL_HALO, d), F32),
        compiler_params=_params(1),
        name="rms_norm_tails",
    )(x, g.reshape(1, d))
    return out.reshape(BATCH, POOL_HALO, d)


def _ffn_kernel(x_ref, g_ref, wg_ref, wu_ref, wd_ref, o_ref, xn_ref, acc_ref):
    j = pl.program_id(1)

    @pl.when(j == 0)
    def _():
        xn_ref[...] = _rms_norm(x_ref[...], g_ref[...]).astype(BF16)
        acc_ref[...] = jnp.zeros_like(acc_ref)

    xn = xn_ref[...]
    gate = _dot(xn, wg_ref[...])
    up = _dot(xn, wu_ref[...])
    h = (gate * jax.nn.sigmoid(gate) * up).astype(BF16)
    acc_ref[...] += _dot(h, wd_ref[...])

    @pl.when(j == pl.num_programs(1) - 1)
    def _():
        o_ref[...] = x_ref[...] + 0.5 * acc_ref[...]


def _ffn(x, g, wg, wu, wd, layer, half):
    m, d = x.shape
    tm, tf = TM_TOK, TF_FFN
    w_in = pl.BlockSpec((None, None, d, tf), lambda i, j: (layer, half, 0, j))
    return pl.pallas_call(
        _ffn_kernel,
        grid=(m // tm, D_FF // tf),
        in_specs=[pl.BlockSpec((tm, d), lambda i, j: (i, 0)),
                  pl.BlockSpec((1, d), lambda i, j: (0, 0)),
                  w_in, w_in,
                  pl.BlockSpec((None, None, tf, d), lambda i, j: (layer, half, j, 0))],
        out_specs=pl.BlockSpec((tm, d), lambda i, j: (i, 0)),
        out_shape=jax.ShapeDtypeStruct((m, d), F32),
        scratch_shapes=[pltpu.VMEM((tm, d), BF16), pltpu.VMEM((tm, d), F32)],
        compiler_params=_params(2),
        name="ffn_half_step",
    )(x, g.reshape(1, d), wg, wu, wd)


def _matmul_kernel(*refs, has_gain, has_bias, has_res):
    refs = list(refs)
    lhs_ref, w_ref = refs[0], refs[1]
    pos = 2
    g_ref = b_ref = res_ref = None
    if has_gain:
        g_ref = refs[pos]
        pos += 1
    if has_bias:
        b_ref = refs[pos]
        pos += 1
    if has_res:
        res_ref = refs[pos]
        pos += 1
    o_ref, lhs_bf_ref = refs[pos], refs[pos + 1]

    @pl.when(pl.program_id(1) == 0)
    def _():
        lhs = lhs_ref[...]
        if has_gain:
            lhs = _rms_norm(lhs, g_ref[...])
        lhs_bf_ref[...] = lhs.astype(BF16)

    acc = _dot(lhs_bf_ref[...], w_ref[...])
    if has_bias:
        acc = acc + b_ref[...]
    if has_res:
        acc = res_ref[...] + acc
    o_ref[...] = acc


def _matmul(lhs, w, bias=None, res=None, *, tm, tn, res_row0=0, gain=None):
    m, k = lhs.shape
    n = w.shape[1]
    off = res_row0 // tm
    in_specs = [pl.BlockSpec((tm, k), lambda i, j: (i, 0)),
                pl.BlockSpec((k, tn), lambda i, j: (0, j))]
    args = [lhs, w]
    if gain is not None:
        in_specs.append(pl.BlockSpec((1, k), lambda i, j: (0, 0)))
        args.append(gain.reshape(1, k))
    if bias is not None:
        in_specs.append(pl.BlockSpec((1, tn), lambda i, j: (0, j)))
        args.append(bias.reshape(1, n))
    aliases = {}
    if res is not None:
        in_specs.append(pl.BlockSpec((tm, tn), lambda i, j: (i + off, j)))
        aliases = {len(args): 0}
        args.append(res)
        out_shape = jax.ShapeDtypeStruct(res.shape, F32)
    else:
        out_shape = jax.ShapeDtypeStruct((m, n), F32)
    return pl.pallas_call(
        functools.partial(_matmul_kernel, has_gain=gain is not None, has_bias=bias is not None,
                          has_res=res is not None),
        grid=(m // tm, n // tn),
        in_specs=in_specs,
        out_specs=pl.BlockSpec((tm, tn), lambda i, j: (i + off, j)),
        out_shape=out_shape,
        scratch_shapes=[pltpu.VMEM((tm, k), BF16)],
        input_output_aliases=aliases,
        compiler_params=_params(2),
        name="matmul_bias_residual",
    )(*args)


def _pool_group_out(diff, gi, x_ref, w_ref, sc_ref, o_ref):
    c = slice(gi * POOL_GROUP_DIM, (gi + 1) * POOL_GROUP_DIM)
    out = _dot(diff.astype(BF16), w_ref[gi])
    o_ref[:, c] = x_ref[:, c] + out * sc_ref[:, c]


def _pool_prompt_kernel(x_ref, halo_ref, g_ref, w_ref, sc_ref, o_ref, ext_ref, *, tm, tiles_per_seq, n_tiles):
    i = pl.program_id(0)

    @pl.when(i < n_tiles)
    def _():
        t_in_seq = i % tiles_per_seq
        g = g_ref[...]
        ext_ref[0:POOL_HALO, :] = jnp.where(t_in_seq == 0, 0.0, _rms_norm(halo_ref[...], g))
        ext_ref[POOL_HALO:, :] = _rms_norm(x_ref[...], g)
        pos = t_in_seq * tm + lax.broadcasted_iota(jnp.int32, (tm, 1), 0)
        for gi, w in enumerate(POOL_WINDOWS):
            c = slice(gi * POOL_GROUP_DIM, (gi + 1) * POOL_GROUP_DIM)
            cur = ext_ref[pl.ds(POOL_HALO, tm), c]
            s = cur
            for back in range(1, w):
                s = s + ext_ref[pl.ds(POOL_HALO - back, tm), c]
            cnt = jnp.minimum(w, pos + 1).astype(F32)
            _pool_group_out(s / cnt - cur, gi, x_ref, w_ref, sc_ref, o_ref)

    @pl.when(i >= n_tiles)
    def _():
        o_ref[...] = x_ref[...]


def _pool_prompt(x, gain, w_pool, scale):
    d = x.shape[1]
    tm = TM_POOL
    halo_blocks = tm // POOL_HALO
    return pl.pallas_call(
        functools.partial(_pool_prompt_kernel, tm=tm, tiles_per_seq=SEQ // tm, n_tiles=N_PROMPT // tm),
        grid=(N_TOK // tm,),
        in_specs=[pl.BlockSpec((tm, d), lambda i: (i, 0)),
                  pl.BlockSpec((POOL_HALO, d), lambda i: (jnp.maximum(i * halo_blocks - 1, 0), 0)),
                  pl.BlockSpec((1, d), lambda i: (0, 0)),
                  pl.BlockSpec((POOL_GROUPS, POOL_GROUP_DIM, POOL_GROUP_DIM), lambda i: (0, 0, 0)),
                  pl.BlockSpec((1, d), lambda i: (0, 0))],
        out_specs=pl.BlockSpec((tm, d), lambda i: (i, 0)),
        out_shape=jax.ShapeDtypeStruct(x.shape, F32),
        scratch_shapes=[pltpu.VMEM((tm + POOL_HALO, d), F32)],
        compiler_params=_params(1),
        name="pool_prompt",
    )(x, x, gain.reshape(1, d), w_pool, scale.reshape(1, d))


def _pool_sample_kernel(ext_ref, x_ref, w_ref, sc_ref, o_ref):
    for gi, w in enumerate(POOL_WINDOWS):
        c = slice(gi * POOL_GROUP_DIM, (gi + 1) * POOL_GROUP_DIM)
        cur = ext_ref[:, pl.ds(POOL_HALO, DEC_SEQ), c]
        s = cur
        for back in range(1, w):
            s = s + ext_ref[:, pl.ds(POOL_HALO - back, DEC_SEQ), c]
        cnt = min(w, PAST_LEN + 1)
        diff = (s / float(cnt) - cur).reshape(N_SAMPLE, POOL_GROUP_DIM)
        _pool_group_out(diff, gi, x_ref, w_ref, sc_ref, o_ref)


def _pool_sample(ext, x, w_pool, scale):
    d = x.shape[1]
    blk = N_PROMPT // N_SAMPLE
    return pl.pallas_call(
        _pool_sample_kernel,
        grid=(1,),
        in_specs=[pl.BlockSpec(ext.shape, lambda i: (0, 0, 0)),
                  pl.BlockSpec((N_SAMPLE, d), lambda i: (blk, 0)),
                  pl.BlockSpec((POOL_GROUPS, POOL_GROUP_DIM, POOL_GROUP_DIM), lambda i: (0, 0, 0)),
                  pl.BlockSpec((1, d), lambda i: (0, 0))],
        out_specs=pl.BlockSpec((N_SAMPLE, d), lambda i: (blk, 0)),
        out_shape=jax.ShapeDtypeStruct(x.shape, F32),
        input_output_aliases={1: 0},
        compiler_params=_params(1),
        name="pool_sample",
    )(ext, x, w_pool, scale.reshape(1, d))


def _t5_bucket_table(tq):
    qi = np.arange(tq)[:, None]
    kj = np.arange(WINDOW + tq)[None, :]
    dist = qi + WINDOW - kj
    exact = T5_BUCKETS // 2
    ratio = np.log(np.maximum(dist, 1) / exact) / math.log(T5_MAX_DISTANCE / exact)
    large = np.minimum(exact + (ratio * (T5_BUCKETS - exact)).astype(np.int64), T5_BUCKETS - 1)
    bucket = np.where(dist < exact, dist, large)
    valid = (dist >= 0) & (dist < WINDOW)
    table = np.where(valid, bucket, -1).astype(np.int32)
    return table[:, :WINDOW], table[:, WINDOW:]


def _attn_kernel(rb_ref, sink_ref, bkt_p_ref, bkt_c_ref, q_ref, kc_ref, vc_ref, kp_ref, vp_ref,
                 o_ref, bias_p_ref, bias_c_ref, sink_col_ref, *, tq, blocks_per_seq):
    step = pl.program_id(0)

    @pl.when(step == 0)
    def _():
        bkt_p = bkt_p_ref[...]
        bkt_c = bkt_c_ref[...]
        bias_p_ref[...] = jnp.zeros_like(bias_p_ref)
        bias_c_ref[...] = jnp.zeros_like(bias_c_ref)

        def add_bucket(b, carry):
            eq_p = bkt_p == b
            eq_c = bkt_c == b
            for h in range(N_HEADS):
                kvh, g = divmod(h, GQA_GROUP)
                r = slice(g * tq, (g + 1) * tq)
                val = rb_ref[b, h]
                bias_p_ref[kvh, r, :] += jnp.where(eq_p, val, 0.0)
                bias_c_ref[kvh, r, :] += jnp.where(eq_c, val, 0.0)
            return carry

        lax.fori_loop(0, T5_BUCKETS, add_bucket, 0)
        for h in range(N_HEADS):
            kvh, g = divmod(h, GQA_GROUP)
            r = slice(g * tq, (g + 1) * tq)
            bias_p_ref[kvh, r, :] = jnp.where(bkt_p < 0, NEG_INF, bias_p_ref[kvh, r, :])
            bias_c_ref[kvh, r, :] = jnp.where(bkt_c < 0, NEG_INF, bias_c_ref[kvh, r, :])
            sink_col_ref[kvh, r, :] = jnp.full((tq, 1), sink_ref[h], F32)

    no_prev = (step % blocks_per_seq == 0) if blocks_per_seq else None

    def kv_head(kvh):
        heads = [kvh * GQA_GROUP + g for g in range(GQA_GROUP)]
        qs = jnp.concatenate([q_ref[:, h * HEAD_DIM:(h + 1) * HEAD_DIM] for h in heads], axis=0).astype(BF16)
        c = slice(kvh * HEAD_DIM, (kvh + 1) * HEAD_DIM)
        qk_p = _dot_nt(qs, kp_ref[:, c].astype(BF16))
        qk_c = _dot_nt(qs, kc_ref[:, c].astype(BF16))
        yield
        s_p = qk_p * ATT_SCALE + bias_p_ref[kvh]
        if no_prev is not None:
            s_p = jnp.where(no_prev, NEG_INF, s_p)
        s_c = qk_c * ATT_SCALE + bias_c_ref[kvh]
        sink = sink_col_ref[kvh]
        m = jnp.maximum(jnp.maximum(jnp.max(s_p, axis=-1, keepdims=True),
                                    jnp.max(s_c, axis=-1, keepdims=True)), sink)
        p_p = jnp.exp(s_p - m)
        p_c = jnp.exp(s_c - m)
        den = (jnp.sum(p_p, axis=-1, keepdims=True) + jnp.sum(p_c, axis=-1, keepdims=True)
               + jnp.exp(sink - m))
        o_p = _dot(p_p.astype(BF16), vp_ref[:, c].astype(BF16))
        o_c = _dot(p_c.astype(BF16), vc_ref[:, c].astype(BF16))
        yield
        o = (o_p + o_c) / den
        for g, h in enumerate(heads):
            o_ref[:, h * HEAD_DIM:(h + 1) * HEAD_DIM] = o[g * tq:(g + 1) * tq, :]

    _round_robin([kv_head(kvh) for kvh in range(N_KV_HEADS)])


def _attention(qkv, rel_bias, sinks, *, tq, n_blocks, row0, blocks_per_seq, k_prev=None, v_prev=None):
    bkt_p, bkt_c = _t5_bucket_table(tq)
    off = row0 // tq
    kcol = ATT_DIM // KV_DIM
    smem = pl.BlockSpec(memory_space=pltpu.SMEM)
    if k_prev is None:
        prev_rows = lambda i: jnp.maximum(i - 1, 0)
        kp_spec = pl.BlockSpec((WINDOW, KV_DIM), lambda i: (prev_rows(i), kcol))
        vp_spec = pl.BlockSpec((WINDOW, KV_DIM), lambda i: (prev_rows(i), kcol + 1))
        k_prev = v_prev = qkv
    else:
        kp_spec = vp_spec = pl.BlockSpec((None, WINDOW, KV_DIM), lambda i: (i, 0, 0))
    rows = GQA_GROUP * tq
    return pl.pallas_call(
        functools.partial(_attn_kernel, tq=tq, blocks_per_seq=blocks_per_seq),
        grid=(n_blocks,),
        in_specs=[smem, smem,
                  pl.BlockSpec((tq, WINDOW), lambda i: (0, 0)),
                  pl.BlockSpec((tq, tq), lambda i: (0, 0)),
                  pl.BlockSpec((tq, ATT_DIM), lambda i: (i + off, 0)),
                  pl.BlockSpec((tq, KV_DIM), lambda i: (i + off, kcol)),
                  pl.BlockSpec((tq, KV_DIM), lambda i: (i + off, kcol + 1)),
                  kp_spec, vp_spec],
        out_specs=pl.BlockSpec((tq, ATT_DIM), lambda i: (i, 0)),
        out_shape=jax.ShapeDtypeStruct((n_blocks * tq, ATT_DIM), F32),
        scratch_shapes=[pltpu.VMEM((N_KV_HEADS, rows, WINDOW), F32),
                        pltpu.VMEM((N_KV_HEADS, rows, tq), F32),
                        pltpu.VMEM((N_KV_HEADS, rows, 1), F32)],
        compiler_params=_params(1),
        name="swa_attention",
    )(rel_bias, sinks, jnp.asarray(bkt_p), jnp.asarray(bkt_c), qkv, qkv, qkv, k_prev, v_prev)


def _rwkv_proj_kernel(*refs, tm, seq_len, has_first):
    refs = list(refs)
    x_ref, halo_ref, gain_ref = refs[:3]
    pos = 3
    first_ref = None
    if has_first:
        first_ref = refs[pos]
        pos += 1
    (mu_ref, wr_ref, wk_ref, wv_ref, w1_ref, w2_ref, w0_ref, a1_ref, a2_ref, a0_ref, g1_ref, g2_ref,
     r_ref, k_ref, v_ref, ld_ref, a_ref, g_ref,
     ext_ref, xr_ref, xk_ref, xv_ref, hw_ref, ha_ref, hg_ref) = refs[pos:]

    @pl.when(pl.program_id(1) == 0)
    def _():
        gain = gain_ref[...]
        u = _rms_norm(x_ref[...], gain)
        ext_ref[0:SHIFT_HALO, :] = _rms_norm(halo_ref[...], gain)
        ext_ref[SHIFT_HALO:, :] = u
        prev = ext_ref[pl.ds(SHIFT_HALO - 1, tm), :]
        row = pl.program_id(0) * tm + lax.broadcasted_iota(jnp.int32, (tm, 1), 0)
        starts = row % seq_len == 0
        prev = jnp.where(starts, first_ref[...] if has_first else 0.0, prev)
        dx = prev - u
        mix = lambda i: (u + dx * mu_ref[i:i + 1, :]).astype(BF16)
        xr_ref[...] = mix(0)
        hw_ref[...] = jnp.tanh(_dot(mix(1), w1_ref[...])).astype(BF16)
        xk_ref[...] = mix(2)
        xv_ref[...] = mix(3)
        ha_ref[...] = _dot(mix(4), a1_ref[...]).astype(BF16)
        hg_ref[...] = jax.nn.sigmoid(_dot(mix(5), g1_ref[...])).astype(BF16)

    r_ref[...] = _dot(xr_ref[...], wr_ref[...])
    k_ref[...] = _dot(xk_ref[...], wk_ref[...])
    v_ref[...] = _dot(xv_ref[...], wv_ref[...])
    z = -(w0_ref[...] + _dot(hw_ref[...], w2_ref[...]))
    softplus = jnp.maximum(z, 0.0) + jnp.log1p(jnp.exp(-jnp.abs(z)))
    ld_ref[...] = -jnp.exp(-softplus - 0.5)
    a_ref[...] = jax.nn.sigmoid(a0_ref[...] + _dot(ha_ref[...], a2_ref[...]))
    g_ref[...] = _dot(hg_ref[...], g2_ref[...])


def _rwkv_proj(x, gain, weights, *, row0, rows, tm, seq_len, first=None):
    mu, w_r, w_k, w_v, w1, w2, w0, a1, a2, a0, g1, g2 = weights
    d = x.shape[1]
    tn = TN_PROJ
    gl = g1.shape[1]
    off = row0 // tm
    halo_blocks = tm // SHIFT_HALO
    halo_off = row0 // SHIFT_HALO
    row = lambda i, j: (i + off, 0)
    col = lambda i, j: (0, j)
    fixed = lambda i, j: (0, 0)
    in_specs = [pl.BlockSpec((tm, d), row),
                pl.BlockSpec((SHIFT_HALO, d), lambda i, j: (jnp.maximum(halo_off + i * halo_blocks - 1, 0), 0)),
                pl.BlockSpec((1, d), fixed)]
    args = [x, x, gain.reshape(1, d)]
    if first is not None:
        in_specs.append(pl.BlockSpec((tm, d), lambda i, j: (i, 0)))
        args.append(first)
    in_specs += [pl.BlockSpec((6, d), fixed),
                 pl.BlockSpec((d, tn), col), pl.BlockSpec((d, tn), col), pl.BlockSpec((d, tn), col),
                 pl.BlockSpec((d, LORA_PAD), fixed), pl.BlockSpec((LORA_PAD, tn), col), pl.BlockSpec((1, tn), col),
                 pl.BlockSpec((d, LORA_PAD), fixed), pl.BlockSpec((LORA_PAD, tn), col), pl.BlockSpec((1, tn), col),
                 pl.BlockSpec((d, gl), fixed), pl.BlockSpec((gl, tn), col)]
    args += [mu, w_r, w_k, w_v, w1, w2, w0.reshape(1, d), a1, a2, a0.reshape(1, d), g1, g2]
    out = jax.ShapeDtypeStruct((rows, d), F32)
    return pl.pallas_call(
        functools.partial(_rwkv_proj_kernel, tm=tm, seq_len=seq_len, has_first=first is not None),
        grid=(rows // tm, d // tn),
        in_specs=in_specs,
        out_specs=[pl.BlockSpec((tm, tn), lambda i, j: (i, j))] * 6,
        out_shape=[out] * 6,
        scratch_shapes=[pltpu.VMEM((tm + SHIFT_HALO, d), F32)] + [pltpu.VMEM((tm, d), BF16)] * 3
        + [pltpu.VMEM((tm, LORA_PAD), BF16)] * 2 + [pltpu.VMEM((tm, gl), BF16)],
        compiler_params=_params(2),
        name="rwkv_projections",
    )(*args)


def _scan_kernel(*refs, chunk, has_init):
    (r_ref, k_ref, v_ref, ld_ref, a_ref, g_ref, kk_ref, ka_ref, rk_ref, lnw_ref, lnb_ref) = refs[:11]
    refs = refs[11:]
    if has_init:
        s0_ref, refs = refs[0], refs[1:]
    z_ref, sout_ref, s_ref = refs
    c_idx = pl.program_id(1)
    cc, n, w = chunk, RWKV_N, SCAN_LANES
    hc = SCAN_HEADS * cc
    n_groups = RWKV_HEADS // SCAN_HEADS

    @pl.when(c_idx == 0)
    def _():
        s_ref[...] = jnp.zeros_like(s_ref)
        if has_init:
            for h in range(RWKV_HEADS):
                gi, hh = divmod(h, SCAN_HEADS)
                s_ref[gi, hh * n:(hh + 1) * n, hh * n:(hh + 1) * n] = s0_ref[h]

    def iota(shape, axis):
        return lax.broadcasted_iota(jnp.int32, shape, axis)

    head_diag = iota((w, w), 0) // n == iota((w, w), 1) // n
    ones_bd = head_diag.astype(BF16)
    tril = (iota((cc, cc), 0) >= iota((cc, cc), 1)).astype(BF16)
    t_row = iota((cc, hc), 0)
    s_col = iota((cc, hc), 1) % cc
    strict = t_row > s_col
    causal = t_row >= s_col
    lanes_diag = iota((hc, w), 0) // cc == iota((hc, w), 1) // n
    tokens_diag = iota((hc, hc), 0) // cc == iota((hc, hc), 1) // cc
    packed = cc % 16 == 0
    if packed:
        lanes_diag = lanes_diag.astype(BF16)
        tokens_diag = tokens_diag.astype(BF16)

    def block_diag(x, diag):
        if packed:
            return jnp.concatenate([x.astype(BF16)] * SCAN_HEADS, axis=0) * diag
        return jnp.where(diag, jnp.concatenate([x] * SCAN_HEADS, axis=0), 0.0).astype(BF16)

    def bd_lanes(x):
        return block_diag(x, lanes_diag)

    def bd_tokens(x):
        return block_diag(x, tokens_diag)

    def head_sums(*xs):
        pieces = []
        for x in xs:
            hi = x.astype(BF16)
            pieces += [hi, (x - hi.astype(F32)).astype(BF16)]
        both = _dot(jnp.concatenate(pieces, axis=0), ones_bd)
        yield
        return [both[2 * i * cc:(2 * i + 1) * cc] + both[(2 * i + 1) * cc:(2 * i + 2) * cc]
                for i in range(len(xs))]

    def group(gi):
        c = slice(gi * w, (gi + 1) * w)
        r, k, v, ld, a = r_ref[:, c], k_ref[:, c], v_ref[:, c], ld_ref[:, c], a_ref[:, c]

        p1 = ld.astype(BF16)
        r1 = ld - p1.astype(F32)
        p2 = r1.astype(BF16)
        p3 = (r1 - p2.astype(F32)).astype(BF16)
        cum = _dot(tril, jnp.concatenate([p1, p2, p3], axis=1))
        yield
        kk = k * kk_ref[:, c]
        kmod = k * (1.0 + (a - 1.0) * ka_ref[:, c])
        kk_sq, r_k = yield from head_sums(kk * kk, r * kmod * rk_ref[:, c])
        kap = kk / jnp.maximum(jnp.sqrt(kk_sq), 1e-12)
        b = kap * a

        lcum = cum[:, :w] + cum[:, w:2 * w] + cum[:, 2 * w:]
        lend = lcum[cc - 1:cc, :]
        e_inc = jnp.exp(lcum)
        e_exc = jnp.exp(lcum - ld)
        e_neg = jnp.exp(-lcum)
        e_end = jnp.exp(lend - lcum)

        kq = kap * e_exc
        rq = r * e_inc
        qr = jnp.concatenate([kq, rq], axis=0).astype(BF16)
        bk_bd = jnp.concatenate([bd_lanes(b * e_neg), bd_lanes(kmod * e_neg)], axis=0)
        gram = _dot_nt(qr, bk_bd)
        yield
        s_bd = s_ref[gi]
        qs = _dot_nt(qr, s_bd.astype(BF16))
        yield
        a_w = jnp.where(strict, gram[:cc, :hc], 0.0)
        bk_w = jnp.where(strict, gram[:cc, hc:], 0.0)
        cb_w = jnp.where(causal, gram[cc:, :hc], 0.0)
        ck_w = jnp.where(causal, gram[cc:, hc:], 0.0)
        v_bd = bd_lanes(v)

        bv = _dot(bk_w.astype(BF16), v_bd)
        yield
        x = -(qs[:cc] + bv)
        ax = _dot(a_w.astype(BF16), bd_lanes(x))
        yield
        x = x - ax
        a_pow = a_w
        power = 2
        while power < cc:
            a_pow = _dot(a_pow.astype(BF16), bd_tokens(a_pow))
            yield
            ax = _dot(a_pow.astype(BF16), bd_lanes(x))
            yield
            x = x + ax
            power *= 2
        u = x

        y_in = _dot(jnp.concatenate([cb_w, ck_w], axis=1).astype(BF16),
                    jnp.concatenate([bd_lanes(u), v_bd], axis=0))
        yield
        upd = _dot_tn(jnp.concatenate([u, v], axis=0).astype(BF16),
                      jnp.concatenate([b * e_end, kmod * e_end], axis=0).astype(BF16))
        yield
        s_ref[gi] = s_bd * jnp.exp(lend) + jnp.where(head_diag, upd, 0.0)

        y = qs[cc:] + y_in
        mean = (yield from head_sums(y))[0] * (1.0 / n)
        dev = y - mean
        var = (yield from head_sums(dev * dev))[0] * (1.0 / n)
        yn = dev * lax.rsqrt(var + GN_EPS) * lnw_ref[:, c] + lnb_ref[:, c]
        z_ref[:, c] = (yn + r_k * v) * g_ref[:, c]

    _round_robin([group(gi) for gi in range(n_groups)])

    @pl.when(c_idx == pl.num_programs(1) - 1)
    def _():
        for h in range(RWKV_HEADS):
            gi, hh = divmod(h, SCAN_HEADS)
            sout_ref[h] = s_ref[gi, hh * n:(hh + 1) * n, hh * n:(hh + 1) * n]


def _rwkv_scan(proj, k_k, k_a, r_k, ln_w, ln_b, *, n_seq, chunk, n_chunks, s0=None):
    d = D_MODEL
    tok = pl.BlockSpec((chunk, d), lambda b, c: (b * n_chunks + c, 0))
    par = pl.BlockSpec((1, d), lambda b, c: (0, 0))
    state = pl.BlockSpec((None, RWKV_HEADS, RWKV_N, RWKV_N), lambda b, c: (b, 0, 0, 0))
    in_specs = [tok] * 6 + [par] * 5
    args = list(proj) + [p.reshape(1, d) for p in (k_k, k_a, r_k, ln_w, ln_b)]
    if s0 is not None:
        in_specs.append(state)
        args.append(s0)
    n_groups = RWKV_HEADS // SCAN_HEADS
    return pl.pallas_call(
        functools.partial(_scan_kernel, chunk=chunk, has_init=s0 is not None),
        grid=(n_seq, n_chunks),
        in_specs=in_specs,
        out_specs=[tok, state],
        out_shape=[jax.ShapeDtypeStruct((n_seq * n_chunks * chunk, d), F32),
                   jax.ShapeDtypeStruct((n_seq, RWKV_HEADS, RWKV_N, RWKV_N), F32)],
        scratch_shapes=[pltpu.VMEM((n_groups, SCAN_LANES, SCAN_LANES), F32)],
        compiler_params=_params(2),
        name="rwkv_scan",
    )(*args)


def _pad_cols(w):
    return jnp.pad(w, ((0, 0), (0, LORA_PAD - w.shape[1])))


def _pad_rows(w):
    return jnp.pad(w, ((0, LORA_PAD - w.shape[0]), (0, 0)))


def kernel(x_prompt, x_sample, state_pool, cache_win_k, cache_win_v, state_shift, state_wkv, norm_ffn1, norm_mix, norm_ffn2, norm_final, ffn_w_gate, ffn_w_up, ffn_w_down, pool_w, pool_scale, att_w_qkv, att_b_qkv, att_w_o, att_b_o, att_sinks, rel_bias, rwkv_mu, rwkv_w_r, rwkv_w_k, rwkv_w_v, rwkv_w_o, rwkv_w0, rwkv_w1, rwkv_w2, rwkv_a0, rwkv_a1, rwkv_a2, rwkv_g1, rwkv_g2, rwkv_k_k, rwkv_k_a, rwkv_r_k, rwkv_ln_w, rwkv_ln_b):
    d = D_MODEL
    x = jnp.concatenate([x_prompt.reshape(N_PROMPT, d), x_sample.reshape(N_SAMPLE, d)], axis=0)
    wg, wu, wd = (w.astype(BF16) for w in (ffn_w_gate, ffn_w_up, ffn_w_down))
    pool_p, pool_s, wk_p, wv_p, wk_s, wv_s, sh_p, sh_s, wkv_p, wkv_s = ([] for _ in range(10))

    for l in range(DEPTH):
        j, kind = divmod(l, N_MIXERS)
        x = _ffn(x, norm_ffn1[l], wg, wu, wd, l, 0)
        gain = norm_mix[l]
        if kind == 0:
            w_pool = pool_w[j].astype(BF16)
            u_s = _norm_rows(x, gain, tm=N_SAMPLE, row0=N_PROMPT, rows=N_SAMPLE).reshape(DEC_BATCH, DEC_SEQ, d)
            ext = jnp.concatenate([jnp.zeros((DEC_BATCH, 1, d), F32), state_pool[j], u_s], axis=1)
            pool_p.append(_norm_tails(x, gain)[:, -POOL_BUF:])
            pool_s.append(ext[:, -POOL_BUF:])
            x = _pool_prompt(x, gain, w_pool, pool_scale[j])
            x = _pool_sample(ext, x, w_pool, pool_scale[j])
        elif kind == 1:
            qkv = _matmul(x, att_w_qkv[j].astype(BF16), att_b_qkv[j], tm=TM_TOK, tn=QKV_DIM // 5, gain=gain)
            k_buf = cache_win_k[j].reshape(DEC_BATCH, WINDOW, KV_DIM)
            v_buf = cache_win_v[j].reshape(DEC_BATCH, WINDOW, KV_DIM)
            o_p = _attention(qkv, rel_bias, att_sinks[j], tq=WINDOW, n_blocks=N_PROMPT // WINDOW,
                             row0=0, blocks_per_seq=SEQ // WINDOW)
            o_s = _attention(qkv, rel_bias, att_sinks[j], tq=DEC_SEQ, n_blocks=DEC_BATCH,
                             row0=N_PROMPT, blocks_per_seq=0, k_prev=k_buf, v_prev=v_buf)
            w_o = att_w_o[j].astype(BF16)
            x = _matmul(o_p, w_o, att_b_o[j], x, tm=TM_PROMPT, tn=d, res_row0=0)
            x = _matmul(o_s, w_o, att_b_o[j], x, tm=N_SAMPLE, tn=d, res_row0=N_PROMPT)
            kv_shape = (WINDOW, N_KV_HEADS, HEAD_DIM)
            k_cols = slice(ATT_DIM, ATT_DIM + KV_DIM)
            v_cols = slice(ATT_DIM + KV_DIM, QKV_DIM)
            tails = [slice((b + 1) * SEQ - WINDOW, (b + 1) * SEQ) for b in range(BATCH)]
            wk_p.append(jnp.stack([qkv[t, k_cols] for t in tails]).reshape(BATCH, *kv_shape))
            wv_p.append(jnp.stack([qkv[t, v_cols] for t in tails]).reshape(BATCH, *kv_shape))
            new_rows = slice(N_PROMPT, N_TOK)
            k_s = jnp.concatenate([k_buf, qkv[new_rows, k_cols].reshape(DEC_BATCH, DEC_SEQ, KV_DIM)], axis=1)
            v_s = jnp.concatenate([v_buf, qkv[new_rows, v_cols].reshape(DEC_BATCH, DEC_SEQ, KV_DIM)], axis=1)
            wk_s.append(k_s[:, -WINDOW:].reshape(DEC_BATCH, *kv_shape))
            wv_s.append(v_s[:, -WINDOW:].reshape(DEC_BATCH, *kv_shape))
        else:
            weights = (rwkv_mu[j],
                       rwkv_w_r[j].astype(BF16), rwkv_w_k[j].astype(BF16), rwkv_w_v[j].astype(BF16),
                       _pad_cols(rwkv_w1[j]).astype(BF16), _pad_rows(rwkv_w2[j]).astype(BF16), rwkv_w0[j],
                       _pad_cols(rwkv_a1[j]).astype(BF16), _pad_rows(rwkv_a2[j]).astype(BF16), rwkv_a0[j],
                       rwkv_g1[j].astype(BF16), rwkv_g2[j].astype(BF16))
            first_s = jnp.repeat(state_shift[j], DEC_SEQ, axis=0)
            proj_p = _rwkv_proj(x, gain, weights, row0=0, rows=N_PROMPT, tm=TM_PROMPT, seq_len=SEQ)
            proj_s = _rwkv_proj(x, gain, weights, row0=N_PROMPT, rows=N_SAMPLE, tm=N_SAMPLE, seq_len=DEC_SEQ,
                                first=first_s)
            head_params = (rwkv_k_k[j], rwkv_k_a[j], rwkv_r_k[j], rwkv_ln_w[j], rwkv_ln_b[j])
            z_p, s_p = _rwkv_scan(proj_p, *head_params, n_seq=BATCH, chunk=SCAN_CHUNK, n_chunks=SEQ // SCAN_CHUNK)
            z_s, s_s = _rwkv_scan(proj_s, *head_params, n_seq=DEC_BATCH, chunk=DEC_SEQ, n_chunks=1,
                                  s0=state_wkv[j])
            sh_p.append(_norm_tails(x, gain)[:, -1])
            sh_s.append(_norm_rows(x, gain, tm=N_SAMPLE, row0=N_PROMPT, rows=N_SAMPLE)
                        .reshape(DEC_BATCH, DEC_SEQ, d)[:, -1])
            w_o = rwkv_w_o[j].astype(BF16)
            x = _matmul(z_p, w_o, None, x, tm=TM_PROMPT, tn=d, res_row0=0)
            x = _matmul(z_s, w_o, None, x, tm=N_SAMPLE, tn=d, res_row0=N_PROMPT)
            wkv_p.append(s_p)
            wkv_s.append(s_s)
        x = _ffn(x, norm_ffn2[l], wg, wu, wd, l, 1)

    y_p = _norm_rows(x, norm_final, tm=TM_PROMPT, row0=0, rows=N_PROMPT)
    y_s = _norm_rows(x, norm_final, tm=N_SAMPLE, row0=N_PROMPT, rows=N_SAMPLE)
    return (y_p.reshape(BATCH, SEQ, d), y_s.reshape(DEC_BATCH, DEC_SEQ, d),
            jnp.stack(pool_p), jnp.stack(pool_s),
            jnp.stack(wk_p), jnp.stack(wv_p), jnp.stack(wk_s), jnp.stack(wv_s),
            jnp.stack(sh_p), jnp.stack(sh_s),
            jnp.stack(wkv_p), jnp.stack(wkv_s))
```

```python
import functools
import math

import numpy as np
import jax
import jax.numpy as jnp
from jax import lax
from jax.experimental import pallas as pl
from jax.experimental.pallas import tpu as pltpu

F32 = jnp.float32
BF16 = jnp.bfloat16

D_MODEL = 2048
BATCH = 2
SEQ = 4096
DEPTH = 4
DEC_BATCH = 32
DEC_SEQ = 8
PAST_LEN = 16384
N_MIXERS = 3
RMS_EPS = 1e-6
D_FF = 5632
POOL_WINDOWS = (2, 4, 8, 16)
POOL_GROUPS = 4
POOL_GROUP_DIM = D_MODEL // POOL_GROUPS
POOL_BUF = max(POOL_WINDOWS) - 1
POOL_HALO = POOL_BUF + 1
HEAD_DIM = 64
N_HEADS = D_MODEL // HEAD_DIM
N_KV_HEADS = 4
GQA_GROUP = N_HEADS // N_KV_HEADS
ATT_DIM = N_HEADS * HEAD_DIM
KV_DIM = N_KV_HEADS * HEAD_DIM
QKV_DIM = ATT_DIM + 2 * KV_DIM
WINDOW = 128
ATT_SCALE = HEAD_DIM ** -0.5
T5_BUCKETS = 32
T5_MAX_DISTANCE = 128
NEG_INF = -1e30
RWKV_N = 64
RWKV_HEADS = D_MODEL // RWKV_N
GN_EPS = 64e-5
LORA_PAD = 128

N_PROMPT = BATCH * SEQ
N_SAMPLE = DEC_BATCH * DEC_SEQ
N_TOK = N_PROMPT + N_SAMPLE

VMEM_LIMIT_BYTES = 56 * 1024 * 1024

TM_TOK = 704
TM_PROMPT = 512
TM_POOL = 256
SHIFT_HALO = 8
TM_FFN = 1056
TF_FFN = 256
TN_PROJ = 512
SCAN_CHUNK = 64
SCAN_HEADS = 4
SCAN_LANES = SCAN_HEADS * RWKV_N


def _params(n_axes):
    return pltpu.CompilerParams(dimension_semantics=("arbitrary",) * n_axes,
                                vmem_limit_bytes=VMEM_LIMIT_BYTES)


def _dot(a, b):
    return jnp.dot(a, b, preferred_element_type=F32)


def _dot_nt(a, b):
    return lax.dot_general(a, b, (((1,), (1,)), ((), ())), preferred_element_type=F32)


def _dot_tn(a, b):
    return lax.dot_general(a, b, (((0,), (0,)), ((), ())), preferred_element_type=F32)


def _round_robin(gens):
    live = list(gens)
    while live:
        still = []
        for gen in live:
            try:
                next(gen)
                still.append(gen)
            except StopIteration:
                pass
        live = still


def _rms_norm(x, g):
    return x * lax.rsqrt(jnp.mean(x * x, axis=-1, keepdims=True) + RMS_EPS) * g


def _norm_kernel(x_ref, g_ref, o_ref):
    o_ref[...] = _rms_norm(x_ref[...], g_ref[...])


def _norm_rows(x, g, *, tm, row0, rows):
    d = x.shape[1]
    off = row0 // tm
    return pl.pallas_call(
        _norm_kernel,
        grid=(rows // tm,),
        in_specs=[pl.BlockSpec((tm, d), lambda i: (i + off, 0)),
                  pl.BlockSpec((1, d), lambda i: (0, 0))],
        out_specs=pl.BlockSpec((tm, d), lambda i: (i, 0)),
        out_shape=jax.ShapeDtypeStruct((rows, d), F32),
        compiler_params=_params(1),
        name="rms_norm",
    )(x, g.reshape(1, d))


def _norm_tails(x, g):
    d = x.shape[1]
    blocks_per_seq = SEQ // POOL_HALO
    out = pl.pallas_call(
        _norm_kernel,
        grid=(BATCH,),
        in_specs=[pl.BlockSpec((POOL_HALO, d), lambda b: ((b + 1) * blocks_per_seq - 1, 0)),
                  pl.BlockSpec((1, d), lambda b: (0, 0))],
        out_specs=pl.BlockSpec((POOL_HALO, d), lambda b: (b, 0)),
        out_shape=jax.ShapeDtypeStruct((BATCH * POOL_HALO, d), F32),
        compiler_params=_params(1),
        name="rms_norm_tails",
    )(x, g.reshape(1, d))
    return out.reshape(BATCH, POOL_HALO, d)


def _ffn_kernel(x_ref, g_ref, wg_ref, wu_ref, wd_ref, o_ref, xn_ref):
    j = pl.program_id(1)

    @pl.when(j == 0)
    def _():
        x = x_ref[...]
        xn_ref[...] = _rms_norm(x, g_ref[...]).astype(BF16)
        o_ref[...] = 2.0 * x

    xn = xn_ref[...]
    gate = _dot(xn, wg_ref[...].astype(BF16))
    up = _dot(xn, wu_ref[...].astype(BF16))
    h = (gate * jax.nn.sigmoid(gate) * up).astype(BF16)
    o_ref[...] += _dot(h, wd_ref[...].astype(BF16))

    @pl.when(j == pl.num_programs(1) - 1)
    def _():
        o_ref[...] = 0.5 * o_ref[...]


def _ffn(x, g, wg, wu, wd, layer, half):
    m, d = x.shape
    tm, tf = TM_FFN, TF_FFN
    w_in = pl.BlockSpec((None, None, d, tf), lambda i, j: (layer, half, 0, j))
    return pl.pallas_call(
        _ffn_kernel,
        grid=(m // tm, D_FF // tf),
        in_specs=[pl.BlockSpec((tm, d), lambda i, j: (i, 0), pipeline_mode=pl.Buffered(1)),
                  pl.BlockSpec((1, d), lambda i, j: (0, 0)),
                  w_in, w_in,
                  pl.BlockSpec((None, None, tf, d), lambda i, j: (layer, half, j, 0))],
        out_specs=pl.BlockSpec((tm, d), lambda i, j: (i, 0)),
        out_shape=jax.ShapeDtypeStruct((m, d), F32),
        scratch_shapes=[pltpu.VMEM((tm, d), BF16)],
        compiler_params=_params(2),
        name="ffn_half_step",
    )(x, g.reshape(1, d), wg, wu, wd)


def _matmul_kernel(*refs, has_gain, has_bias, has_res):
    refs = list(refs)
    lhs_ref, w_ref = refs[0], refs[1]
    pos = 2
    g_ref = b_ref = res_ref = None
    if has_gain:
        g_ref = refs[pos]
        pos += 1
    if has_bias:
        b_ref = refs[pos]
        pos += 1
    if has_res:
        res_ref = refs[pos]
        pos += 1
    o_ref, lhs_bf_ref = refs[pos], refs[pos + 1]

    @pl.when(pl.program_id(1) == 0)
    def _():
        lhs = lhs_ref[...]
        if has_gain:
            lhs = _rms_norm(lhs, g_ref[...])
        lhs_bf_ref[...] = lhs.astype(BF16)

    acc = _dot(lhs_bf_ref[...], w_ref[...])
    if has_bias:
        acc = acc + b_ref[...]
    if has_res:
        acc = res_ref[...] + acc
    o_ref[...] = acc


def _matmul(lhs, w, bias=None, res=None, *, tm, tn, res_row0=0, gain=None):
    m, k = lhs.shape
    n = w.shape[1]
    off = res_row0 // tm
    in_specs = [pl.BlockSpec((tm, k), lambda i, j: (i, 0)),
                pl.BlockSpec((k, tn), lambda i, j: (0, j))]
    args = [lhs, w]
    if gain is not None:
        in_specs.append(pl.BlockSpec((1, k), lambda i, j: (0, 0)))
        args.append(gain.reshape(1, k))
    if bias is not None:
        in_specs.append(pl.BlockSpec((1, tn), lambda i, j: (0, j)))
        args.append(bias.reshape(1, n))
    aliases = {}
    if res is not None:
        in_specs.append(pl.BlockSpec((tm, tn), lambda i, j: (i + off, j)))
        aliases = {len(args): 0}
        args.append(res)
        out_shape = jax.ShapeDtypeStruct(res.shape, F32)
    else:
        out_shape = jax.ShapeDtypeStruct((m, n), F32)
    return pl.pallas_call(
        functools.partial(_matmul_kernel, has_gain=gain is not None, has_bias=bias is not None,
                          has_res=res is not None),
        grid=(m // tm, n // tn),
        in_specs=in_specs,
        out_specs=pl.BlockSpec((tm, tn), lambda i, j: (i + off, j)),
        out_shape=out_shape,
        scratch_shapes=[pltpu.VMEM((tm, k), BF16)],
        input_output_aliases=aliases,
        compiler_params=_params(2),
        name="matmul_bias_residual",
    )(*args)


def _pool_group_out(diff, gi, x_ref, w_ref, sc_ref, o_ref):
    c = slice(gi * POOL_GROUP_DIM, (gi + 1) * POOL_GROUP_DIM)
    out = _dot(diff.astype(BF16), w_ref[gi])
    o_ref[:, c] = x_ref[:, c] + out * sc_ref[:, c]


def _pool_prompt_kernel(x_ref, halo_ref, g_ref, w_ref, sc_ref, o_ref, ext_ref, *, tm, tiles_per_seq, n_tiles):
    i = pl.program_id(0)

    @pl.when(i < n_tiles)
    def _():
        t_in_seq = i % tiles_per_seq
        g = g_ref[...]
        ext_ref[0:POOL_HALO, :] = jnp.where(t_in_seq == 0, 0.0, _rms_norm(halo_ref[...], g))
        ext_ref[POOL_HALO:, :] = _rms_norm(x_ref[...], g)
        pos = t_in_seq * tm + lax.broadcasted_iota(jnp.int32, (tm, 1), 0)
        for gi, w in enumerate(POOL_WINDOWS):
            c = slice(gi * POOL_GROUP_DIM, (gi + 1) * POOL_GROUP_DIM)
            cur = ext_ref[pl.ds(POOL_HALO, tm), c]
            s = cur
            for back in range(1, w):
                s = s + ext_ref[pl.ds(POOL_HALO - back, tm), c]
            cnt = jnp.minimum(w, pos + 1).astype(F32)
            _pool_group_out(s / cnt - cur, gi, x_ref, w_ref, sc_ref, o_ref)

    @pl.when(i >= n_tiles)
    def _():
        o_ref[...] = x_ref[...]


def _pool_prompt(x, gain, w_pool, scale):
    d = x.shape[1]
    tm = TM_POOL
    halo_blocks = tm // POOL_HALO
    return pl.pallas_call(
        functools.partial(_pool_prompt_kernel, tm=tm, tiles_per_seq=SEQ // tm, n_tiles=N_PROMPT // tm),
        grid=(N_TOK // tm,),
        in_specs=[pl.BlockSpec((tm, d), lambda i: (i, 0)),
                  pl.BlockSpec((POOL_HALO, d), lambda i: (jnp.maximum(i * halo_blocks - 1, 0), 0)),
                  pl.BlockSpec((1, d), lambda i: (0, 0)),
                  pl.BlockSpec((POOL_GROUPS, POOL_GROUP_DIM, POOL_GROUP_DIM), lambda i: (0, 0, 0)),
                  pl.BlockSpec((1, d), lambda i: (0, 0))],
        out_specs=pl.BlockSpec((tm, d), lambda i: (i, 0)),
        out_shape=jax.ShapeDtypeStruct(x.shape, F32),
        scratch_shapes=[pltpu.VMEM((tm + POOL_HALO, d), F32)],
        compiler_params=_params(1),
        name="pool_prompt",
    )(x, x, gain.reshape(1, d), w_pool, scale.reshape(1, d))


def _pool_sample_kernel(ext_ref, x_ref, w_ref, sc_ref, o_ref):
    for gi, w in enumerate(POOL_WINDOWS):
        c = slice(gi * POOL_GROUP_DIM, (gi + 1) * POOL_GROUP_DIM)
        cur = ext_ref[:, pl.ds(POOL_HALO, DEC_SEQ), c]
        s = cur
        for back in range(1, w):
            s = s + ext_ref[:, pl.ds(POOL_HALO - back, DEC_SEQ), c]
        cnt = min(w, PAST_LEN + 1)
        diff = (s / float(cnt) - cur).reshape(N_SAMPLE, POOL_GROUP_DIM)
        _pool_group_out(diff, gi, x_ref, w_ref, sc_ref, o_ref)


def _pool_sample(ext, x, w_pool, scale):
    d = x.shape[1]
    blk = N_PROMPT // N_SAMPLE
    return pl.pallas_call(
        _pool_sample_kernel,
        grid=(1,),
        in_specs=[pl.BlockSpec(ext.shape, lambda i: (0, 0, 0)),
                  pl.BlockSpec((N_SAMPLE, d), lambda i: (blk, 0)),
                  pl.BlockSpec((POOL_GROUPS, POOL_GROUP_DIM, POOL_GROUP_DIM), lambda i: (0, 0, 0)),
                  pl.BlockSpec((1, d), lambda i: (0, 0))],
        out_specs=pl.BlockSpec((N_SAMPLE, d), lambda i: (blk, 0)),
        out_shape=jax.ShapeDtypeStruct(x.shape, F32),
        input_output_aliases={1: 0},
        compiler_params=_params(1),
        name="pool_sample",
    )(ext, x, w_pool, scale.reshape(1, d))


def _t5_bucket_table(tq):
    qi = np.arange(tq)[:, None]
    kj = np.arange(WINDOW + tq)[None, :]
    dist = qi + WINDOW - kj
    exact = T5_BUCKETS // 2
    ratio = np.log(np.maximum(dist, 1) / exact) / math.log(T5_MAX_DISTANCE / exact)
    large = np.minimum(exact + (ratio * (T5_BUCKETS - exact)).astype(np.int64), T5_BUCKETS - 1)
    bucket = np.where(dist < exact, dist, large)
    valid = (dist >= 0) & (dist < WINDOW)
    table = np.where(valid, bucket, -1).astype(np.int32)
    return table[:, :WINDOW], table[:, WINDOW:]


def _attn_kernel(rb_ref, sink_ref, bkt_p_ref, bkt_c_ref, q_ref, kc_ref, vc_ref, kp_ref, vp_ref,
                 o_ref, bias_p_ref, bias_c_ref, sink_col_ref, *, tq, blocks_per_seq):
    step = pl.program_id(0)

    @pl.when(step == 0)
    def _():
        bkt_p = bkt_p_ref[...]
        bkt_c = bkt_c_ref[...]
        bias_p_ref[...] = jnp.zeros_like(bias_p_ref)
        bias_c_ref[...] = jnp.zeros_like(bias_c_ref)

        def add_bucket(b, carry):
            eq_p = bkt_p == b
            eq_c = bkt_c == b
            for h in range(N_HEADS):
                kvh, g = divmod(h, GQA_GROUP)
                r = slice(g * tq, (g + 1) * tq)
                val = rb_ref[b, h]
                bias_p_ref[kvh, r, :] += jnp.where(eq_p, val, 0.0)
                bias_c_ref[kvh, r, :] += jnp.where(eq_c, val, 0.0)
            return carry

        lax.fori_loop(0, T5_BUCKETS, add_bucket, 0)
        for h in range(N_HEADS):
            kvh, g = divmod(h, GQA_GROUP)
            r = slice(g * tq, (g + 1) * tq)
            bias_p_ref[kvh, r, :] = jnp.where(bkt_p < 0, NEG_INF, bias_p_ref[kvh, r, :])
            bias_c_ref[kvh, r, :] = jnp.where(bkt_c < 0, NEG_INF, bias_c_ref[kvh, r, :])
            sink_col_ref[kvh, r, :] = jnp.full((tq, 1), sink_ref[h], F32)

    no_prev = (step % blocks_per_seq == 0) if blocks_per_seq else None

    def kv_head(kvh):
        heads = [kvh * GQA_GROUP + g for g in range(GQA_GROUP)]
        qs = jnp.concatenate([q_ref[:, h * HEAD_DIM:(h + 1) * HEAD_DIM] for h in heads], axis=0).astype(BF16)
        c = slice(kvh * HEAD_DIM, (kvh + 1) * HEAD_DIM)
        qk_p = _dot_nt(qs, kp_ref[:, c].astype(BF16))
        qk_c = _dot_nt(qs, kc_ref[:, c].astype(BF16))
        yield
        s_p = qk_p * ATT_SCALE + bias_p_ref[kvh]
        if no_prev is not None:
            s_p = jnp.where(no_prev, NEG_INF, s_p)
        s_c = qk_c * ATT_SCALE + bias_c_ref[kvh]
        sink = sink_col_ref[kvh]
        m = jnp.maximum(jnp.maximum(jnp.max(s_p, axis=-1, keepdims=True),
                                    jnp.max(s_c, axis=-1, keepdims=True)), sink)
        p_p = jnp.exp(s_p - m)
        p_c = jnp.exp(s_c - m)
        den = (jnp.sum(p_p, axis=-1, keepdims=True) + jnp.sum(p_c, axis=-1, keepdims=True)
               + jnp.exp(sink - m))
        o_p = _dot(p_p.astype(BF16), vp_ref[:, c].astype(BF16))
        o_c = _dot(p_c.astype(BF16), vc_ref[:, c].astype(BF16))
        yield
        o = (o_p + o_c) / den
        for g, h in enumerate(heads):
            o_ref[:, h * HEAD_DIM:(h + 1) * HEAD_DIM] = o[g * tq:(g + 1) * tq, :]

    _round_robin([kv_head(kvh) for kvh in range(N_KV_HEADS)])


def _attention(qkv, rel_bias, sinks, *, tq, n_blocks, row0, blocks_per_seq, k_prev=None, v_prev=None):
    bkt_p, bkt_c = _t5_bucket_table(tq)
    off = row0 // tq
    kcol = ATT_DIM // KV_DIM
    smem = pl.BlockSpec(memory_space=pltpu.SMEM)
    if k_prev is None:
        prev_rows = lambda i: jnp.maximum(i - 1, 0)
        kp_spec = pl.BlockSpec((WINDOW, KV_DIM), lambda i: (prev_rows(i), kcol))
        vp_spec = pl.BlockSpec((WINDOW, KV_DIM), lambda i: (prev_rows(i), kcol + 1))
        k_prev = v_prev = qkv
    else:
        kp_spec = vp_spec = pl.BlockSpec((None, WINDOW, KV_DIM), lambda i: (i, 0, 0))
    rows = GQA_GROUP * tq
    return pl.pallas_call(
        functools.partial(_attn_kernel, tq=tq, blocks_per_seq=blocks_per_seq),
        grid=(n_blocks,),
        in_specs=[smem, smem,
                  pl.BlockSpec((tq, WINDOW), lambda i: (0, 0)),
                  pl.BlockSpec((tq, tq), lambda i: (0, 0)),
                  pl.BlockSpec((tq, ATT_DIM), lambda i: (i + off, 0)),
                  pl.BlockSpec((tq, KV_DIM), lambda i: (i + off, kcol)),
                  pl.BlockSpec((tq, KV_DIM), lambda i: (i + off, kcol + 1)),
                  kp_spec, vp_spec],
        out_specs=pl.BlockSpec((tq, ATT_DIM), lambda i: (i, 0)),
        out_shape=jax.ShapeDtypeStruct((n_blocks * tq, ATT_DIM), F32),
        scratch_shapes=[pltpu.VMEM((N_KV_HEADS, rows, WINDOW), F32),
                        pltpu.VMEM((N_KV_HEADS, rows, tq), F32),
                        pltpu.VMEM((N_KV_HEADS, rows, 1), F32)],
        compiler_params=_params(1),
        name="swa_attention",
    )(rel_bias, sinks, jnp.asarray(bkt_p), jnp.asarray(bkt_c), qkv, qkv, qkv, k_prev, v_prev)


def _rwkv_proj_kernel(*refs, tm, seq_len, has_first):
    refs = list(refs)
    x_ref, halo_ref, gain_ref = refs[:3]
    pos = 3
    first_ref = None
    if has_first:
        first_ref = refs[pos]
        pos += 1
    (mu_ref, wr_ref, wk_ref, wv_ref, w1_ref, w2_ref, w0_ref, a1_ref, a2_ref, a0_ref, g1_ref, g2_ref,
     r_ref, k_ref, v_ref, ld_ref, a_ref, g_ref,
     ext_ref, xr_ref, xk_ref, xv_ref, hw_ref, ha_ref, hg_ref) = refs[pos:]

    @pl.when(pl.program_id(1) == 0)
    def _():
        gain = gain_ref[...]
        u = _rms_norm(x_ref[...], gain)
        ext_ref[0:SHIFT_HALO, :] = _rms_norm(halo_ref[...], gain)
        ext_ref[SHIFT_HALO:, :] = u
        prev = ext_ref[pl.ds(SHIFT_HALO - 1, tm), :]
        row = pl.program_id(0) * tm + lax.broadcasted_iota(jnp.int32, (tm, 1), 0)
        starts = row % seq_len == 0
        prev = jnp.where(starts, first_ref[...] if has_first else 0.0, prev)
        dx = prev - u
        mix = lambda i: (u + dx * mu_ref[i:i + 1, :]).astype(BF16)
        xr_ref[...] = mix(0)
        hw_ref[...] = jnp.tanh(_dot(mix(1), w1_ref[...])).astype(BF16)
        xk_ref[...] = mix(2)
        xv_ref[...] = mix(3)
        ha_ref[...] = _dot(mix(4), a1_ref[...]).astype(BF16)
        hg_ref[...] = jax.nn.sigmoid(_dot(mix(5), g1_ref[...])).astype(BF16)

    r_ref[...] = _dot(xr_ref[...], wr_ref[...])
    k_ref[...] = _dot(xk_ref[...], wk_ref[...])
    v_ref[...] = _dot(xv_ref[...], wv_ref[...])
    z = -(w0_ref[...] + _dot(hw_ref[...], w2_ref[...]))
    softplus = jnp.maximum(z, 0.0) + jnp.log1p(jnp.exp(-jnp.abs(z)))
    ld_ref[...] = -jnp.exp(-softplus - 0.5)
    a_ref[...] = jax.nn.sigmoid(a0_ref[...] + _dot(ha_ref[...], a2_ref[...]))
    g_ref[...] = _dot(hg_ref[...], g2_ref[...])


def _rwkv_proj(x, gain, weights, *, row0, rows, tm, seq_len, first=None):
    mu, w_r, w_k, w_v, w1, w2, w0, a1, a2, a0, g1, g2 = weights
    d = x.shape[1]
    tn = TN_PROJ
    gl = g1.shape[1]
    off = row0 // tm
    halo_blocks = tm // SHIFT_HALO
    halo_off = row0 // SHIFT_HALO
    row = lambda i, j: (i + off, 0)
    col = lambda i, j: (0, j)
    fixed = lambda i, j: (0, 0)
    in_specs = [pl.BlockSpec((tm, d), row),
                pl.BlockSpec((SHIFT_HALO, d), lambda i, j: (jnp.maximum(halo_off + i * halo_blocks - 1, 0), 0)),
                pl.BlockSpec((1, d), fixed)]
    args = [x, x, gain.reshape(1, d)]
    if first is not None:
        in_specs.append(pl.BlockSpec((tm, d), lambda i, j: (i, 0)))
        args.append(first)
    in_specs += [pl.BlockSpec((6, d), fixed),
                 pl.BlockSpec((d, tn), col), pl.BlockSpec((d, tn), col), pl.BlockSpec((d, tn), col),
                 pl.BlockSpec((d, LORA_PAD), fixed), pl.BlockSpec((LORA_PAD, tn), col), pl.BlockSpec((1, tn), col),
                 pl.BlockSpec((d, LORA_PAD), fixed), pl.BlockSpec((LORA_PAD, tn), col), pl.BlockSpec((1, tn), col),
                 pl.BlockSpec((d, gl), fixed), pl.BlockSpec((gl, tn), col)]
    args += [mu, w_r, w_k, w_v, w1, w2, w0.reshape(1, d), a1, a2, a0.reshape(1, d), g1, g2]
    out = jax.ShapeDtypeStruct((rows, d), F32)
    return pl.pallas_call(
        functools.partial(_rwkv_proj_kernel, tm=tm, seq_len=seq_len, has_first=first is not None),
        grid=(rows // tm, d // tn),
        in_specs=in_specs,
        out_specs=[pl.BlockSpec((tm, tn), lambda i, j: (i, j))] * 6,
        out_shape=[out] * 6,
        scratch_shapes=[pltpu.VMEM((tm + SHIFT_HALO, d), F32)] + [pltpu.VMEM((tm, d), BF16)] * 3
        + [pltpu.VMEM((tm, LORA_PAD), BF16)] * 2 + [pltpu.VMEM((tm, gl), BF16)],
        compiler_params=_params(2),
        name="rwkv_projections",
    )(*args)


def _scan_kernel(*refs, chunk, has_init):
    (r_ref, k_ref, v_ref, ld_ref, a_ref, g_ref, kk_ref, ka_ref, rk_ref, lnw_ref, lnb_ref) = refs[:11]
    refs = refs[11:]
    if has_init:
        s0_ref, refs = refs[0], refs[1:]
    z_ref, sout_ref, s_ref = refs
    c_idx = pl.program_id(1)
    cc, n, w = chunk, RWKV_N, SCAN_LANES
    hc = SCAN_HEADS * cc
    n_groups = RWKV_HEADS // SCAN_HEADS

    @pl.when(c_idx == 0)
    def _():
        s_ref[...] = jnp.zeros_like(s_ref)
        if has_init:
            for h in range(RWKV_HEADS):
                gi, hh = divmod(h, SCAN_HEADS)
                s_ref[gi, hh * n:(hh + 1) * n, hh * n:(hh + 1) * n] = s0_ref[h]

    def iota(shape, axis):
        return lax.broadcasted_iota(jnp.int32, shape, axis)

    head_diag = iota((w, w), 0) // n == iota((w, w), 1) // n
    ones_bd = head_diag.astype(BF16)
    tril = (iota((cc, cc), 0) >= iota((cc, cc), 1)).astype(BF16)
    t_row = iota((cc, hc), 0)
    s_col = iota((cc, hc), 1) % cc
    strict = t_row > s_col
    causal = t_row >= s_col
    lanes_diag = iota((hc, w), 0) // cc == iota((hc, w), 1) // n
    tokens_diag = iota((hc, hc), 0) // cc == iota((hc, hc), 1) // cc

    def block_diag(x, diag):
        return jnp.where(diag, jnp.concatenate([x] * SCAN_HEADS, axis=0), 0.0).astype(BF16)

    def bd_lanes(x):
        return block_diag(x, lanes_diag)

    def bd_tokens(x):
        return block_diag(x, tokens_diag)

    def head_sums(*xs):
        pieces = []
        for x in xs:
            hi = x.astype(BF16)
            pieces += [hi, (x - hi.astype(F32)).astype(BF16)]
        both = _dot(jnp.concatenate(pieces, axis=0), ones_bd)
        yield
        return [both[2 * i * cc:(2 * i + 1) * cc] + both[(2 * i + 1) * cc:(2 * i + 2) * cc]
                for i in range(len(xs))]

    def group(gi):
        c = slice(gi * w, (gi + 1) * w)
        r, k, v, ld, a = r_ref[:, c], k_ref[:, c], v_ref[:, c], ld_ref[:, c], a_ref[:, c]

        p1 = ld.astype(BF16)
        r1 = ld - p1.astype(F32)
        p2 = r1.astype(BF16)
        p3 = (r1 - p2.astype(F32)).astype(BF16)
        cum = _dot(tril, jnp.concatenate([p1, p2, p3], axis=1))
        yield
        kk = k * kk_ref[:, c]
        kmod = k * (1.0 + (a - 1.0) * ka_ref[:, c])
        kk_sq, r_k = yield from head_sums(kk * kk, r * kmod * rk_ref[:, c])
        kap = kk / jnp.maximum(jnp.sqrt(kk_sq), 1e-12)
        b = kap * a

        lcum = cum[:, :w] + cum[:, w:2 * w] + cum[:, 2 * w:]
        lend = lcum[cc - 1:cc, :]
        e_inc = jnp.exp(lcum)
        e_exc = jnp.exp(lcum - ld)
        e_neg = jnp.exp(-lcum)
        e_end = jnp.exp(lend - lcum)

        kq = kap * e_exc
        rq = r * e_inc
        qr = jnp.concatenate([kq, rq], axis=0).astype(BF16)
        bk_bd = jnp.concatenate([bd_lanes(b * e_neg), bd_lanes(kmod * e_neg)], axis=0)
        gram = _dot_nt(qr, bk_bd)
        yield
        s_bd = s_ref[gi]
        qs = _dot_nt(qr, s_bd.astype(BF16))
        yield
        a_w = jnp.where(strict, gram[:cc, :hc], 0.0)
        bk_w = jnp.where(strict, gram[:cc, hc:], 0.0)
        cb_w = jnp.where(causal, gram[cc:, :hc], 0.0)
        ck_w = jnp.where(causal, gram[cc:, hc:], 0.0)
        v_bd = bd_lanes(v)

        bv = _dot(bk_w.astype(BF16), v_bd)
        yield
        x = -(qs[:cc] + bv)
        ax = _dot(a_w.astype(BF16), bd_lanes(x))
        yield
        x = x - ax
        a_pow = a_w
        power = 2
        while power < cc:
            a_pow = _dot(a_pow.astype(BF16), bd_tokens(a_pow))
            yield
            ax = _dot(a_pow.astype(BF16), bd_lanes(x))
            yield
            x = x + ax
            power *= 2
        u = x

        y_in = _dot(jnp.concatenate([cb_w, ck_w], axis=1).astype(BF16),
                    jnp.concatenate([bd_lanes(u), v_bd], axis=0))
        yield
        upd = _dot_tn(jnp.concatenate([u, v], axis=0).astype(BF16),
                      jnp.concatenate([b * e_end, kmod * e_end], axis=0).astype(BF16))
        yield
        s_ref[gi] = s_bd * jnp.exp(lend) + jnp.where(head_diag, upd, 0.0)

        y = qs[cc:] + y_in
        mean = (yield from head_sums(y))[0] * (1.0 / n)
        dev = y - mean
        var = (yield from head_sums(dev * dev))[0] * (1.0 / n)
        yn = dev * lax.rsqrt(var + GN_EPS) * lnw_ref[:, c] + lnb_ref[:, c]
        z_ref[:, c] = (yn + r_k * v) * g_ref[:, c]

    _round_robin([group(gi) for gi in range(n_groups)])

    @pl.when(c_idx == pl.num_programs(1) - 1)
    def _():
        for h in range(RWKV_HEADS):
            gi, hh = divmod(h, SCAN_HEADS)
            sout_ref[h] = s_ref[gi, hh * n:(hh + 1) * n, hh * n:(hh + 1) * n]


def _rwkv_scan(proj, k_k, k_a, r_k, ln_w, ln_b, *, n_seq, chunk, n_chunks, s0=None):
    d = D_MODEL
    tok = pl.BlockSpec((chunk, d), lambda b, c: (b * n_chunks + c, 0))
    par = pl.BlockSpec((1, d), lambda b, c: (0, 0))
    state = pl.BlockSpec((None, RWKV_HEADS, RWKV_N, RWKV_N), lambda b, c: (b, 0, 0, 0))
    in_specs = [tok] * 6 + [par] * 5
    args = list(proj) + [p.reshape(1, d) for p in (k_k, k_a, r_k, ln_w, ln_b)]
    if s0 is not None:
        in_specs.append(state)
        args.append(s0)
    n_groups = RWKV_HEADS // SCAN_HEADS
    return pl.pallas_call(
        functools.partial(_scan_kernel, chunk=chunk, has_init=s0 is not None),
        grid=(n_seq, n_chunks),
        in_specs=in_specs,
        out_specs=[tok, state],
        out_shape=[jax.ShapeDtypeStruct((n_seq * n_chunks * chunk, d), F32),
                   jax.ShapeDtypeStruct((n_seq, RWKV_HEADS, RWKV_N, RWKV_N), F32)],
        scratch_shapes=[pltpu.VMEM((n_groups, SCAN_LANES, SCAN_LANES), F32)],
        compiler_params=_params(2),
        name="rwkv_scan",
    )(*args)


def _pad_cols(w):
    return jnp.pad(w, ((0, 0), (0, LORA_PAD - w.shape[1])))


def _pad_rows(w):
    return jnp.pad(w, ((0, LORA_PAD - w.shape[0]), (0, 0)))


def kernel(x_prompt, x_sample, state_pool, cache_win_k, cache_win_v, state_shift, state_wkv, norm_ffn1, norm_mix, norm_ffn2, norm_final, ffn_w_gate, ffn_w_up, ffn_w_down, pool_w, pool_scale, att_w_qkv, att_b_qkv, att_w_o, att_b_o, att_sinks, rel_bias, rwkv_mu, rwkv_w_r, rwkv_w_k, rwkv_w_v, rwkv_w_o, rwkv_w0, rwkv_w1, rwkv_w2, rwkv_a0, rwkv_a1, rwkv_a2, rwkv_g1, rwkv_g2, rwkv_k_k, rwkv_k_a, rwkv_r_k, rwkv_ln_w, rwkv_ln_b):
    d = D_MODEL
    x = jnp.concatenate([x_prompt.reshape(N_PROMPT, d), x_sample.reshape(N_SAMPLE, d)], axis=0)
    wg, wu, wd = ffn_w_gate, ffn_w_up, ffn_w_down
    pool_p, pool_s, wk_p, wv_p, wk_s, wv_s, sh_p, sh_s, wkv_p, wkv_s = ([] for _ in range(10))

    for l in range(DEPTH):
        j, kind = divmod(l, N_MIXERS)
        x = _ffn(x, norm_ffn1[l], wg, wu, wd, l, 0)
        gain = norm_mix[l]
        if kind == 0:
            w_pool = pool_w[j].astype(BF16)
            u_s = _norm_rows(x, gain, tm=N_SAMPLE, row0=N_PROMPT, rows=N_SAMPLE).reshape(DEC_BATCH, DEC_SEQ, d)
            ext = jnp.concatenate([jnp.zeros((DEC_BATCH, 1, d), F32), state_pool[j], u_s], axis=1)
            pool_p.append(_norm_tails(x, gain)[:, -POOL_BUF:])
            pool_s.append(ext[:, -POOL_BUF:])
            x = _pool_prompt(x, gain, w_pool, pool_scale[j])
            x = _pool_sample(ext, x, w_pool, pool_scale[j])
        elif kind == 1:
            qkv = _matmul(x, att_w_qkv[j].astype(BF16), att_b_qkv[j], tm=TM_TOK, tn=QKV_DIM // 5, gain=gain)
            k_buf = cache_win_k[j].reshape(DEC_BATCH, WINDOW, KV_DIM)
            v_buf = cache_win_v[j].reshape(DEC_BATCH, WINDOW, KV_DIM)
            o_p = _attention(qkv, rel_bias, att_sinks[j], tq=WINDOW, n_blocks=N_PROMPT // WINDOW,
                             row0=0, blocks_per_seq=SEQ // WINDOW)
            o_s = _attention(qkv, rel_bias, att_sinks[j], tq=DEC_SEQ, n_blocks=DEC_BATCH,
                             row0=N_PROMPT, blocks_per_seq=0, k_prev=k_buf, v_prev=v_buf)
            w_o = att_w_o[j].astype(BF16)
            x = _matmul(o_p, w_o, att_b_o[j], x, tm=TM_PROMPT, tn=d, res_row0=0)
            x = _matmul(o_s, w_o, att_b_o[j], x, tm=N_SAMPLE, tn=d, res_row0=N_PROMPT)
            kv_shape = (WINDOW, N_KV_HEADS, HEAD_DIM)
            k_cols = slice(ATT_DIM, ATT_DIM + KV_DIM)
            v_cols = slice(ATT_DIM + KV_DIM, QKV_DIM)
            tails = [slice((b + 1) * SEQ - WINDOW, (b + 1) * SEQ) for b in range(BATCH)]
            wk_p.append(jnp.stack([qkv[t, k_cols] for t in tails]).reshape(BATCH, *kv_shape))
            wv_p.append(jnp.stack([qkv[t, v_cols] for t in tails]).reshape(BATCH, *kv_shape))
            new_rows = slice(N_PROMPT, N_TOK)
            k_s = jnp.concatenate([k_buf, qkv[new_rows, k_cols].reshape(DEC_BATCH, DEC_SEQ, KV_DIM)], axis=1)
            v_s = jnp.concatenate([v_buf, qkv[new_rows, v_cols].reshape(DEC_BATCH, DEC_SEQ, KV_DIM)], axis=1)
            wk_s.append(k_s[:, -WINDOW:].reshape(DEC_BATCH, *kv_shape))
            wv_s.append(v_s[:, -WINDOW:].reshape(DEC_BATCH, *kv_shape))
        else:
            weights = (rwkv_mu[j],
                       rwkv_w_r[j].astype(BF16), rwkv_w_k[j].astype(BF16), rwkv_w_v[j].astype(BF16),
                       _pad_cols(rwkv_w1[j]).astype(BF16), _pad_rows(rwkv_w2[j]).astype(BF16), rwkv_w0[j],
                       _pad_cols(rwkv_a1[j]).astype(BF16), _pad_rows(rwkv_a2[j]).astype(BF16), rwkv_a0[j],
                       rwkv_g1[j].astype(BF16), rwkv_g2[j].astype(BF16))
            first_s = jnp.repeat(state_shift[j], DEC_SEQ, axis=0)
            proj_p = _rwkv_proj(x, gain, weights, row0=0, rows=N_PROMPT, tm=TM_PROMPT, seq_len=SEQ)
            proj_s = _rwkv_proj(x, gain, weights, row0=N_PROMPT, rows=N_SAMPLE, tm=N_SAMPLE, seq_len=DEC_SEQ,
                                first=first_s)
            head_params = (rwkv_k_k[j], rwkv_k_a[j], rwkv_r_k[j], rwkv_ln_w[j], rwkv_ln_b[j])
            z_p, s_p = _rwkv_scan(proj_p, *head_params, n_seq=BATCH, chunk=SCAN_CHUNK, n_chunks=SEQ // SCAN_CHUNK)
            z_s, s_s = _rwkv_scan(proj_s, *head_params, n_seq=DEC_BATCH, chunk=DEC_SEQ, n_chunks=1,
                                  s0=state_wkv[j])
            sh_p.append(_norm_tails(x, gain)[:, -1])
            sh_s.append(_norm_rows(x, gain, tm=N_SAMPLE, row0=N_PROMPT, rows=N_SAMPLE)
                        .reshape(DEC_BATCH, DEC_SEQ, d)[:, -1])
            w_o = rwkv_w_o[j].astype(BF16)
            x = _matmul(z_p, w_o, None, x, tm=TM_PROMPT, tn=d, res_row0=0)
            x = _matmul(z_s, w_o, None, x, tm=N_SAMPLE, tn=d, res_row0=N_PROMPT)
            wkv_p.append(s_p)
            wkv_s.append(s_s)
        x = _ffn(x, norm_ffn2[l], wg, wu, wd, l, 1)

    y_p = _norm_rows(x, norm_final, tm=TM_PROMPT, row0=0, rows=N_PROMPT)
    y_s = _norm_rows(x, norm_final, tm=N_SAMPLE, row0=N_PROMPT, rows=N_SAMPLE)
    return (y_p.reshape(BATCH, SEQ, d), y_s.reshape(DEC_BATCH, DEC_SEQ, d),
            jnp.stack(pool_p), jnp.stack(pool_s),
            jnp.stack(wk_p), jnp.stack(wv_p), jnp.stack(wk_s), jnp.stack(wv_s),
            jnp.stack(sh_p), jnp.stack(sh_s),
            jnp.stack(wkv_p), jnp.stack(wkv_s))
```

```python
import functools
import math

import numpy as np
import jax
import jax.numpy as jnp
from jax import lax
from jax.experimental import pallas as pl
from jax.experimental.pallas import tpu as pltpu

F32 = jnp.float32
BF16 = jnp.bfloat16

D_MODEL = 2048
BATCH = 2
SEQ = 4096
DEPTH = 4
DEC_BATCH = 32
DEC_SEQ = 8
PAST_LEN = 16384
N_MIXERS = 3
RMS_EPS = 1e-6
D_FF = 5632
POOL_WINDOWS = (2, 4, 8, 16)
POOL_GROUPS = 4
POOL_GROUP_DIM = D_MODEL // POOL_GROUPS
POOL_BUF = max(POOL_WINDOWS) - 1
POOL_HALO = POOL_BUF + 1
HEAD_DIM = 64
N_HEADS = D_MODEL // HEAD_DIM
N_KV_HEADS = 4
GQA_GROUP = N_HEADS // N_KV_HEADS
ATT_DIM = N_HEADS * HEAD_DIM
KV_DIM = N_KV_HEADS * HEAD_DIM
QKV_DIM = ATT_DIM + 2 * KV_DIM
WINDOW = 128
ATT_SCALE = HEAD_DIM ** -0.5
T5_BUCKETS = 32
T5_MAX_DISTANCE = 128
NEG_INF = -1e30
RWKV_N = 64
RWKV_HEADS = D_MODEL // RWKV_N
GN_EPS = 64e-5
LORA_PAD = 128

N_PROMPT = BATCH * SEQ
N_SAMPLE = DEC_BATCH * DEC_SEQ
N_TOK = N_PROMPT + N_SAMPLE

VMEM_LIMIT_BYTES = 56 * 1024 * 1024

TM_TOK = 704
TM_PROMPT = 512
TM_POOL = 256
SHIFT_HALO = 8
TF_FFN = 512
TN_PROJ = 512
SCAN_CHUNK = 64
SCAN_HEADS = 2
SCAN_LANES = SCAN_HEADS * RWKV_N


def _params(n_axes):
    return pltpu.CompilerParams(dimension_semantics=("arbitrary",) * n_axes,
                                vmem_limit_bytes=VMEM_LIMIT_BYTES)


def _dot(a, b):
    return jnp.dot(a, b, preferred_element_type=F32)


def _dot_nt(a, b):
    return lax.dot_general(a, b, (((1,), (1,)), ((), ())), preferred_element_type=F32)


def _dot_tn(a, b):
    return lax.dot_general(a, b, (((0,), (0,)), ((), ())), preferred_element_type=F32)


def _round_robin(gens):
    live = list(gens)
    while live:
        still = []
        for gen in live:
            try:
                next(gen)
                still.append(gen)
            except StopIteration:
                pass
        live = still


def _rms_norm(x, g):
    return x * lax.rsqrt(jnp.mean(x * x, axis=-1, keepdims=True) + RMS_EPS) * g


def _norm_kernel(x_ref, g_ref, o_ref):
    o_ref[...] = _rms_norm(x_ref[...], g_ref[...])


def _norm_rows(x, g, *, tm, row0, rows):
    d = x.shape[1]
    off = row0 // tm
    return pl.pallas_call(
        _norm_kernel,
        grid=(rows // tm,),
        in_specs=[pl.BlockSpec((tm, d), lambda i: (i + off, 0)),
                  pl.BlockSpec((1, d), lambda i: (0, 0))],
        out_specs=pl.BlockSpec((tm, d), lambda i: (i, 0)),
        out_shape=jax.ShapeDtypeStruct((rows, d), F32),
        compiler_params=_params(1),
        name="rms_norm",
    )(x, g.reshape(1, d))


def _norm_tails(x, g):
    d = x.shape[1]
    blocks_per_seq = SEQ // POOL_HALO
    out = pl.pallas_call(
        _norm_kernel,
        grid=(BATCH,),
        in_specs=[pl.BlockSpec((POOL_HALO, d), lambda b: ((b + 1) * blocks_per_seq - 1, 0)),
                  pl.BlockSpec((1, d), lambda b: (0, 0))],
        out_specs=pl.BlockSpec((POOL_HALO, d), lambda b: (b, 0)),
        out_shape=jax.ShapeDtypeStruct((BATCH * POOL_HALO, d), F32),
        compiler_params=_params(1),
        name="rms_norm_tails",
    )(x, g.reshape(1, d))
    return out.reshape(BATCH, POOL_HALO, d)


def _ffn_kernel(x_ref, g_ref, wg_ref, wu_ref, wd_ref, o_ref, xn_ref, acc_ref):
    j = pl.program_id(1)

    @pl.when(j == 0)
    def _():
        xn_ref[...] = _rms_norm(x_ref[...], g_ref[...]).astype(BF16)
        acc_ref[...] = jnp.zeros_like(acc_ref)

    xn = xn_ref[...]
    gate = _dot(xn, wg_ref[...])
    up = _dot(xn, wu_ref[...])
    h = (gate * jax.nn.sigmoid(gate) * up).astype(BF16)
    acc_ref[...] += _dot(h, wd_ref[...])

    @pl.when(j == pl.num_programs(1) - 1)
    def _():
        o_ref[...] = x_ref[...] + 0.5 * acc_ref[...]


def _ffn(x, g, wg, wu, wd, layer, half):
    m, d = x.shape
    tm, tf = TM_TOK, TF_FFN
    w_in = pl.BlockSpec((None, None, d, tf), lambda i, j: (layer, half, 0, j))
    return pl.pallas_call(
        _ffn_kernel,
        grid=(m // tm, D_FF // tf),
        in_specs=[pl.BlockSpec((tm, d), lambda i, j: (i, 0)),
                  pl.BlockSpec((1, d), lambda i, j: (0, 0)),
                  w_in, w_in,
                  pl.BlockSpec((None, None, tf, d), lambda i, j: (layer, half, j, 0))],
        out_specs=pl.BlockSpec((tm, d), lambda i, j: (i, 0)),
        out_shape=jax.ShapeDtypeStruct((m, d), F32),
        scratch_shapes=[pltpu.VMEM((tm, d), BF16), pltpu.VMEM((tm, d), F32)],
        compiler_params=_params(2),
        name="ffn_half_step",
    )(x, g.reshape(1, d), wg, wu, wd)


def _matmul_kernel(*refs, has_gain, has_bias, has_res):
    refs = list(refs)
    lhs_ref, w_ref = refs[0], refs[1]
    pos = 2
    g_ref = b_ref = res_ref = None
    if has_gain:
        g_ref = refs[pos]
        pos += 1
    if has_bias:
        b_ref = refs[pos]
        pos += 1
    if has_res:
        res_ref = refs[pos]
        pos += 1
    o_ref, lhs_bf_ref = refs[pos], refs[pos + 1]

    @pl.when(pl.program_id(1) == 0)
    def _():
        lhs = lhs_ref[...]
        if has_gain:
            lhs = _rms_norm(lhs, g_ref[...])
        lhs_bf_ref[...] = lhs.astype(BF16)

    acc = _dot(lhs_bf_ref[...], w_ref[...])
    if has_bias:
        acc = acc + b_ref[...]
    if has_res:
        acc = res_ref[...] + acc
    o_ref[...] = acc


def _matmul(lhs, w, bias=None, res=None, *, tm, tn, res_row0=0, gain=None):
    m, k = lhs.shape
    n = w.shape[1]
    off = res_row0 // tm
    in_specs = [pl.BlockSpec((tm, k), lambda i, j: (i, 0)),
                pl.BlockSpec((k, tn), lambda i, j: (0, j))]
    args = [lhs, w]
    if gain is not None:
        in_specs.append(pl.BlockSpec((1, k), lambda i, j: (0, 0)))
        args.append(gain.reshape(1, k))
    if bias is not None:
        in_specs.append(pl.BlockSpec((1, tn), lambda i, j: (0, j)))
        args.append(bias.reshape(1, n))
    aliases = {}
    if res is not None:
        in_specs.append(pl.BlockSpec((tm, tn), lambda i, j: (i + off, j)))
        aliases = {len(args): 0}
        args.append(res)
        out_shape = jax.ShapeDtypeStruct(res.shape, F32)
    else:
        out_shape = jax.ShapeDtypeStruct((m, n), F32)
    return pl.pallas_call(
        functools.partial(_matmul_kernel, has_gain=gain is not None, has_bias=bias is not None,
                          has_res=res is not None),
        grid=(m // tm, n // tn),
        in_specs=in_specs,
        out_specs=pl.BlockSpec((tm, tn), lambda i, j: (i + off, j)),
        out_shape=out_shape,
        scratch_shapes=[pltpu.VMEM((tm, k), BF16)],
        input_output_aliases=aliases,
        compiler_params=_params(2),
        name="matmul_bias_residual",
    )(*args)


def _pool_group_out(diff, gi, x_ref, w_ref, sc_ref, o_ref):
    c = slice(gi * POOL_GROUP_DIM, (gi + 1) * POOL_GROUP_DIM)
    out = _dot(diff.astype(BF16), w_ref[gi])
    o_ref[:, c] = x_ref[:, c] + out * sc_ref[:, c]


def _pool_prompt_kernel(x_ref, halo_ref, g_ref, w_ref, sc_ref, o_ref, ext_ref, *, tm, tiles_per_seq, n_tiles):
    i = pl.program_id(0)

    @pl.when(i < n_tiles)
    def _():
        t_in_seq = i % tiles_per_seq
        g = g_ref[...]
        ext_ref[0:POOL_HALO, :] = jnp.where(t_in_seq == 0, 0.0, _rms_norm(halo_ref[...], g))
        ext_ref[POOL_HALO:, :] = _rms_norm(x_ref[...], g)
        pos = t_in_seq * tm + lax.broadcasted_iota(jnp.int32, (tm, 1), 0)
        for gi, w in enumerate(POOL_WINDOWS):
            c = slice(gi * POOL_GROUP_DIM, (gi + 1) * POOL_GROUP_DIM)
            cur = ext_ref[pl.ds(POOL_HALO, tm), c]
            s = cur
            for back in range(1, w):
                s = s + ext_ref[pl.ds(POOL_HALO - back, tm), c]
            cnt = jnp.minimum(w, pos + 1).astype(F32)
            _pool_group_out(s / cnt - cur, gi, x_ref, w_ref, sc_ref, o_ref)

    @pl.when(i >= n_tiles)
    def _():
        o_ref[...] = x_ref[...]


def _pool_prompt(x, gain, w_pool, scale):
    d = x.shape[1]
    tm = TM_POOL
    halo_blocks = tm // POOL_HALO
    return pl.pallas_call(
        functools.partial(_pool_prompt_kernel, tm=tm, tiles_per_seq=SEQ // tm, n_tiles=N_PROMPT // tm),
        grid=(N_TOK // tm,),
        in_specs=[pl.BlockSpec((tm, d), lambda i: (i, 0)),
                  pl.BlockSpec((POOL_HALO, d), lambda i: (jnp.maximum(i * halo_blocks - 1, 0), 0)),
                  pl.BlockSpec((1, d), lambda i: (0, 0)),
                  pl.BlockSpec((POOL_GROUPS, POOL_GROUP_DIM, POOL_GROUP_DIM), lambda i: (0, 0, 0)),
                  pl.BlockSpec((1, d), lambda i: (0, 0))],
        out_specs=pl.BlockSpec((tm, d), lambda i: (i, 0)),
        out_shape=jax.ShapeDtypeStruct(x.shape, F32),
        scratch_shapes=[pltpu.VMEM((tm + POOL_HALO, d), F32)],
        compiler_params=_params(1),
        name="pool_prompt",
    )(x, x, gain.reshape(1, d), w_pool, scale.reshape(1, d))


def _pool_sample_kernel(ext_ref, x_ref, w_ref, sc_ref, o_ref):
    for gi, w in enumerate(POOL_WINDOWS):
        c = slice(gi * POOL_GROUP_DIM, (gi + 1) * POOL_GROUP_DIM)
        cur = ext_ref[:, pl.ds(POOL_HALO, DEC_SEQ), c]
        s = cur
        for back in range(1, w):
            s = s + ext_ref[:, pl.ds(POOL_HALO - back, DEC_SEQ), c]
        cnt = min(w, PAST_LEN + 1)
        diff = (s / float(cnt) - cur).reshape(N_SAMPLE, POOL_GROUP_DIM)
        _pool_group_out(diff, gi, x_ref, w_ref, sc_ref, o_ref)


def _pool_sample(ext, x, w_pool, scale):
    d = x.shape[1]
    blk = N_PROMPT // N_SAMPLE
    return pl.pallas_call(
        _pool_sample_kernel,
        grid=(1,),
        in_specs=[pl.BlockSpec(ext.shape, lambda i: (0, 0, 0)),
                  pl.BlockSpec((N_SAMPLE, d), lambda i: (blk, 0)),
                  pl.BlockSpec((POOL_GROUPS, POOL_GROUP_DIM, POOL_GROUP_DIM), lambda i: (0, 0, 0)),
                  pl.BlockSpec((1, d), lambda i: (0, 0))],
        out_specs=pl.BlockSpec((N_SAMPLE, d), lambda i: (blk, 0)),
        out_shape=jax.ShapeDtypeStruct(x.shape, F32),
        input_output_aliases={1: 0},
        compiler_params=_params(1),
        name="pool_sample",
    )(ext, x, w_pool, scale.reshape(1, d))


def _t5_bucket_table(tq):
    qi = np.arange(tq)[:, None]
    kj = np.arange(WINDOW + tq)[None, :]
    dist = qi + WINDOW - kj
    exact = T5_BUCKETS // 2
    ratio = np.log(np.maximum(dist, 1) / exact) / math.log(T5_MAX_DISTANCE / exact)
    large = np.minimum(exact + (ratio * (T5_BUCKETS - exact)).astype(np.int64), T5_BUCKETS - 1)
    bucket = np.where(dist < exact, dist, large)
    valid = (dist >= 0) & (dist < WINDOW)
    table = np.where(valid, bucket, -1).astype(np.int32)
    return table[:, :WINDOW], table[:, WINDOW:]


def _attn_sample_kernel(rb_ref, sink_ref, bkt_p_ref, bkt_c_ref, q_ref, kc_ref, vc_ref, kp_ref, vp_ref,
                        o_ref, bias_p_ref, bias_c_ref, sink_col_ref, *, tq):
    step = pl.program_id(0)

    @pl.when(step == 0)
    def _():
        bkt_p = bkt_p_ref[...]
        bkt_c = bkt_c_ref[...]
        bias_p_ref[...] = jnp.zeros_like(bias_p_ref)
        bias_c_ref[...] = jnp.zeros_like(bias_c_ref)

        def add_bucket(b, carry):
            eq_p = bkt_p == b
            eq_c = bkt_c == b
            for h in range(N_HEADS):
                kvh, g = divmod(h, GQA_GROUP)
                r = slice(g * tq, (g + 1) * tq)
                val = rb_ref[b, h]
                bias_p_ref[kvh, r, :] += jnp.where(eq_p, val, 0.0)
                bias_c_ref[kvh, r, :] += jnp.where(eq_c, val, 0.0)
            return carry

        lax.fori_loop(0, T5_BUCKETS, add_bucket, 0)
        for h in range(N_HEADS):
            kvh, g = divmod(h, GQA_GROUP)
            r = slice(g * tq, (g + 1) * tq)
            bias_p_ref[kvh, r, :] = jnp.where(bkt_p < 0, NEG_INF, bias_p_ref[kvh, r, :])
            bias_c_ref[kvh, r, :] = jnp.where(bkt_c < 0, NEG_INF, bias_c_ref[kvh, r, :])
            sink_col_ref[kvh, r, :] = jnp.full((tq, 1), sink_ref[h], F32)

    def kv_head(kvh):
        heads = [kvh * GQA_GROUP + g for g in range(GQA_GROUP)]
        qs = jnp.concatenate([q_ref[:, h * HEAD_DIM:(h + 1) * HEAD_DIM] for h in heads], axis=0).astype(BF16)
        c = slice(kvh * HEAD_DIM, (kvh + 1) * HEAD_DIM)
        qk_p = _dot_nt(qs, kp_ref[:, c].astype(BF16))
        qk_c = _dot_nt(qs, kc_ref[:, c].astype(BF16))
        yield
        s_p = qk_p * ATT_SCALE + bias_p_ref[kvh]
        s_c = qk_c * ATT_SCALE + bias_c_ref[kvh]
        sink = sink_col_ref[kvh]
        m = jnp.maximum(jnp.maximum(jnp.max(s_p, axis=-1, keepdims=True),
                                    jnp.max(s_c, axis=-1, keepdims=True)), sink)
        p_p = jnp.exp(s_p - m)
        p_c = jnp.exp(s_c - m)
        den = (jnp.sum(p_p, axis=-1, keepdims=True) + jnp.sum(p_c, axis=-1, keepdims=True)
               + jnp.exp(sink - m))
        o_p = _dot(p_p.astype(BF16), vp_ref[:, c].astype(BF16))
        o_c = _dot(p_c.astype(BF16), vc_ref[:, c].astype(BF16))
        yield
        o = (o_p + o_c) / den
        for g, h in enumerate(heads):
            o_ref[:, h * HEAD_DIM:(h + 1) * HEAD_DIM] = o[g * tq:(g + 1) * tq, :]

    _round_robin([kv_head(kvh) for kvh in range(N_KV_HEADS)])


def _attention_sample(qkv, rel_bias, sinks, k_prev, v_prev):
    tq = DEC_SEQ
    bkt_p, bkt_c = _t5_bucket_table(tq)
    off = N_PROMPT // tq
    kcol = ATT_DIM // KV_DIM
    smem = pl.BlockSpec(memory_space=pltpu.SMEM)
    cache = pl.BlockSpec((None, WINDOW, KV_DIM), lambda i: (i, 0, 0))
    rows = GQA_GROUP * tq
    return pl.pallas_call(
        functools.partial(_attn_sample_kernel, tq=tq),
        grid=(DEC_BATCH,),
        in_specs=[smem, smem,
                  pl.BlockSpec((tq, WINDOW), lambda i: (0, 0)),
                  pl.BlockSpec((tq, tq), lambda i: (0, 0)),
                  pl.BlockSpec((tq, ATT_DIM), lambda i: (i + off, 0)),
                  pl.BlockSpec((tq, KV_DIM), lambda i: (i + off, kcol)),
                  pl.BlockSpec((tq, KV_DIM), lambda i: (i + off, kcol + 1)),
                  cache, cache],
        out_specs=pl.BlockSpec((tq, ATT_DIM), lambda i: (i, 0)),
        out_shape=jax.ShapeDtypeStruct((N_SAMPLE, ATT_DIM), F32),
        scratch_shapes=[pltpu.VMEM((N_KV_HEADS, rows, WINDOW), F32),
                        pltpu.VMEM((N_KV_HEADS, rows, tq), F32),
                        pltpu.VMEM((N_KV_HEADS, rows, 1), F32)],
        compiler_params=_params(1),
        name="swa_attention_sample",
    )(rel_bias, sinks, jnp.asarray(bkt_p), jnp.asarray(bkt_c), qkv, qkv, qkv, k_prev, v_prev)


def _attn_prompt_kernel(rb_ref, sink_ref, bkt_ref, q_ref, kc_ref, vc_ref, kp_ref, vp_ref, o_ref, bias_ref,
                        *, blocks_per_seq):
    step = pl.program_id(0)
    tq, nk, pair = WINDOW, 2 * WINDOW, 2 * HEAD_DIM

    @pl.when(step == 0)
    def _():
        bkt = bkt_ref[...]
        bias_ref[...] = jnp.zeros_like(bias_ref)

        def add_bucket(b, carry):
            eq = bkt == b
            for h in range(N_HEADS):
                bias_ref[h] += jnp.where(eq, rb_ref[b, h], 0.0)
            return carry

        lax.fori_loop(0, T5_BUCKETS, add_bucket, 0)
        for h in range(N_HEADS):
            bias_ref[h] = jnp.where(bkt < 0, NEG_INF, bias_ref[h])

    no_prev = step % blocks_per_seq == 0
    dead = jnp.logical_and(no_prev, lax.broadcasted_iota(jnp.int32, (nk, tq), 0) < WINDOW)
    lane_half = lax.broadcasted_iota(jnp.int32, (nk, pair), 1) // HEAD_DIM
    row_half = lax.broadcasted_iota(jnp.int32, (pair, tq), 0) // HEAD_DIM

    for kvh in range(N_KV_HEADS):
        tile = slice((kvh // 2) * pair, (kvh // 2 + 1) * pair)
        half = kvh % 2

        def both_halves(prev_ref, cur_ref):
            x = jnp.concatenate([prev_ref[:, tile], cur_ref[:, tile]], axis=0)
            own = jnp.where(lane_half == half, x, 0.0)
            other = pltpu.roll(own, HEAD_DIM, axis=1)
            return (own, other) if half == 0 else (other, own)

        k_lo, k_hi = both_halves(kp_ref, kc_ref)
        v_lo, v_hi = both_halves(vp_ref, vc_ref)
        k2 = jnp.concatenate([k_lo, k_hi], axis=0).astype(BF16)
        v_lo_t = v_lo.T.astype(BF16)
        v_hi_t = v_hi.T.astype(BF16)

        def head_pair(j):
            cols = slice(j * pair, (j + 1) * pair)
            qp = (q_ref[:, cols] * ATT_SCALE).astype(BF16)
            s2 = _dot_nt(k2, qp)
            yield
            probs, dens = [], []
            for t in range(2):
                h = 2 * j + t
                s = s2[t * nk:(t + 1) * nk] + bias_ref[h]
                s = jnp.where(dead, NEG_INF, s)
                sink = sink_ref[h]
                m = jnp.maximum(jnp.max(s, axis=0, keepdims=True), sink)
                p = jnp.exp(s - m)
                dens.append(jnp.sum(p, axis=0, keepdims=True) + jnp.exp(sink - m))
                probs.append(p.astype(BF16))
            o2 = _dot(v_lo_t, probs[0]) + _dot(v_hi_t, probs[1])
            yield
            o2 = o2 / jnp.where(row_half == 0, dens[0], dens[1])
            o_ref[:, cols] = o2.T

        pairs_per_kv = GQA_GROUP // 2
        _round_robin([head_pair(kvh * pairs_per_kv + jj) for jj in range(pairs_per_kv)])


def _attention_prompt(qkv, rel_bias, sinks):
    bkt_p, bkt_c = _t5_bucket_table(WINDOW)
    bkt_t = np.ascontiguousarray(np.concatenate([bkt_p, bkt_c], axis=1).T)
    kcol = ATT_DIM // KV_DIM
    smem = pl.BlockSpec(memory_space=pltpu.SMEM)
    prev = lambda i: jnp.maximum(i - 1, 0)
    return pl.pallas_call(
        functools.partial(_attn_prompt_kernel, blocks_per_seq=SEQ // WINDOW),
        grid=(N_PROMPT // WINDOW,),
        in_specs=[smem, smem,
                  pl.BlockSpec((2 * WINDOW, WINDOW), lambda i: (0, 0)),
                  pl.BlockSpec((WINDOW, ATT_DIM), lambda i: (i, 0)),
                  pl.BlockSpec((WINDOW, KV_DIM), lambda i: (i, kcol)),
                  pl.BlockSpec((WINDOW, KV_DIM), lambda i: (i, kcol + 1)),
                  pl.BlockSpec((WINDOW, KV_DIM), lambda i: (prev(i), kcol)),
                  pl.BlockSpec((WINDOW, KV_DIM), lambda i: (prev(i), kcol + 1))],
        out_specs=pl.BlockSpec((WINDOW, ATT_DIM), lambda i: (i, 0)),
        out_shape=jax.ShapeDtypeStruct((N_PROMPT, ATT_DIM), F32),
        scratch_shapes=[pltpu.VMEM((N_HEADS, 2 * WINDOW, WINDOW), F32)],
        compiler_params=_params(1),
        name="swa_attention_prompt",
    )(rel_bias, sinks, jnp.asarray(bkt_t), qkv, qkv, qkv, qkv, qkv)


def _rwkv_proj_kernel(*refs, tm, seq_len, has_first):
    refs = list(refs)
    x_ref, halo_ref, gain_ref = refs[:3]
    pos = 3
    first_ref = None
    if has_first:
        first_ref = refs[pos]
        pos += 1
    (mu_ref, wr_ref, wk_ref, wv_ref, w1_ref, w2_ref, w0_ref, a1_ref, a2_ref, a0_ref, g1_ref, g2_ref,
     r_ref, k_ref, v_ref, ld_ref, a_ref, g_ref,
     ext_ref, xr_ref, xk_ref, xv_ref, hw_ref, ha_ref, hg_ref) = refs[pos:]

    @pl.when(pl.program_id(1) == 0)
    def _():
        gain = gain_ref[...]
        u = _rms_norm(x_ref[...], gain)
        ext_ref[0:SHIFT_HALO, :] = _rms_norm(halo_ref[...], gain)
        ext_ref[SHIFT_HALO:, :] = u
        prev = ext_ref[pl.ds(SHIFT_HALO - 1, tm), :]
        row = pl.program_id(0) * tm + lax.broadcasted_iota(jnp.int32, (tm, 1), 0)
        starts = row % seq_len == 0
        prev = jnp.where(starts, first_ref[...] if has_first else 0.0, prev)
        dx = prev - u
        mix = lambda i: (u + dx * mu_ref[i:i + 1, :]).astype(BF16)
        xr_ref[...] = mix(0)
        hw_ref[...] = jnp.tanh(_dot(mix(1), w1_ref[...])).astype(BF16)
        xk_ref[...] = mix(2)
        xv_ref[...] = mix(3)
        ha_ref[...] = _dot(mix(4), a1_ref[...]).astype(BF16)
        hg_ref[...] = jax.nn.sigmoid(_dot(mix(5), g1_ref[...])).astype(BF16)

    r_ref[...] = _dot(xr_ref[...], wr_ref[...])
    k_ref[...] = _dot(xk_ref[...], wk_ref[...])
    v_ref[...] = _dot(xv_ref[...], wv_ref[...])
    z = -(w0_ref[...] + _dot(hw_ref[...], w2_ref[...]))
    softplus = jnp.maximum(z, 0.0) + jnp.log1p(jnp.exp(-jnp.abs(z)))
    ld_ref[...] = -jnp.exp(-softplus - 0.5)
    a_ref[...] = jax.nn.sigmoid(a0_ref[...] + _dot(ha_ref[...], a2_ref[...]))
    g_ref[...] = _dot(hg_ref[...], g2_ref[...])


def _rwkv_proj(x, gain, weights, *, row0, rows, tm, seq_len, first=None):
    mu, w_r, w_k, w_v, w1, w2, w0, a1, a2, a0, g1, g2 = weights
    d = x.shape[1]
    tn = TN_PROJ
    gl = g1.shape[1]
    off = row0 // tm
    halo_blocks = tm // SHIFT_HALO
    halo_off = row0 // SHIFT_HALO
    row = lambda i, j: (i + off, 0)
    col = lambda i, j: (0, j)
    fixed = lambda i, j: (0, 0)
    in_specs = [pl.BlockSpec((tm, d), row),
                pl.BlockSpec((SHIFT_HALO, d), lambda i, j: (jnp.maximum(halo_off + i * halo_blocks - 1, 0), 0)),
                pl.BlockSpec((1, d), fixed)]
    args = [x, x, gain.reshape(1, d)]
    if first is not None:
        in_specs.append(pl.BlockSpec((tm, d), lambda i, j: (i, 0)))
        args.append(first)
    in_specs += [pl.BlockSpec((6, d), fixed),
                 pl.BlockSpec((d, tn), col), pl.BlockSpec((d, tn), col), pl.BlockSpec((d, tn), col),
                 pl.BlockSpec((d, LORA_PAD), fixed), pl.BlockSpec((LORA_PAD, tn), col), pl.BlockSpec((1, tn), col),
                 pl.BlockSpec((d, LORA_PAD), fixed), pl.BlockSpec((LORA_PAD, tn), col), pl.BlockSpec((1, tn), col),
                 pl.BlockSpec((d, gl), fixed), pl.BlockSpec((gl, tn), col)]
    args += [mu, w_r, w_k, w_v, w1, w2, w0.reshape(1, d), a1, a2, a0.reshape(1, d), g1, g2]
    out = jax.ShapeDtypeStruct((rows, d), F32)
    return pl.pallas_call(
        functools.partial(_rwkv_proj_kernel, tm=tm, seq_len=seq_len, has_first=first is not None),
        grid=(rows // tm, d // tn),
        in_specs=in_specs,
        out_specs=[pl.BlockSpec((tm, tn), lambda i, j: (i, j))] * 6,
        out_shape=[out] * 6,
        scratch_shapes=[pltpu.VMEM((tm + SHIFT_HALO, d), F32)] + [pltpu.VMEM((tm, d), BF16)] * 3
        + [pltpu.VMEM((tm, LORA_PAD), BF16)] * 2 + [pltpu.VMEM((tm, gl), BF16)],
        compiler_params=_params(2),
        name="rwkv_projections",
    )(*args)


def _scan_kernel(*refs, chunk, has_init):
    (r_ref, k_ref, v_ref, ld_ref, a_ref, g_ref, kk_ref, ka_ref, rk_ref, lnw_ref, lnb_ref) = refs[:11]
    refs = refs[11:]
    if has_init:
        s0_ref, refs = refs[0], refs[1:]
    z_ref, sout_ref, s_ref = refs
    c_idx = pl.program_id(1)
    cc, n, w = chunk, RWKV_N, SCAN_LANES
    hc = SCAN_HEADS * cc
    n_groups = RWKV_HEADS // SCAN_HEADS

    @pl.when(c_idx == 0)
    def _():
        s_ref[...] = jnp.zeros_like(s_ref)
        if has_init:
            for h in range(RWKV_HEADS):
                gi, hh = divmod(h, SCAN_HEADS)
                s_ref[gi, hh * n:(hh + 1) * n, hh * n:(hh + 1) * n] = s0_ref[h]

    def iota(shape, axis):
        return lax.broadcasted_iota(jnp.int32, shape, axis)

    head_diag = iota((w, w), 0) // n == iota((w, w), 1) // n
    tril = (iota((cc, cc), 0) >= iota((cc, cc), 1)).astype(BF16)
    t_row = iota((cc, hc), 0)
    s_col = iota((cc, hc), 1) % cc
    strict = t_row > s_col
    causal = t_row >= s_col
    lanes_diag = iota((hc, w), 0) // cc == iota((hc, w), 1) // n
    tokens_diag = iota((hc, hc), 0) // cc == iota((hc, hc), 1) // cc
    lane_head = iota((cc, w), 1) // n

    def block_diag(x, diag):
        return jnp.where(diag, jnp.concatenate([x] * SCAN_HEADS, axis=0), 0.0).astype(BF16)

    def bd_lanes(x):
        return block_diag(x, lanes_diag)

    def bd_tokens(x):
        return block_diag(x, tokens_diag)

    def head_sum(x):
        out = jnp.zeros_like(x)
        for hh in range(SCAN_HEADS):
            mine = lane_head == hh
            out = jnp.where(mine, jnp.sum(jnp.where(mine, x, 0.0), axis=-1, keepdims=True), out)
        return out

    def group(gi):
        c = slice(gi * w, (gi + 1) * w)
        r, k, v, ld, a = r_ref[:, c], k_ref[:, c], v_ref[:, c], ld_ref[:, c], a_ref[:, c]

        p1 = ld.astype(BF16)
        r1 = ld - p1.astype(F32)
        p2 = r1.astype(BF16)
        p3 = (r1 - p2.astype(F32)).astype(BF16)
        cum = _dot(tril, jnp.concatenate([p1, p2, p3], axis=1))
        yield
        kk = k * kk_ref[:, c]
        kmod = k * (1.0 + (a - 1.0) * ka_ref[:, c])
        kap = kk / jnp.maximum(jnp.sqrt(head_sum(kk * kk)), 1e-12)
        b = kap * a

        lcum = cum[:, :w] + cum[:, w:2 * w] + cum[:, 2 * w:]
        lend = lcum[cc - 1:cc, :]
        e_inc = jnp.exp(lcum)
        e_exc = jnp.exp(lcum - ld)
        e_neg = jnp.exp(-lcum)
        e_end = jnp.exp(lend - lcum)

        kq = kap * e_exc
        rq = r * e_inc
        qr = jnp.concatenate([kq, rq], axis=0).astype(BF16)
        bk_bd = jnp.concatenate([bd_lanes(b * e_neg), bd_lanes(kmod * e_neg)], axis=0)
        gram = _dot_nt(qr, bk_bd)
        yield
        s_bd = s_ref[gi]
        qs = _dot_nt(qr, s_bd.astype(BF16))
        yield
        a_w = jnp.where(strict, gram[:cc, :hc], 0.0)
        bk_w = jnp.where(strict, gram[:cc, hc:], 0.0)
        cb_w = jnp.where(causal, gram[cc:, :hc], 0.0)
        ck_w = jnp.where(causal, gram[cc:, hc:], 0.0)
        v_bd = bd_lanes(v)

        bv = _dot(bk_w.astype(BF16), v_bd)
        yield
        x = -(qs[:cc] + bv)
        ax = _dot(a_w.astype(BF16), bd_lanes(x))
        yield
        x = x - ax
        a_pow = a_w
        power = 2
        while power < cc:
            a_pow = _dot(a_pow.astype(BF16), bd_tokens(a_pow))
            yield
            ax = _dot(a_pow.astype(BF16), bd_lanes(x))
            yield
            x = x + ax
            power *= 2
        u = x

        y_in = _dot(jnp.concatenate([cb_w, ck_w], axis=1).astype(BF16),
                    jnp.concatenate([bd_lanes(u), v_bd], axis=0))
        yield
        upd = _dot_tn(jnp.concatenate([u, v], axis=0).astype(BF16),
                      jnp.concatenate([b * e_end, kmod * e_end], axis=0).astype(BF16))
        yield
        s_ref[gi] = s_bd * jnp.exp(lend) + jnp.where(head_diag, upd, 0.0)

        y = qs[cc:] + y_in
        mean = head_sum(y) * (1.0 / n)
        dev = y - mean
        var = head_sum(dev * dev) * (1.0 / n)
        yn = dev * lax.rsqrt(var + GN_EPS) * lnw_ref[:, c] + lnb_ref[:, c]
        bonus = head_sum(r * kmod * rk_ref[:, c]) * v
        z_ref[:, c] = (yn + bonus) * g_ref[:, c]

    _round_robin([group(gi) for gi in range(n_groups)])

    @pl.when(c_idx == pl.num_programs(1) - 1)
    def _():
        for h in range(RWKV_HEADS):
            gi, hh = divmod(h, SCAN_HEADS)
            sout_ref[h] = s_ref[gi, hh * n:(hh + 1) * n, hh * n:(hh + 1) * n]


def _rwkv_scan(proj, k_k, k_a, r_k, ln_w, ln_b, *, n_seq, chunk, n_chunks, s0=None):
    d = D_MODEL
    tok = pl.BlockSpec((chunk, d), lambda b, c: (b * n_chunks + c, 0))
    par = pl.BlockSpec((1, d), lambda b, c: (0, 0))
    state = pl.BlockSpec((None, RWKV_HEADS, RWKV_N, RWKV_N), lambda b, c: (b, 0, 0, 0))
    in_specs = [tok] * 6 + [par] * 5
    args = list(proj) + [p.reshape(1, d) for p in (k_k, k_a, r_k, ln_w, ln_b)]
    if s0 is not None:
        in_specs.append(state)
        args.append(s0)
    n_groups = RWKV_HEADS // SCAN_HEADS
    return pl.pallas_call(
        functools.partial(_scan_kernel, chunk=chunk, has_init=s0 is not None),
        grid=(n_seq, n_chunks),
        in_specs=in_specs,
        out_specs=[tok, state],
        out_shape=[jax.ShapeDtypeStruct((n_seq * n_chunks * chunk, d), F32),
                   jax.ShapeDtypeStruct((n_seq, RWKV_HEADS, RWKV_N, RWKV_N), F32)],
        scratch_shapes=[pltpu.VMEM((n_groups, SCAN_LANES, SCAN_LANES), F32)],
        compiler_params=_params(2),
        name="rwkv_scan",
    )(*args)


def _pad_cols(w):
    return jnp.pad(w, ((0, 0), (0, LORA_PAD - w.shape[1])))


def _pad_rows(w):
    return jnp.pad(w, ((0, LORA_PAD - w.shape[0]), (0, 0)))


def kernel(x_prompt, x_sample, state_pool, cache_win_k, cache_win_v, state_shift, state_wkv, norm_ffn1, norm_mix, norm_ffn2, norm_final, ffn_w_gate, ffn_w_up, ffn_w_down, pool_w, pool_scale, att_w_qkv, att_b_qkv, att_w_o, att_b_o, att_sinks, rel_bias, rwkv_mu, rwkv_w_r, rwkv_w_k, rwkv_w_v, rwkv_w_o, rwkv_w0, rwkv_w1, rwkv_w2, rwkv_a0, rwkv_a1, rwkv_a2, rwkv_g1, rwkv_g2, rwkv_k_k, rwkv_k_a, rwkv_r_k, rwkv_ln_w, rwkv_ln_b):
    d = D_MODEL
    x = jnp.concatenate([x_prompt.reshape(N_PROMPT, d), x_sample.reshape(N_SAMPLE, d)], axis=0)
    wg, wu, wd = (w.astype(BF16) for w in (ffn_w_gate, ffn_w_up, ffn_w_down))
    pool_p, pool_s, wk_p, wv_p, wk_s, wv_s, sh_p, sh_s, wkv_p, wkv_s = ([] for _ in range(10))

    for l in range(DEPTH):
        j, kind = divmod(l, N_MIXERS)
        x = _ffn(x, norm_ffn1[l], wg, wu, wd, l, 0)
        gain = norm_mix[l]
        if kind == 0:
            w_pool = pool_w[j].astype(BF16)
            u_s = _norm_rows(x, gain, tm=N_SAMPLE, row0=N_PROMPT, rows=N_SAMPLE).reshape(DEC_BATCH, DEC_SEQ, d)
            ext = jnp.concatenate([jnp.zeros((DEC_BATCH, 1, d), F32), state_pool[j], u_s], axis=1)
            pool_p.append(_norm_tails(x, gain)[:, -POOL_BUF:])
            pool_s.append(ext[:, -POOL_BUF:])
            x = _pool_prompt(x, gain, w_pool, pool_scale[j])
            x = _pool_sample(ext, x, w_pool, pool_scale[j])
        elif kind == 1:
            qkv = _matmul(x, att_w_qkv[j].astype(BF16), att_b_qkv[j], tm=TM_TOK, tn=QKV_DIM // 5, gain=gain)
            k_buf = cache_win_k[j].reshape(DEC_BATCH, WINDOW, KV_DIM)
            v_buf = cache_win_v[j].reshape(DEC_BATCH, WINDOW, KV_DIM)
            o_p = _attention_prompt(qkv, rel_bias, att_sinks[j])
            o_s = _attention_sample(qkv, rel_bias, att_sinks[j], k_buf, v_buf)
            w_o = att_w_o[j].astype(BF16)
            x = _matmul(o_p, w_o, att_b_o[j], x, tm=TM_PROMPT, tn=d, res_row0=0)
            x = _matmul(o_s, w_o, att_b_o[j], x, tm=N_SAMPLE, tn=d, res_row0=N_PROMPT)
            kv_shape = (WINDOW, N_KV_HEADS, HEAD_DIM)
            k_cols = slice(ATT_DIM, ATT_DIM + KV_DIM)
            v_cols = slice(ATT_DIM + KV_DIM, QKV_DIM)
            tails = [slice((b + 1) * SEQ - WINDOW, (b + 1) * SEQ) for b in range(BATCH)]
            wk_p.append(jnp.stack([qkv[t, k_cols] for t in tails]).reshape(BATCH, *kv_shape))
            wv_p.append(jnp.stack([qkv[t, v_cols] for t in tails]).reshape(BATCH, *kv_shape))
            new_rows = slice(N_PROMPT, N_TOK)
            k_s = jnp.concatenate([k_buf, qkv[new_rows, k_cols].reshape(DEC_BATCH, DEC_SEQ, KV_DIM)], axis=1)
            v_s = jnp.concatenate([v_buf, qkv[new_rows, v_cols].reshape(DEC_BATCH, DEC_SEQ, KV_DIM)], axis=1)
            wk_s.append(k_s[:, -WINDOW:].reshape(DEC_BATCH, *kv_shape))
            wv_s.append(v_s[:, -WINDOW:].reshape(DEC_BATCH, *kv_shape))
        else:
            weights = (rwkv_mu[j],
                       rwkv_w_r[j].astype(BF16), rwkv_w_k[j].astype(BF16), rwkv_w_v[j].astype(BF16),
                       _pad_cols(rwkv_w1[j]).astype(BF16), _pad_rows(rwkv_w2[j]).astype(BF16), rwkv_w0[j],
                       _pad_cols(rwkv_a1[j]).astype(BF16), _pad_rows(rwkv_a2[j]).astype(BF16), rwkv_a0[j],
                       rwkv_g1[j].astype(BF16), rwkv_g2[j].astype(BF16))
            first_s = jnp.repeat(state_shift[j], DEC_SEQ, axis=0)
            proj_p = _rwkv_proj(x, gain, weights, row0=0, rows=N_PROMPT, tm=TM_PROMPT, seq_len=SEQ)
            proj_s = _rwkv_proj(x, gain, weights, row0=N_PROMPT, rows=N_SAMPLE, tm=N_SAMPLE, seq_len=DEC_SEQ,
                                first=first_s)
            head_params = (rwkv_k_k[j], rwkv_k_a[j], rwkv_r_k[j], rwkv_ln_w[j], rwkv_ln_b[j])
            z_p, s_p = _rwkv_scan(proj_p, *head_params, n_seq=BATCH, chunk=SCAN_CHUNK, n_chunks=SEQ // SCAN_CHUNK)
            z_s, s_s = _rwkv_scan(proj_s, *head_params, n_seq=DEC_BATCH, chunk=DEC_SEQ, n_chunks=1,
                                  s0=state_wkv[j])
            sh_p.append(_norm_tails(x, gain)[:, -1])
            sh_s.append(_norm_rows(x, gain, tm=N_SAMPLE, row0=N_PROMPT, rows=N_SAMPLE)
                        .reshape(DEC_BATCH, DEC_SEQ, d)[:, -1])
            w_o = rwkv_w_o[j].astype(BF16)
            x = _matmul(z_p, w_o, None, x, tm=TM_PROMPT, tn=d, res_row0=0)
            x = _matmul(z_s, w_o, None, x, tm=N_SAMPLE, tn=d, res_row0=N_PROMPT)
            wkv_p.append(s_p)
            wkv_s.append(s_s)
        x = _ffn(x, norm_ffn2[l], wg, wu, wd, l, 1)

    y_p = _norm_rows(x, norm_final, tm=TM_PROMPT, row0=0, rows=N_PROMPT)
    y_s = _norm_rows(x, norm_final, tm=N_SAMPLE, row0=N_PROMPT, rows=N_SAMPLE)
    return (y_p.reshape(BATCH, SEQ, d), y_s.reshape(DEC_BATCH, DEC_SEQ, d),
            jnp.stack(pool_p), jnp.stack(pool_s),
            jnp.stack(wk_p), jnp.stack(wv_p), jnp.stack(wk_s), jnp.stack(wv_s),
            jnp.stack(sh_p), jnp.stack(sh_s),
            jnp.stack(wkv_p), jnp.stack(wkv_s))
```

```python
import functools
import math

import numpy as np
import jax
import jax.numpy as jnp
from jax import lax
from jax.experimental import pallas as pl
from jax.experimental.pallas import tpu as pltpu

F32 = jnp.float32
BF16 = jnp.bfloat16

D_MODEL = 2048
BATCH = 2
SEQ = 4096
DEPTH = 4
DEC_BATCH = 32
DEC_SEQ = 8
PAST_LEN = 16384
N_MIXERS = 3
RMS_EPS = 1e-6
D_FF = 5632
POOL_WINDOWS = (2, 4, 8, 16)
POOL_GROUPS = 4
POOL_GROUP_DIM = D_MODEL // POOL_GROUPS
POOL_BUF = max(POOL_WINDOWS) - 1
POOL_HALO = POOL_BUF + 1
HEAD_DIM = 64
N_HEADS = D_MODEL // HEAD_DIM
N_KV_HEADS = 4
GQA_GROUP = N_HEADS // N_KV_HEADS
ATT_DIM = N_HEADS * HEAD_DIM
KV_DIM = N_KV_HEADS * HEAD_DIM
QKV_DIM = ATT_DIM + 2 * KV_DIM
WINDOW = 128
ATT_SCALE = HEAD_DIM ** -0.5
T5_BUCKETS = 32
T5_MAX_DISTANCE = 128
NEG_INF = -1e30
RWKV_N = 64
RWKV_HEADS = D_MODEL // RWKV_N
GN_EPS = 64e-5
LORA_PAD = 128

N_PROMPT = BATCH * SEQ
N_SAMPLE = DEC_BATCH * DEC_SEQ
N_TOK = N_PROMPT + N_SAMPLE

VMEM_LIMIT_BYTES = 56 * 1024 * 1024

TM_TOK = 704
TM_PROMPT = 512
TM_POOL = 256
SHIFT_HALO = 8
TF_FFN = 512
CAST_BLOCK = 512
CAST_BLOCKS = (D_MODEL // CAST_BLOCK) * (D_FF // CAST_BLOCK)
TN_PROJ = 512
SCAN_CHUNK = 64
SCAN_HEADS = 2
SCAN_LANES = SCAN_HEADS * RWKV_N


def _params(n_axes):
    return pltpu.CompilerParams(dimension_semantics=("arbitrary",) * n_axes,
                                vmem_limit_bytes=VMEM_LIMIT_BYTES)


def _dot(a, b):
    return jnp.dot(a, b, preferred_element_type=F32)


def _dot_nt(a, b):
    return lax.dot_general(a, b, (((1,), (1,)), ((), ())), preferred_element_type=F32)


def _dot_tn(a, b):
    return lax.dot_general(a, b, (((0,), (0,)), ((), ())), preferred_element_type=F32)


def _round_robin(gens):
    live = list(gens)
    while live:
        still = []
        for gen in live:
            try:
                next(gen)
                still.append(gen)
            except StopIteration:
                pass
        live = still


def _rms_norm(x, g):
    return x * lax.rsqrt(jnp.mean(x * x, axis=-1, keepdims=True) + RMS_EPS) * g


def _norm_kernel(x_ref, g_ref, o_ref):
    o_ref[...] = _rms_norm(x_ref[...], g_ref[...])


def _norm_rows(x, g, *, tm, row0, rows):
    d = x.shape[1]
    off = row0 // tm
    return pl.pallas_call(
        _norm_kernel,
        grid=(rows // tm,),
        in_specs=[pl.BlockSpec((tm, d), lambda i: (i + off, 0)),
                  pl.BlockSpec((1, d), lambda i: (0, 0))],
        out_specs=pl.BlockSpec((tm, d), lambda i: (i, 0)),
        out_shape=jax.ShapeDtypeStruct((rows, d), F32),
        compiler_params=_params(1),
        name="rms_norm",
    )(x, g.reshape(1, d))


def _norm_tails(x, g):
    d = x.shape[1]
    blocks_per_seq = SEQ // POOL_HALO
    out = pl.pallas_call(
        _norm_kernel,
        grid=(BATCH,),
        in_specs=[pl.BlockSpec((POOL_HALO, d), lambda b: ((b + 1) * blocks_per_seq - 1, 0)),
                  pl.BlockSpec((1, d), lambda b: (0, 0))],
        out_specs=pl.BlockSpec((POOL_HALO, d), lambda b: (b, 0)),
        out_shape=jax.ShapeDtypeStruct((BATCH * POOL_HALO, d), F32),
        compiler_params=_params(1),
        name="rms_norm_tails",
    )(x, g.reshape(1, d))
    return out.reshape(BATCH, POOL_HALO, d)


def _ffn_kernel(*refs, cast_next):
    if cast_next:
        (x_ref, g_ref, wg_ref, wu_ref, wd_ref, ng_ref, nu_ref, nd_ref,
         o_ref, cg_ref, cu_ref, cd_ref, xn_ref) = refs
    else:
        x_ref, g_ref, wg_ref, wu_ref, wd_ref, o_ref, xn_ref = refs
    i, j = pl.program_id(0), pl.program_id(1)

    @pl.when(j == 0)
    def _():
        xn_ref[...] = _rms_norm(x_ref[...], g_ref[...]).astype(BF16)
        o_ref[...] = jnp.zeros_like(o_ref)

    xn = xn_ref[...]
    gate = _dot(xn, wg_ref[...])
    up = _dot(xn, wu_ref[...])
    h = (gate * jax.nn.sigmoid(gate) * up).astype(BF16)
    o_ref[...] += _dot(h, wd_ref[...])

    @pl.when(j == pl.num_programs(1) - 1)
    def _():
        o_ref[...] = x_ref[...] + 0.5 * o_ref[...]

    if cast_next:
        step = i * pl.num_programs(1) + j
        for k, (src, dst) in enumerate(((ng_ref, cg_ref), (nu_ref, cu_ref), (nd_ref, cd_ref))):
            @pl.when(jnp.logical_and(step >= k * CAST_BLOCKS, step < (k + 1) * CAST_BLOCKS))
            def _(src=src, dst=dst):
                dst[...] = src[...].astype(BF16)


def _ffn(x, g, w_bf, w_next=None):
    m, d = x.shape
    tm, tf, cb = TM_TOK, TF_FFN, CAST_BLOCK
    grid = (m // tm, D_FF // tf)
    in_specs = [pl.BlockSpec((tm, d), lambda i, j: (i, 0)),
                pl.BlockSpec((1, d), lambda i, j: (0, 0)),
                pl.BlockSpec((d, tf), lambda i, j: (0, j)),
                pl.BlockSpec((d, tf), lambda i, j: (0, j)),
                pl.BlockSpec((tf, d), lambda i, j: (j, 0))]
    out_specs = [pl.BlockSpec((tm, d), lambda i, j: (i, 0))]
    out_shape = [jax.ShapeDtypeStruct((m, d), F32)]
    args = [x, g.reshape(1, d), *w_bf]
    if w_next is not None:
        assert grid[0] * grid[1] == 3 * CAST_BLOCKS
        (ng, nu, nd), layer, half = w_next

        def block(k, n_col_blocks):
            def index(i, j):
                b = jnp.clip(i * grid[1] + j - k * CAST_BLOCKS, 0, CAST_BLOCKS - 1)
                return b // n_col_blocks, b % n_col_blocks
            return index

        for k, w in enumerate((ng, nu, nd)):
            idx = block(k, w.shape[3] // cb)
            in_specs.append(pl.BlockSpec((None, None, cb, cb),
                                         lambda i, j, idx=idx: (layer, half, *idx(i, j))))
            out_specs.append(pl.BlockSpec((cb, cb), idx))
            out_shape.append(jax.ShapeDtypeStruct(w.shape[2:], BF16))
            args.append(w)
    out = pl.pallas_call(
        functools.partial(_ffn_kernel, cast_next=w_next is not None),
        grid=grid,
        in_specs=in_specs,
        out_specs=out_specs,
        out_shape=out_shape,
        scratch_shapes=[pltpu.VMEM((tm, d), BF16)],
        compiler_params=_params(2),
        name="ffn_half_step",
    )(*args)
    return (out[0], tuple(out[1:])) if w_next is not None else out[0]


def _matmul_kernel(*refs, has_gain, has_bias, has_res):
    refs = list(refs)
    lhs_ref, w_ref = refs[0], refs[1]
    pos = 2
    g_ref = b_ref = res_ref = None
    if has_gain:
        g_ref = refs[pos]
        pos += 1
    if has_bias:
        b_ref = refs[pos]
        pos += 1
    if has_res:
        res_ref = refs[pos]
        pos += 1
    o_ref, lhs_bf_ref = refs[pos], refs[pos + 1]

    @pl.when(pl.program_id(1) == 0)
    def _():
        lhs = lhs_ref[...]
        if has_gain:
            lhs = _rms_norm(lhs, g_ref[...])
        lhs_bf_ref[...] = lhs.astype(BF16)

    acc = _dot(lhs_bf_ref[...], w_ref[...])
    if has_bias:
        acc = acc + b_ref[...]
    if has_res:
        acc = res_ref[...] + acc
    o_ref[...] = acc


def _matmul(lhs, w, bias=None, res=None, *, tm, tn, res_row0=0, gain=None):
    m, k = lhs.shape
    n = w.shape[1]
    off = res_row0 // tm
    in_specs = [pl.BlockSpec((tm, k), lambda i, j: (i, 0)),
                pl.BlockSpec((k, tn), lambda i, j: (0, j))]
    args = [lhs, w]
    if gain is not None:
        in_specs.append(pl.BlockSpec((1, k), lambda i, j: (0, 0)))
        args.append(gain.reshape(1, k))
    if bias is not None:
        in_specs.append(pl.BlockSpec((1, tn), lambda i, j: (0, j)))
        args.append(bias.reshape(1, n))
    aliases = {}
    if res is not None:
        in_specs.append(pl.BlockSpec((tm, tn), lambda i, j: (i + off, j)))
        aliases = {len(args): 0}
        args.append(res)
        out_shape = jax.ShapeDtypeStruct(res.shape, F32)
    else:
        out_shape = jax.ShapeDtypeStruct((m, n), F32)
    return pl.pallas_call(
        functools.partial(_matmul_kernel, has_gain=gain is not None, has_bias=bias is not None,
                          has_res=res is not None),
        grid=(m // tm, n // tn),
        in_specs=in_specs,
        out_specs=pl.BlockSpec((tm, tn), lambda i, j: (i + off, j)),
        out_shape=out_shape,
        scratch_shapes=[pltpu.VMEM((tm, k), BF16)],
        input_output_aliases=aliases,
        compiler_params=_params(2),
        name="matmul_bias_residual",
    )(*args)


def _pool_group_out(diff, gi, x_ref, w_ref, sc_ref, o_ref):
    c = slice(gi * POOL_GROUP_DIM, (gi + 1) * POOL_GROUP_DIM)
    out = _dot(diff.astype(BF16), w_ref[gi])
    o_ref[:, c] = x_ref[:, c] + out * sc_ref[:, c]


def _pool_prompt_kernel(x_ref, halo_ref, g_ref, w_ref, sc_ref, o_ref, ext_ref, *, tm, tiles_per_seq, n_tiles):
    i = pl.program_id(0)

    @pl.when(i < n_tiles)
    def _():
        t_in_seq = i % tiles_per_seq
        g = g_ref[...]
        ext_ref[0:POOL_HALO, :] = jnp.where(t_in_seq == 0, 0.0, _rms_norm(halo_ref[...], g))
        ext_ref[POOL_HALO:, :] = _rms_norm(x_ref[...], g)
        pos = t_in_seq * tm + lax.broadcasted_iota(jnp.int32, (tm, 1), 0)
        for gi, w in enumerate(POOL_WINDOWS):
            c = slice(gi * POOL_GROUP_DIM, (gi + 1) * POOL_GROUP_DIM)
            cur = ext_ref[pl.ds(POOL_HALO, tm), c]
            s = cur
            for back in range(1, w):
                s = s + ext_ref[pl.ds(POOL_HALO - back, tm), c]
            cnt = jnp.minimum(w, pos + 1).astype(F32)
            _pool_group_out(s / cnt - cur, gi, x_ref, w_ref, sc_ref, o_ref)

    @pl.when(i >= n_tiles)
    def _():
        o_ref[...] = x_ref[...]


def _pool_prompt(x, gain, w_pool, scale):
    d = x.shape[1]
    tm = TM_POOL
    halo_blocks = tm // POOL_HALO
    return pl.pallas_call(
        functools.partial(_pool_prompt_kernel, tm=tm, tiles_per_seq=SEQ // tm, n_tiles=N_PROMPT // tm),
        grid=(N_TOK // tm,),
        in_specs=[pl.BlockSpec((tm, d), lambda i: (i, 0)),
                  pl.BlockSpec((POOL_HALO, d), lambda i: (jnp.maximum(i * halo_blocks - 1, 0), 0)),
                  pl.BlockSpec((1, d), lambda i: (0, 0)),
                  pl.BlockSpec((POOL_GROUPS, POOL_GROUP_DIM, POOL_GROUP_DIM), lambda i: (0, 0, 0)),
                  pl.BlockSpec((1, d), lambda i: (0, 0))],
        out_specs=pl.BlockSpec((tm, d), lambda i: (i, 0)),
        out_shape=jax.ShapeDtypeStruct(x.shape, F32),
        scratch_shapes=[pltpu.VMEM((tm + POOL_HALO, d), F32)],
        compiler_params=_params(1),
        name="pool_prompt",
    )(x, x, gain.reshape(1, d), w_pool, scale.reshape(1, d))


def _pool_sample_kernel(ext_ref, x_ref, w_ref, sc_ref, o_ref):
    for gi, w in enumerate(POOL_WINDOWS):
        c = slice(gi * POOL_GROUP_DIM, (gi + 1) * POOL_GROUP_DIM)
        cur = ext_ref[:, pl.ds(POOL_HALO, DEC_SEQ), c]
        s = cur
        for back in range(1, w):
            s = s + ext_ref[:, pl.ds(POOL_HALO - back, DEC_SEQ), c]
        cnt = min(w, PAST_LEN + 1)
        diff = (s / float(cnt) - cur).reshape(N_SAMPLE, POOL_GROUP_DIM)
        _pool_group_out(diff, gi, x_ref, w_ref, sc_ref, o_ref)


def _pool_sample(ext, x, w_pool, scale):
    d = x.shape[1]
    blk = N_PROMPT // N_SAMPLE
    return pl.pallas_call(
        _pool_sample_kernel,
        grid=(1,),
        in_specs=[pl.BlockSpec(ext.shape, lambda i: (0, 0, 0)),
                  pl.BlockSpec((N_SAMPLE, d), lambda i: (blk, 0)),
                  pl.BlockSpec((POOL_GROUPS, POOL_GROUP_DIM, POOL_GROUP_DIM), lambda i: (0, 0, 0)),
                  pl.BlockSpec((1, d), lambda i: (0, 0))],
        out_specs=pl.BlockSpec((N_SAMPLE, d), lambda i: (blk, 0)),
        out_shape=jax.ShapeDtypeStruct(x.shape, F32),
        input_output_aliases={1: 0},
        compiler_params=_params(1),
        name="pool_sample",
    )(ext, x, w_pool, scale.reshape(1, d))


def _t5_bucket_table(tq):
    qi = np.arange(tq)[:, None]
    kj = np.arange(WINDOW + tq)[None, :]
    dist = qi + WINDOW - kj
    exact = T5_BUCKETS // 2
    ratio = np.log(np.maximum(dist, 1) / exact) / math.log(T5_MAX_DISTANCE / exact)
    large = np.minimum(exact + (ratio * (T5_BUCKETS - exact)).astype(np.int64), T5_BUCKETS - 1)
    bucket = np.where(dist < exact, dist, large)
    valid = (dist >= 0) & (dist < WINDOW)
    table = np.where(valid, bucket, -1).astype(np.int32)
    return table[:, :WINDOW], table[:, WINDOW:]


def _attn_sample_kernel(rb_ref, sink_ref, bkt_p_ref, bkt_c_ref, q_ref, kc_ref, vc_ref, kp_ref, vp_ref,
                        o_ref, bias_p_ref, bias_c_ref, sink_col_ref, *, tq):
    step = pl.program_id(0)

    @pl.when(step == 0)
    def _():
        bkt_p = bkt_p_ref[...]
        bkt_c = bkt_c_ref[...]
        bias_p_ref[...] = jnp.zeros_like(bias_p_ref)
        bias_c_ref[...] = jnp.zeros_like(bias_c_ref)

        def add_bucket(b, carry):
            eq_p = bkt_p == b
            eq_c = bkt_c == b
            for h in range(N_HEADS):
                kvh, g = divmod(h, GQA_GROUP)
                r = slice(g * tq, (g + 1) * tq)
                val = rb_ref[b, h]
                bias_p_ref[kvh, r, :] += jnp.where(eq_p, val, 0.0)
                bias_c_ref[kvh, r, :] += jnp.where(eq_c, val, 0.0)
            return carry

        lax.fori_loop(0, T5_BUCKETS, add_bucket, 0)
        for h in range(N_HEADS):
            kvh, g = divmod(h, GQA_GROUP)
            r = slice(g * tq, (g + 1) * tq)
            bias_p_ref[kvh, r, :] = jnp.where(bkt_p < 0, NEG_INF, bias_p_ref[kvh, r, :])
            bias_c_ref[kvh, r, :] = jnp.where(bkt_c < 0, NEG_INF, bias_c_ref[kvh, r, :])
            sink_col_ref[kvh, r, :] = jnp.full((tq, 1), sink_ref[h], F32)

    def kv_head(kvh):
        heads = [kvh * GQA_GROUP + g for g in range(GQA_GROUP)]
        qs = jnp.concatenate([q_ref[:, h * HEAD_DIM:(h + 1) * HEAD_DIM] for h in heads], axis=0).astype(BF16)
        c = slice(kvh * HEAD_DIM, (kvh + 1) * HEAD_DIM)
        qk_p = _dot_nt(qs, kp_ref[:, c].astype(BF16))
        qk_c = _dot_nt(qs, kc_ref[:, c].astype(BF16))
        yield
        s_p = qk_p * ATT_SCALE + bias_p_ref[kvh]
        s_c = qk_c * ATT_SCALE + bias_c_ref[kvh]
        sink = sink_col_ref[kvh]
        m = jnp.maximum(jnp.maximum(jnp.max(s_p, axis=-1, keepdims=True),
                                    jnp.max(s_c, axis=-1, keepdims=True)), sink)
        p_p = jnp.exp(s_p - m)
        p_c = jnp.exp(s_c - m)
        den = (jnp.sum(p_p, axis=-1, keepdims=True) + jnp.sum(p_c, axis=-1, keepdims=True)
               + jnp.exp(sink - m))
        o_p = _dot(p_p.astype(BF16), vp_ref[:, c].astype(BF16))
        o_c = _dot(p_c.astype(BF16), vc_ref[:, c].astype(BF16))
        yield
        o = (o_p + o_c) / den
        for g, h in enumerate(heads):
            o_ref[:, h * HEAD_DIM:(h + 1) * HEAD_DIM] = o[g * tq:(g + 1) * tq, :]

    _round_robin([kv_head(kvh) for kvh in range(N_KV_HEADS)])


def _attention_sample(qkv, rel_bias, sinks, k_prev, v_prev):
    tq = DEC_SEQ
    bkt_p, bkt_c = _t5_bucket_table(tq)
    off = N_PROMPT // tq
    kcol = ATT_DIM // KV_DIM
    smem = pl.BlockSpec(memory_space=pltpu.SMEM)
    cache = pl.BlockSpec((None, WINDOW, KV_DIM), lambda i: (i, 0, 0))
    rows = GQA_GROUP * tq
    return pl.pallas_call(
        functools.partial(_attn_sample_kernel, tq=tq),
        grid=(DEC_BATCH,),
        in_specs=[smem, smem,
                  pl.BlockSpec((tq, WINDOW), lambda i: (0, 0)),
                  pl.BlockSpec((tq, tq), lambda i: (0, 0)),
                  pl.BlockSpec((tq, ATT_DIM), lambda i: (i + off, 0)),
                  pl.BlockSpec((tq, KV_DIM), lambda i: (i + off, kcol)),
                  pl.BlockSpec((tq, KV_DIM), lambda i: (i + off, kcol + 1)),
                  cache, cache],
        out_specs=pl.BlockSpec((tq, ATT_DIM), lambda i: (i, 0)),
        out_shape=jax.ShapeDtypeStruct((N_SAMPLE, ATT_DIM), F32),
        scratch_shapes=[pltpu.VMEM((N_KV_HEADS, rows, WINDOW), F32),
                        pltpu.VMEM((N_KV_HEADS, rows, tq), F32),
                        pltpu.VMEM((N_KV_HEADS, rows, 1), F32)],
        compiler_params=_params(1),
        name="swa_attention_sample",
    )(rel_bias, sinks, jnp.asarray(bkt_p), jnp.asarray(bkt_c), qkv, qkv, qkv, k_prev, v_prev)


def _attn_prompt_kernel(rb_ref, sink_ref, bkt_ref, q_ref, kc_ref, vc_ref, kp_ref, vp_ref, o_ref, bias_ref,
                        *, blocks_per_seq):
    step = pl.program_id(0)
    tq, nk, pair = WINDOW, 2 * WINDOW, 2 * HEAD_DIM

    @pl.when(step == 0)
    def _():
        bkt = bkt_ref[...]
        bias_ref[...] = jnp.zeros_like(bias_ref)

        def add_bucket(b, carry):
            eq = bkt == b
            for h in range(N_HEADS):
                bias_ref[h] += jnp.where(eq, rb_ref[b, h], 0.0)
            return carry

        lax.fori_loop(0, T5_BUCKETS, add_bucket, 0)
        for h in range(N_HEADS):
            bias_ref[h] = jnp.where(bkt < 0, NEG_INF, bias_ref[h])

    no_prev = step % blocks_per_seq == 0
    dead = jnp.logical_and(no_prev, lax.broadcasted_iota(jnp.int32, (nk, tq), 0) < WINDOW)
    lane_half = lax.broadcasted_iota(jnp.int32, (nk, pair), 1) // HEAD_DIM
    row_half = lax.broadcasted_iota(jnp.int32, (pair, tq), 0) // HEAD_DIM

    for kvh in range(N_KV_HEADS):
        tile = slice((kvh // 2) * pair, (kvh // 2 + 1) * pair)
        half = kvh % 2

        def both_halves(prev_ref, cur_ref):
            x = jnp.concatenate([prev_ref[:, tile], cur_ref[:, tile]], axis=0)
            own = jnp.where(lane_half == half, x, 0.0)
            other = pltpu.roll(own, HEAD_DIM, axis=1)
            return (own, other) if half == 0 else (other, own)

        k_lo, k_hi = both_halves(kp_ref, kc_ref)
        v_lo, v_hi = both_halves(vp_ref, vc_ref)
        k2 = jnp.concatenate([k_lo, k_hi], axis=0).astype(BF16)
        v_lo_t = v_lo.T.astype(BF16)
        v_hi_t = v_hi.T.astype(BF16)

        def head_pair(j):
            cols = slice(j * pair, (j + 1) * pair)
            qp = (q_ref[:, cols] * ATT_SCALE).astype(BF16)
            s2 = _dot_nt(k2, qp)
            yield
            probs, dens = [], []
            for t in range(2):
                h = 2 * j + t
                s = s2[t * nk:(t + 1) * nk] + bias_ref[h]
                s = jnp.where(dead, NEG_INF, s)
                sink = sink_ref[h]
                m = jnp.maximum(jnp.max(s, axis=0, keepdims=True), sink)
                p = jnp.exp(s - m)
                dens.append(jnp.sum(p, axis=0, keepdims=True) + jnp.exp(sink - m))
                probs.append(p.astype(BF16))
            o2 = _dot(v_lo_t, probs[0]) + _dot(v_hi_t, probs[1])
            yield
            o2 = o2 / jnp.where(row_half == 0, dens[0], dens[1])
            o_ref[:, cols] = o2.T

        pairs_per_kv = GQA_GROUP // 2
        _round_robin([head_pair(kvh * pairs_per_kv + jj) for jj in range(pairs_per_kv)])


def _attention_prompt(qkv, rel_bias, sinks):
    bkt_p, bkt_c = _t5_bucket_table(WINDOW)
    bkt_t = np.ascontiguousarray(np.concatenate([bkt_p, bkt_c], axis=1).T)
    kcol = ATT_DIM // KV_DIM
    smem = pl.BlockSpec(memory_space=pltpu.SMEM)
    prev = lambda i: jnp.maximum(i - 1, 0)
    return pl.pallas_call(
        functools.partial(_attn_prompt_kernel, blocks_per_seq=SEQ // WINDOW),
        grid=(N_PROMPT // WINDOW,),
        in_specs=[smem, smem,
                  pl.BlockSpec((2 * WINDOW, WINDOW), lambda i: (0, 0)),
                  pl.BlockSpec((WINDOW, ATT_DIM), lambda i: (i, 0)),
                  pl.BlockSpec((WINDOW, KV_DIM), lambda i: (i, kcol)),
                  pl.BlockSpec((WINDOW, KV_DIM), lambda i: (i, kcol + 1)),
                  pl.BlockSpec((WINDOW, KV_DIM), lambda i: (prev(i), kcol)),
                  pl.BlockSpec((WINDOW, KV_DIM), lambda i: (prev(i), kcol + 1))],
        out_specs=pl.BlockSpec((WINDOW, ATT_DIM), lambda i: (i, 0)),
        out_shape=jax.ShapeDtypeStruct((N_PROMPT, ATT_DIM), F32),
        scratch_shapes=[pltpu.VMEM((N_HEADS, 2 * WINDOW, WINDOW), F32)],
        compiler_params=_params(1),
        name="swa_attention_prompt",
    )(rel_bias, sinks, jnp.asarray(bkt_t), qkv, qkv, qkv, qkv, qkv)


def _rwkv_proj_kernel(*refs, tm, seq_len, has_first):
    refs = list(refs)
    x_ref, halo_ref, gain_ref = refs[:3]
    pos = 3
    first_ref = None
    if has_first:
        first_ref = refs[pos]
        pos += 1
    (mu_ref, wr_ref, wk_ref, wv_ref, w1_ref, w2_ref, w0_ref, a1_ref, a2_ref, a0_ref, g1_ref, g2_ref,
     r_ref, k_ref, v_ref, ld_ref, a_ref, g_ref,
     ext_ref, xr_ref, xk_ref, xv_ref, hw_ref, ha_ref, hg_ref) = refs[pos:]

    @pl.when(pl.program_id(1) == 0)
    def _():
        gain = gain_ref[...]
        u = _rms_norm(x_ref[...], gain)
        ext_ref[0:SHIFT_HALO, :] = _rms_norm(halo_ref[...], gain)
        ext_ref[SHIFT_HALO:, :] = u
        prev = ext_ref[pl.ds(SHIFT_HALO - 1, tm), :]
        row = pl.program_id(0) * tm + lax.broadcasted_iota(jnp.int32, (tm, 1), 0)
        starts = row % seq_len == 0
        prev = jnp.where(starts, first_ref[...] if has_first else 0.0, prev)
        dx = prev - u
        mix = lambda i: (u + dx * mu_ref[i:i + 1, :]).astype(BF16)
        xr_ref[...] = mix(0)
        hw_ref[...] = jnp.tanh(_dot(mix(1), w1_ref[...])).astype(BF16)
        xk_ref[...] = mix(2)
        xv_ref[...] = mix(3)
        ha_ref[...] = _dot(mix(4), a1_ref[...]).astype(BF16)
        hg_ref[...] = jax.nn.sigmoid(_dot(mix(5), g1_ref[...])).astype(BF16)

    r_ref[...] = _dot(xr_ref[...], wr_ref[...])
    k_ref[...] = _dot(xk_ref[...], wk_ref[...])
    v_ref[...] = _dot(xv_ref[...], wv_ref[...])
    z = -(w0_ref[...] + _dot(hw_ref[...], w2_ref[...]))
    softplus = jnp.maximum(z, 0.0) + jnp.log1p(jnp.exp(-jnp.abs(z)))
    ld_ref[...] = -jnp.exp(-softplus - 0.5)
    a_ref[...] = jax.nn.sigmoid(a0_ref[...] + _dot(ha_ref[...], a2_ref[...]))
    g_ref[...] = _dot(hg_ref[...], g2_ref[...])


def _rwkv_proj(x, gain, weights, *, row0, rows, tm, seq_len, first=None):
    mu, w_r, w_k, w_v, w1, w2, w0, a1, a2, a0, g1, g2 = weights
    d = x.shape[1]
    tn = TN_PROJ
    gl = g1.shape[1]
    off = row0 // tm
    halo_blocks = tm // SHIFT_HALO
    halo_off = row0 // SHIFT_HALO
    row = lambda i, j: (i + off, 0)
    col = lambda i, j: (0, j)
    fixed = lambda i, j: (0, 0)
    in_specs = [pl.BlockSpec((tm, d), row),
                pl.BlockSpec((SHIFT_HALO, d), lambda i, j: (jnp.maximum(halo_off + i * halo_blocks - 1, 0), 0)),
                pl.BlockSpec((1, d), fixed)]
    args = [x, x, gain.reshape(1, d)]
    if first is not None:
        in_specs.append(pl.BlockSpec((tm, d), lambda i, j: (i, 0)))
        args.append(first)
    in_specs += [pl.BlockSpec((6, d), fixed),
                 pl.BlockSpec((d, tn), col), pl.BlockSpec((d, tn), col), pl.BlockSpec((d, tn), col),
                 pl.BlockSpec((d, LORA_PAD), fixed), pl.BlockSpec((LORA_PAD, tn), col), pl.BlockSpec((1, tn), col),
                 pl.BlockSpec((d, LORA_PAD), fixed), pl.BlockSpec((LORA_PAD, tn), col), pl.BlockSpec((1, tn), col),
                 pl.BlockSpec((d, gl), fixed), pl.BlockSpec((gl, tn), col)]
    args += [mu, w_r, w_k, w_v, w1, w2, w0.reshape(1, d), a1, a2, a0.reshape(1, d), g1, g2]
    out = jax.ShapeDtypeStruct((rows, d), F32)
    return pl.pallas_call(
        functools.partial(_rwkv_proj_kernel, tm=tm, seq_len=seq_len, has_first=first is not None),
        grid=(rows // tm, d // tn),
        in_specs=in_specs,
        out_specs=[pl.BlockSpec((tm, tn), lambda i, j: (i, j))] * 6,
        out_shape=[out] * 6,
        scratch_shapes=[pltpu.VMEM((tm + SHIFT_HALO, d), F32)] + [pltpu.VMEM((tm, d), BF16)] * 3
        + [pltpu.VMEM((tm, LORA_PAD), BF16)] * 2 + [pltpu.VMEM((tm, gl), BF16)],
        compiler_params=_params(2),
        name="rwkv_projections",
    )(*args)


def _scan_kernel(*refs, chunk, has_init):
    (r_ref, k_ref, v_ref, ld_ref, a_ref, g_ref, kk_ref, ka_ref, rk_ref, lnw_ref, lnb_ref) = refs[:11]
    refs = refs[11:]
    if has_init:
        s0_ref, refs = refs[0], refs[1:]
    z_ref, sout_ref, s_ref = refs
    c_idx = pl.program_id(1)
    cc, n, w = chunk, RWKV_N, SCAN_LANES
    hc = SCAN_HEADS * cc
    n_groups = RWKV_HEADS // SCAN_HEADS

    @pl.when(c_idx == 0)
    def _():
        s_ref[...] = jnp.zeros_like(s_ref)
        if has_init:
            for h in range(RWKV_HEADS):
                gi, hh = divmod(h, SCAN_HEADS)
                s_ref[gi, hh * n:(hh + 1) * n, hh * n:(hh + 1) * n] = s0_ref[h]

    def iota(shape, axis):
        return lax.broadcasted_iota(jnp.int32, shape, axis)

    head_diag = iota((w, w), 0) // n == iota((w, w), 1) // n
    tril = (iota((cc, cc), 0) >= iota((cc, cc), 1)).astype(BF16)
    t_row = iota((cc, hc), 0)
    s_col = iota((cc, hc), 1) % cc
    strict = t_row > s_col
    causal = t_row >= s_col
    lanes_diag = iota((hc, w), 0) // cc == iota((hc, w), 1) // n
    tokens_diag = iota((hc, hc), 0) // cc == iota((hc, hc), 1) // cc
    lane_head = iota((cc, w), 1) // n

    def block_diag(x, diag):
        return jnp.where(diag, jnp.concatenate([x] * SCAN_HEADS, axis=0), 0.0).astype(BF16)

    def bd_lanes(x):
        return block_diag(x, lanes_diag)

    def bd_tokens(x):
        return block_diag(x, tokens_diag)

    def head_sum(x):
        out = jnp.zeros_like(x)
        for hh in range(SCAN_HEADS):
            mine = lane_head == hh
            out = jnp.where(mine, jnp.sum(jnp.where(mine, x, 0.0), axis=-1, keepdims=True), out)
        return out

    def group(gi):
        c = slice(gi * w, (gi + 1) * w)
        r, k, v, ld, a = r_ref[:, c], k_ref[:, c], v_ref[:, c], ld_ref[:, c], a_ref[:, c]

        p1 = ld.astype(BF16)
        r1 = ld - p1.astype(F32)
        p2 = r1.astype(BF16)
        p3 = (r1 - p2.astype(F32)).astype(BF16)
        cum = _dot(tril, jnp.concatenate([p1, p2, p3], axis=1))
        yield
        kk = k * kk_ref[:, c]
        kmod = k * (1.0 + (a - 1.0) * ka_ref[:, c])
        kap = kk / jnp.maximum(jnp.sqrt(head_sum(kk * kk)), 1e-12)
        b = kap * a

        lcum = cum[:, :w] + cum[:, w:2 * w] + cum[:, 2 * w:]
        lend = lcum[cc - 1:cc, :]
        e_inc = jnp.exp(lcum)
        e_exc = jnp.exp(lcum - ld)
        e_neg = jnp.exp(-lcum)
        e_end = jnp.exp(lend - lcum)

        kq = kap * e_exc
        rq = r * e_inc
        qr = jnp.concatenate([kq, rq], axis=0).astype(BF16)
        bk_bd = jnp.concatenate([bd_lanes(b * e_neg), bd_lanes(kmod * e_neg)], axis=0)
        gram = _dot_nt(qr, bk_bd)
        yield
        s_bd = s_ref[gi]
        qs = _dot_nt(qr, s_bd.astype(BF16))
        yield
        a_w = jnp.where(strict, gram[:cc, :hc], 0.0)
        bk_w = jnp.where(strict, gram[:cc, hc:], 0.0)
        cb_w = jnp.where(causal, gram[cc:, :hc], 0.0)
        ck_w = jnp.where(causal, gram[cc:, hc:], 0.0)
        v_bd = bd_lanes(v)

        bv = _dot(bk_w.astype(BF16), v_bd)
        yield
        x = -(qs[:cc] + bv)
        ax = _dot(a_w.astype(BF16), bd_lanes(x))
        yield
        x = x - ax
        a_pow = a_w
        power = 2
        while power < cc:
            a_pow = _dot(a_pow.astype(BF16), bd_tokens(a_pow))
            yield
            ax = _dot(a_pow.astype(BF16), bd_lanes(x))
            yield
            x = x + ax
            power *= 2
        u = x

        y_in = _dot(jnp.concatenate([cb_w, ck_w], axis=1).astype(BF16),
                    jnp.concatenate([bd_lanes(u), v_bd], axis=0))
        yield
        upd = _dot_tn(jnp.concatenate([u, v], axis=0).astype(BF16),
                      jnp.concatenate([b * e_end, kmod * e_end], axis=0).astype(BF16))
        yield
        s_ref[gi] = s_bd * jnp.exp(lend) + jnp.where(head_diag, upd, 0.0)

        y = qs[cc:] + y_in
        mean = head_sum(y) * (1.0 / n)
        dev = y - mean
        var = head_sum(dev * dev) * (1.0 / n)
        yn = dev * lax.rsqrt(var + GN_EPS) * lnw_ref[:, c] + lnb_ref[:, c]
        bonus = head_sum(r * kmod * rk_ref[:, c]) * v
        z_ref[:, c] = (yn + bonus) * g_ref[:, c]

    _round_robin([group(gi) for gi in range(n_groups)])

    @pl.when(c_idx == pl.num_programs(1) - 1)
    def _():
        for h in range(RWKV_HEADS):
            gi, hh = divmod(h, SCAN_HEADS)
            sout_ref[h] = s_ref[gi, hh * n:(hh + 1) * n, hh * n:(hh + 1) * n]


def _rwkv_scan(proj, k_k, k_a, r_k, ln_w, ln_b, *, n_seq, chunk, n_chunks, s0=None):
    d = D_MODEL
    tok = pl.BlockSpec((chunk, d), lambda b, c: (b * n_chunks + c, 0))
    par = pl.BlockSpec((1, d), lambda b, c: (0, 0))
    state = pl.BlockSpec((None, RWKV_HEADS, RWKV_N, RWKV_N), lambda b, c: (b, 0, 0, 0))
    in_specs = [tok] * 6 + [par] * 5
    args = list(proj) + [p.reshape(1, d) for p in (k_k, k_a, r_k, ln_w, ln_b)]
    if s0 is not None:
        in_specs.append(state)
        args.append(s0)
    n_groups = RWKV_HEADS // SCAN_HEADS
    return pl.pallas_call(
        functools.partial(_scan_kernel, chunk=chunk, has_init=s0 is not None),
        grid=(n_seq, n_chunks),
        in_specs=in_specs,
        out_specs=[tok, state],
        out_shape=[jax.ShapeDtypeStruct((n_seq * n_chunks * chunk, d), F32),
                   jax.ShapeDtypeStruct((n_seq, RWKV_HEADS, RWKV_N, RWKV_N), F32)],
        scratch_shapes=[pltpu.VMEM((n_groups, SCAN_LANES, SCAN_LANES), F32)],
        compiler_params=_params(2),
        name="rwkv_scan",
    )(*args)


def _pad_cols(w):
    return jnp.pad(w, ((0, 0), (0, LORA_PAD - w.shape[1])))


def _pad_rows(w):
    return jnp.pad(w, ((0, LORA_PAD - w.shape[0]), (0, 0)))


def kernel(x_prompt, x_sample, state_pool, cache_win_k, cache_win_v, state_shift, state_wkv, norm_ffn1, norm_mix, norm_ffn2, norm_final, ffn_w_gate, ffn_w_up, ffn_w_down, pool_w, pool_scale, att_w_qkv, att_b_qkv, att_w_o, att_b_o, att_sinks, rel_bias, rwkv_mu, rwkv_w_r, rwkv_w_k, rwkv_w_v, rwkv_w_o, rwkv_w0, rwkv_w1, rwkv_w2, rwkv_a0, rwkv_a1, rwkv_a2, rwkv_g1, rwkv_g2, rwkv_k_k, rwkv_k_a, rwkv_r_k, rwkv_ln_w, rwkv_ln_b):
    d = D_MODEL
    x = jnp.concatenate([x_prompt.reshape(N_PROMPT, d), x_sample.reshape(N_SAMPLE, d)], axis=0)
    ffn_w = (ffn_w_gate, ffn_w_up, ffn_w_down)
    w_bf = tuple(w[0, 0].astype(BF16) for w in ffn_w)
    pool_p, pool_s, wk_p, wv_p, wk_s, wv_s, sh_p, sh_s, wkv_p, wkv_s = ([] for _ in range(10))

    for l in range(DEPTH):
        j, kind = divmod(l, N_MIXERS)
        x, w_bf = _ffn(x, norm_ffn1[l], w_bf, (ffn_w, l, 1))
        gain = norm_mix[l]
        if kind == 0:
            w_pool = pool_w[j].astype(BF16)
            u_s = _norm_rows(x, gain, tm=N_SAMPLE, row0=N_PROMPT, rows=N_SAMPLE).reshape(DEC_BATCH, DEC_SEQ, d)
            ext = jnp.concatenate([jnp.zeros((DEC_BATCH, 1, d), F32), state_pool[j], u_s], axis=1)
            pool_p.append(_norm_tails(x, gain)[:, -POOL_BUF:])
            pool_s.append(ext[:, -POOL_BUF:])
            x = _pool_prompt(x, gain, w_pool, pool_scale[j])
            x = _pool_sample(ext, x, w_pool, pool_scale[j])
        elif kind == 1:
            qkv = _matmul(x, att_w_qkv[j].astype(BF16), att_b_qkv[j], tm=TM_TOK, tn=QKV_DIM // 5, gain=gain)
            k_buf = cache_win_k[j].reshape(DEC_BATCH, WINDOW, KV_DIM)
            v_buf = cache_win_v[j].reshape(DEC_BATCH, WINDOW, KV_DIM)
            o_p = _attention_prompt(qkv, rel_bias, att_sinks[j])
            o_s = _attention_sample(qkv, rel_bias, att_sinks[j], k_buf, v_buf)
            w_o = att_w_o[j].astype(BF16)
            x = _matmul(o_p, w_o, att_b_o[j], x, tm=TM_PROMPT, tn=d, res_row0=0)
            x = _matmul(o_s, w_o, att_b_o[j], x, tm=N_SAMPLE, tn=d, res_row0=N_PROMPT)
            kv_shape = (WINDOW, N_KV_HEADS, HEAD_DIM)
            k_cols = slice(ATT_DIM, ATT_DIM + KV_DIM)
            v_cols = slice(ATT_DIM + KV_DIM, QKV_DIM)
            tails = [slice((b + 1) * SEQ - WINDOW, (b + 1) * SEQ) for b in range(BATCH)]
            wk_p.append(jnp.stack([qkv[t, k_cols] for t in tails]).reshape(BATCH, *kv_shape))
            wv_p.append(jnp.stack([qkv[t, v_cols] for t in tails]).reshape(BATCH, *kv_shape))
            new_rows = slice(N_PROMPT, N_TOK)
            k_s = jnp.concatenate([k_buf, qkv[new_rows, k_cols].reshape(DEC_BATCH, DEC_SEQ, KV_DIM)], axis=1)
            v_s = jnp.concatenate([v_buf, qkv[new_rows, v_cols].reshape(DEC_BATCH, DEC_SEQ, KV_DIM)], axis=1)
            wk_s.append(k_s[:, -WINDOW:].reshape(DEC_BATCH, *kv_shape))
            wv_s.append(v_s[:, -WINDOW:].reshape(DEC_BATCH, *kv_shape))
        else:
            weights = (rwkv_mu[j],
                       rwkv_w_r[j].astype(BF16), rwkv_w_k[j].astype(BF16), rwkv_w_v[j].astype(BF16),
                       _pad_cols(rwkv_w1[j]).astype(BF16), _pad_rows(rwkv_w2[j]).astype(BF16), rwkv_w0[j],
                       _pad_cols(rwkv_a1[j]).astype(BF16), _pad_rows(rwkv_a2[j]).astype(BF16), rwkv_a0[j],
                       rwkv_g1[j].astype(BF16), rwkv_g2[j].astype(BF16))
            first_s = jnp.repeat(state_shift[j], DEC_SEQ, axis=0)
            proj_p = _rwkv_proj(x, gain, weights, row0=0, rows=N_PROMPT, tm=TM_PROMPT, seq_len=SEQ)
            proj_s = _rwkv_proj(x, gain, weights, row0=N_PROMPT, rows=N_SAMPLE, tm=N_SAMPLE, seq_len=DEC_SEQ,
                                first=first_s)
            head_params = (rwkv_k_k[j], rwkv_k_a[j], rwkv_r_k[j], rwkv_ln_w[j], rwkv_ln_b[j])
            z_p, s_p = _rwkv_scan(proj_p, *head_params, n_seq=BATCH, chunk=SCAN_CHUNK, n_chunks=SEQ // SCAN_CHUNK)
            z_s, s_s = _rwkv_scan(proj_s, *head_params, n_seq=DEC_BATCH, chunk=DEC_SEQ, n_chunks=1,
                                  s0=state_wkv[j])
            sh_p.append(_norm_tails(x, gain)[:, -1])
            sh_s.append(_norm_rows(x, gain, tm=N_SAMPLE, row0=N_PROMPT, rows=N_SAMPLE)
                        .reshape(DEC_BATCH, DEC_SEQ, d)[:, -1])
            w_o = rwkv_w_o[j].astype(BF16)
            x = _matmul(z_p, w_o, None, x, tm=TM_PROMPT, tn=d, res_row0=0)
            x = _matmul(z_s, w_o, None, x, tm=N_SAMPLE, tn=d, res_row0=N_PROMPT)
            wkv_p.append(s_p)
            wkv_s.append(s_s)
        if l + 1 < DEPTH:
            x, w_bf = _ffn(x, norm_ffn2[l], w_bf, (ffn_w, l + 1, 0))
        else:
            x = _ffn(x, norm_ffn2[l], w_bf)

    y_p = _norm_rows(x, norm_final, tm=TM_PROMPT, row0=0, rows=N_PROMPT)
    y_s = _norm_rows(x, norm_final, tm=N_SAMPLE, row0=N_PROMPT, rows=N_SAMPLE)
    return (y_p.reshape(BATCH, SEQ, d), y_s.reshape(DEC_BATCH, DEC_SEQ, d),
            jnp.stack(pool_p), jnp.stack(pool_s),
            jnp.stack(wk_p), jnp.stack(wv_p), jnp.stack(wk_s), jnp.stack(wv_s),
            jnp.stack(sh_p), jnp.stack(sh_s),
            jnp.stack(wkv_p), jnp.stack(wkv_s))
```

```python
import functools
import math

import numpy as np
import jax
import jax.numpy as jnp
from jax import lax
from jax.experimental import pallas as pl
from jax.experimental.pallas import tpu as pltpu

F32 = jnp.float32
BF16 = jnp.bfloat16

D_MODEL = 2048
BATCH = 2
SEQ = 4096
DEPTH = 4
DEC_BATCH = 32
DEC_SEQ = 8
PAST_LEN = 16384
N_MIXERS = 3
RMS_EPS = 1e-6
D_FF = 5632
POOL_WINDOWS = (2, 4, 8, 16)
POOL_GROUPS = 4
POOL_GROUP_DIM = D_MODEL // POOL_GROUPS
POOL_BUF = max(POOL_WINDOWS) - 1
POOL_HALO = POOL_BUF + 1
HEAD_DIM = 64
N_HEADS = D_MODEL // HEAD_DIM
N_KV_HEADS = 4
GQA_GROUP = N_HEADS // N_KV_HEADS
ATT_DIM = N_HEADS * HEAD_DIM
KV_DIM = N_KV_HEADS * HEAD_DIM
QKV_DIM = ATT_DIM + 2 * KV_DIM
WINDOW = 128
ATT_SCALE = HEAD_DIM ** -0.5
T5_BUCKETS = 32
T5_MAX_DISTANCE = 128
NEG_INF = -1e30
RWKV_N = 64
RWKV_HEADS = D_MODEL // RWKV_N
GN_EPS = 64e-5
LORA_PAD = 128

N_PROMPT = BATCH * SEQ
N_SAMPLE = DEC_BATCH * DEC_SEQ
N_TOK = N_PROMPT + N_SAMPLE

VMEM_LIMIT_BYTES = 56 * 1024 * 1024

TM_TOK = 704
TM_PROMPT = 512
TM_POOL = 256
SHIFT_HALO = 8
TF_FFN = 512
CAST_BLOCK = 512
CAST_BLOCKS = (D_MODEL // CAST_BLOCK) * (D_FF // CAST_BLOCK)
TN_PROJ = 512
SCAN_CHUNK = 64
SCAN_HEADS = 2
SCAN_LANES = SCAN_HEADS * RWKV_N


def _params(n_axes):
    return pltpu.CompilerParams(dimension_semantics=("arbitrary",) * n_axes,
                                vmem_limit_bytes=VMEM_LIMIT_BYTES)


def _dot(a, b):
    return jnp.dot(a, b, preferred_element_type=F32)


def _dot_nt(a, b):
    return lax.dot_general(a, b, (((1,), (1,)), ((), ())), preferred_element_type=F32)


def _dot_tn(a, b):
    return lax.dot_general(a, b, (((0,), (0,)), ((), ())), preferred_element_type=F32)


def _round_robin(gens):
    live = list(gens)
    while live:
        still = []
        for gen in live:
            try:
                next(gen)
                still.append(gen)
            except StopIteration:
                pass
        live = still


def _rms_norm(x, g):
    return x * lax.rsqrt(jnp.mean(x * x, axis=-1, keepdims=True) + RMS_EPS) * g


def _norm_kernel(x_ref, g_ref, o_ref):
    o_ref[...] = _rms_norm(x_ref[...], g_ref[...])


def _norm_rows(x, g, *, tm, row0, rows):
    d = x.shape[1]
    off = row0 // tm
    return pl.pallas_call(
        _norm_kernel,
        grid=(rows // tm,),
        in_specs=[pl.BlockSpec((tm, d), lambda i: (i + off, 0)),
                  pl.BlockSpec((1, d), lambda i: (0, 0))],
        out_specs=pl.BlockSpec((tm, d), lambda i: (i, 0)),
        out_shape=jax.ShapeDtypeStruct((rows, d), F32),
        compiler_params=_params(1),
        name="rms_norm",
    )(x, g.reshape(1, d))


def _norm_tails(x, g):
    d = x.shape[1]
    blocks_per_seq = SEQ // POOL_HALO
    out = pl.pallas_call(
        _norm_kernel,
        grid=(BATCH,),
        in_specs=[pl.BlockSpec((POOL_HALO, d), lambda b: ((b + 1) * blocks_per_seq - 1, 0)),
                  pl.BlockSpec((1, d), lambda b: (0, 0))],
        out_specs=pl.BlockSpec((POOL_HALO, d), lambda b: (b, 0)),
        out_shape=jax.ShapeDtypeStruct((BATCH * POOL_HALO, d), F32),
        compiler_params=_params(1),
        name="rms_norm_tails",
    )(x, g.reshape(1, d))
    return out.reshape(BATCH, POOL_HALO, d)


def _ffn_kernel(*refs, cast_next):
    if cast_next:
        (x_ref, g_ref, wg_ref, wu_ref, wd_ref, ng_ref, nu_ref, nd_ref,
         o_ref, cg_ref, cu_ref, cd_ref, xn_ref) = refs
        acc_ref = o_ref
    else:
        x_ref, g_ref, wg_ref, wu_ref, wd_ref, fg_ref, yp_ref, ys_ref, xn_ref, acc_ref = refs
    i, j = pl.program_id(0), pl.program_id(1)

    @pl.when(j == 0)
    def _():
        xn_ref[...] = _rms_norm(x_ref[...], g_ref[...]).astype(BF16)
        acc_ref[...] = jnp.zeros_like(acc_ref)

    riders = ((ng_ref, cg_ref), (nu_ref, cu_ref), (nd_ref, cd_ref)) if cast_next else ()

    def convert(pairs):
        for src, dst in pairs:
            dst[...] = src[...].astype(BF16)

    xn = xn_ref[...]
    gate = _dot(xn, wg_ref[...])
    convert(riders[:2])
    up = _dot(xn, wu_ref[...])
    convert(riders[2:])
    h = (gate * jax.nn.sigmoid(gate) * up).astype(BF16)
    acc_ref[...] += _dot(h, wd_ref[...])

    @pl.when(j == pl.num_programs(1) - 1)
    def _():
        o = x_ref[...] + 0.5 * acc_ref[...]
        if cast_next:
            o_ref[...] = o
        else:
            y = _rms_norm(o, fg_ref[...])
            yp_ref[...] = y

            @pl.when(i == pl.num_programs(0) - 1)
            def _():
                ys_ref[...] = y[y.shape[0] - N_SAMPLE:]


def _ffn(x, g, w_bf, w_next=None, final_gain=None):
    m, d = x.shape
    tm, tf, cb = TM_TOK, TF_FFN, CAST_BLOCK
    grid = (m // tm, D_FF // tf)
    in_specs = [pl.BlockSpec((tm, d), lambda i, j: (i, 0)),
                pl.BlockSpec((1, d), lambda i, j: (0, 0)),
                pl.BlockSpec((d, tf), lambda i, j: (0, j)),
                pl.BlockSpec((d, tf), lambda i, j: (0, j)),
                pl.BlockSpec((tf, d), lambda i, j: (j, 0))]
    args = [x, g.reshape(1, d), *w_bf]
    scratch = [pltpu.VMEM((tm, d), BF16)]
    if w_next is not None:
        assert grid[0] * grid[1] == 3 * CAST_BLOCKS
        (ng, nu, nd), layer, half = w_next
        out_specs = [pl.BlockSpec((tm, d), lambda i, j: (i, 0))]
        out_shape = [jax.ShapeDtypeStruct((m, d), F32)]

        def block(k, n_col_blocks):
            def index(i, j):
                b = jnp.clip(i * grid[1] + j - k * CAST_BLOCKS, 0, CAST_BLOCKS - 1)
                return b // n_col_blocks, b % n_col_blocks
            return index

        for k, w in enumerate((ng, nu, nd)):
            idx = block(k, w.shape[3] // cb)
            in_specs.append(pl.BlockSpec((None, None, cb, cb),
                                         lambda i, j, idx=idx: (layer, half, *idx(i, j))))
            out_specs.append(pl.BlockSpec((cb, cb), idx))
            out_shape.append(jax.ShapeDtypeStruct(w.shape[2:], BF16))
            args.append(w)
    else:
        assert m == N_TOK and N_TOK - (grid[0] - 1) * tm >= N_SAMPLE
        in_specs.append(pl.BlockSpec((1, d), lambda i, j: (0, 0)))
        args.append(final_gain.reshape(1, d))
        out_specs = [pl.BlockSpec((tm, d), lambda i, j: (i, 0)),
                     pl.BlockSpec((N_SAMPLE, d), lambda i, j: (0, 0))]
        out_shape = [jax.ShapeDtypeStruct((N_PROMPT, d), F32), jax.ShapeDtypeStruct((N_SAMPLE, d), F32)]
        scratch.append(pltpu.VMEM((tm, d), F32))
    out = pl.pallas_call(
        functools.partial(_ffn_kernel, cast_next=w_next is not None),
        grid=grid,
        in_specs=in_specs,
        out_specs=out_specs,
        out_shape=out_shape,
        scratch_shapes=scratch,
        compiler_params=_params(2),
        name="ffn_half_step",
    )(*args)
    return (out[0], tuple(out[1:])) if w_next is not None else tuple(out)


def _matmul_kernel(*refs, has_gain, has_bias, has_res):
    refs = list(refs)
    lhs_ref, w_ref = refs[0], refs[1]
    pos = 2
    g_ref = b_ref = res_ref = None
    if has_gain:
        g_ref = refs[pos]
        pos += 1
    if has_bias:
        b_ref = refs[pos]
        pos += 1
    if has_res:
        res_ref = refs[pos]
        pos += 1
    o_ref, lhs_bf_ref = refs[pos], refs[pos + 1]

    @pl.when(pl.program_id(1) == 0)
    def _():
        lhs = lhs_ref[...]
        if has_gain:
            lhs = _rms_norm(lhs, g_ref[...])
        lhs_bf_ref[...] = lhs.astype(BF16)

    acc = _dot(lhs_bf_ref[...], w_ref[...])
    if has_bias:
        acc = acc + b_ref[...]
    if has_res:
        acc = res_ref[...] + acc
    o_ref[...] = acc


def _matmul(lhs, w, bias=None, res=None, *, tm, tn, res_row0=0, gain=None):
    m, k = lhs.shape
    n = w.shape[1]
    off = res_row0 // tm
    in_specs = [pl.BlockSpec((tm, k), lambda i, j: (i, 0)),
                pl.BlockSpec((k, tn), lambda i, j: (0, j))]
    args = [lhs, w]
    if gain is not None:
        in_specs.append(pl.BlockSpec((1, k), lambda i, j: (0, 0)))
        args.append(gain.reshape(1, k))
    if bias is not None:
        in_specs.append(pl.BlockSpec((1, tn), lambda i, j: (0, j)))
        args.append(bias.reshape(1, n))
    aliases = {}
    if res is not None:
        in_specs.append(pl.BlockSpec((tm, tn), lambda i, j: (i + off, j)))
        aliases = {len(args): 0}
        args.append(res)
        out_shape = jax.ShapeDtypeStruct(res.shape, F32)
    else:
        out_shape = jax.ShapeDtypeStruct((m, n), F32)
    return pl.pallas_call(
        functools.partial(_matmul_kernel, has_gain=gain is not None, has_bias=bias is not None,
                          has_res=res is not None),
        grid=(m // tm, n // tn),
        in_specs=in_specs,
        out_specs=pl.BlockSpec((tm, tn), lambda i, j: (i + off, j)),
        out_shape=out_shape,
        scratch_shapes=[pltpu.VMEM((tm, k), BF16)],
        input_output_aliases=aliases,
        compiler_params=_params(2),
        name="matmul_bias_residual",
    )(*args)


def _pool_group_out(diff, gi, x_ref, w_ref, sc_ref, o_ref):
    c = slice(gi * POOL_GROUP_DIM, (gi + 1) * POOL_GROUP_DIM)
    out = _dot(diff.astype(BF16), w_ref[gi])
    o_ref[:, c] = x_ref[:, c] + out * sc_ref[:, c]


def _pool_prompt_kernel(x_ref, halo_ref, g_ref, w_ref, sc_ref, o_ref, ext_ref, *, tm, tiles_per_seq, n_tiles):
    i = pl.program_id(0)

    @pl.when(i < n_tiles)
    def _():
        t_in_seq = i % tiles_per_seq
        g = g_ref[...]
        ext_ref[0:POOL_HALO, :] = jnp.where(t_in_seq == 0, 0.0, _rms_norm(halo_ref[...], g))
        ext_ref[POOL_HALO:, :] = _rms_norm(x_ref[...], g)
        pos = t_in_seq * tm + lax.broadcasted_iota(jnp.int32, (tm, 1), 0)
        for gi, w in enumerate(POOL_WINDOWS):
            c = slice(gi * POOL_GROUP_DIM, (gi + 1) * POOL_GROUP_DIM)
            cur = ext_ref[pl.ds(POOL_HALO, tm), c]
            s = cur
            for back in range(1, w):
                s = s + ext_ref[pl.ds(POOL_HALO - back, tm), c]
            cnt = jnp.minimum(w, pos + 1).astype(F32)
            _pool_group_out(s / cnt - cur, gi, x_ref, w_ref, sc_ref, o_ref)

    @pl.when(i >= n_tiles)
    def _():
        o_ref[...] = x_ref[...]


def _pool_prompt(x, gain, w_pool, scale):
    d = x.shape[1]
    tm = TM_POOL
    halo_blocks = tm // POOL_HALO
    return pl.pallas_call(
        functools.partial(_pool_prompt_kernel, tm=tm, tiles_per_seq=SEQ // tm, n_tiles=N_PROMPT // tm),
        grid=(N_TOK // tm,),
        in_specs=[pl.BlockSpec((tm, d), lambda i: (i, 0)),
                  pl.BlockSpec((POOL_HALO, d), lambda i: (jnp.maximum(i * halo_blocks - 1, 0), 0)),
                  pl.BlockSpec((1, d), lambda i: (0, 0)),
                  pl.BlockSpec((POOL_GROUPS, POOL_GROUP_DIM, POOL_GROUP_DIM), lambda i: (0, 0, 0)),
                  pl.BlockSpec((1, d), lambda i: (0, 0))],
        out_specs=pl.BlockSpec((tm, d), lambda i: (i, 0)),
        out_shape=jax.ShapeDtypeStruct(x.shape, F32),
        scratch_shapes=[pltpu.VMEM((tm + POOL_HALO, d), F32)],
        compiler_params=_params(1),
        name="pool_prompt",
    )(x, x, gain.reshape(1, d), w_pool, scale.reshape(1, d))


def _pool_sample_kernel(ext_ref, x_ref, w_ref, sc_ref, o_ref):
    for gi, w in enumerate(POOL_WINDOWS):
        c = slice(gi * POOL_GROUP_DIM, (gi + 1) * POOL_GROUP_DIM)
        cur = ext_ref[:, pl.ds(POOL_HALO, DEC_SEQ), c]
        s = cur
        for back in range(1, w):
            s = s + ext_ref[:, pl.ds(POOL_HALO - back, DEC_SEQ), c]
        cnt = min(w, PAST_LEN + 1)
        diff = (s / float(cnt) - cur).reshape(N_SAMPLE, POOL_GROUP_DIM)
        _pool_group_out(diff, gi, x_ref, w_ref, sc_ref, o_ref)


def _pool_sample(ext, x, w_pool, scale):
    d = x.shape[1]
    blk = N_PROMPT // N_SAMPLE
    return pl.pallas_call(
        _pool_sample_kernel,
        grid=(1,),
        in_specs=[pl.BlockSpec(ext.shape, lambda i: (0, 0, 0)),
                  pl.BlockSpec((N_SAMPLE, d), lambda i: (blk, 0)),
                  pl.BlockSpec((POOL_GROUPS, POOL_GROUP_DIM, POOL_GROUP_DIM), lambda i: (0, 0, 0)),
                  pl.BlockSpec((1, d), lambda i: (0, 0))],
        out_specs=pl.BlockSpec((N_SAMPLE, d), lambda i: (blk, 0)),
        out_shape=jax.ShapeDtypeStruct(x.shape, F32),
        input_output_aliases={1: 0},
        compiler_params=_params(1),
        name="pool_sample",
    )(ext, x, w_pool, scale.reshape(1, d))


def _t5_bucket_table(tq):
    qi = np.arange(tq)[:, None]
    kj = np.arange(WINDOW + tq)[None, :]
    dist = qi + WINDOW - kj
    exact = T5_BUCKETS // 2
    ratio = np.log(np.maximum(dist, 1) / exact) / math.log(T5_MAX_DISTANCE / exact)
    large = np.minimum(exact + (ratio * (T5_BUCKETS - exact)).astype(np.int64), T5_BUCKETS - 1)
    bucket = np.where(dist < exact, dist, large)
    valid = (dist >= 0) & (dist < WINDOW)
    table = np.where(valid, bucket, -1).astype(np.int32)
    return table[:, :WINDOW], table[:, WINDOW:]


def _attn_sample_kernel(rb_ref, sink_ref, bkt_p_ref, bkt_c_ref, q_ref, kc_ref, vc_ref, kp_ref, vp_ref,
                        o_ref, bias_p_ref, bias_c_ref, sink_col_ref, *, tq):
    step = pl.program_id(0)

    @pl.when(step == 0)
    def _():
        bkt_p = bkt_p_ref[...]
        bkt_c = bkt_c_ref[...]
        bias_p_ref[...] = jnp.zeros_like(bias_p_ref)
        bias_c_ref[...] = jnp.zeros_like(bias_c_ref)

        def add_bucket(b, carry):
            eq_p = bkt_p == b
            eq_c = bkt_c == b
            for h in range(N_HEADS):
                kvh, g = divmod(h, GQA_GROUP)
                r = slice(g * tq, (g + 1) * tq)
                val = rb_ref[b, h]
                bias_p_ref[kvh, r, :] += jnp.where(eq_p, val, 0.0)
                bias_c_ref[kvh, r, :] += jnp.where(eq_c, val, 0.0)
            return carry

        lax.fori_loop(0, T5_BUCKETS, add_bucket, 0)
        for h in range(N_HEADS):
            kvh, g = divmod(h, GQA_GROUP)
            r = slice(g * tq, (g + 1) * tq)
            bias_p_ref[kvh, r, :] = jnp.where(bkt_p < 0, NEG_INF, bias_p_ref[kvh, r, :])
            bias_c_ref[kvh, r, :] = jnp.where(bkt_c < 0, NEG_INF, bias_c_ref[kvh, r, :])
            sink_col_ref[kvh, r, :] = jnp.full((tq, 1), sink_ref[h], F32)

    def kv_head(kvh):
        heads = [kvh * GQA_GROUP + g for g in range(GQA_GROUP)]
        qs = jnp.concatenate([q_ref[:, h * HEAD_DIM:(h + 1) * HEAD_DIM] for h in heads], axis=0).astype(BF16)
        c = slice(kvh * HEAD_DIM, (kvh + 1) * HEAD_DIM)
        qk_p = _dot_nt(qs, kp_ref[:, c].astype(BF16))
        qk_c = _dot_nt(qs, kc_ref[:, c].astype(BF16))
        yield
        s_p = qk_p * ATT_SCALE + bias_p_ref[kvh]
        s_c = qk_c * ATT_SCALE + bias_c_ref[kvh]
        sink = sink_col_ref[kvh]
        m = jnp.maximum(jnp.maximum(jnp.max(s_p, axis=-1, keepdims=True),
                                    jnp.max(s_c, axis=-1, keepdims=True)), sink)
        p_p = jnp.exp(s_p - m)
        p_c = jnp.exp(s_c - m)
        den = (jnp.sum(p_p, axis=-1, keepdims=True) + jnp.sum(p_c, axis=-1, keepdims=True)
               + jnp.exp(sink - m))
        o_p = _dot(p_p.astype(BF16), vp_ref[:, c].astype(BF16))
        o_c = _dot(p_c.astype(BF16), vc_ref[:, c].astype(BF16))
        yield
        o = (o_p + o_c) / den
        for g, h in enumerate(heads):
            o_ref[:, h * HEAD_DIM:(h + 1) * HEAD_DIM] = o[g * tq:(g + 1) * tq, :]

    _round_robin([kv_head(kvh) for kvh in range(N_KV_HEADS)])


def _attention_sample(qkv, rel_bias, sinks, k_prev, v_prev):
    tq = DEC_SEQ
    bkt_p, bkt_c = _t5_bucket_table(tq)
    off = N_PROMPT // tq
    kcol = ATT_DIM // KV_DIM
    smem = pl.BlockSpec(memory_space=pltpu.SMEM)
    cache = pl.BlockSpec((None, WINDOW, KV_DIM), lambda i: (i, 0, 0))
    rows = GQA_GROUP * tq
    return pl.pallas_call(
        functools.partial(_attn_sample_kernel, tq=tq),
        grid=(DEC_BATCH,),
        in_specs=[smem, smem,
                  pl.BlockSpec((tq, WINDOW), lambda i: (0, 0)),
                  pl.BlockSpec((tq, tq), lambda i: (0, 0)),
                  pl.BlockSpec((tq, ATT_DIM), lambda i: (i + off, 0)),
                  pl.BlockSpec((tq, KV_DIM), lambda i: (i + off, kcol)),
                  pl.BlockSpec((tq, KV_DIM), lambda i: (i + off, kcol + 1)),
                  cache, cache],
        out_specs=pl.BlockSpec((tq, ATT_DIM), lambda i: (i, 0)),
        out_shape=jax.ShapeDtypeStruct((N_SAMPLE, ATT_DIM), F32),
        scratch_shapes=[pltpu.VMEM((N_KV_HEADS, rows, WINDOW), F32),
                        pltpu.VMEM((N_KV_HEADS, rows, tq), F32),
                        pltpu.VMEM((N_KV_HEADS, rows, 1), F32)],
        compiler_params=_params(1),
        name="swa_attention_sample",
    )(rel_bias, sinks, jnp.asarray(bkt_p), jnp.asarray(bkt_c), qkv, qkv, qkv, k_prev, v_prev)


def _attn_prompt_kernel(rb_ref, sink_ref, bkt_ref, q_ref, kc_ref, vc_ref, kp_ref, vp_ref, o_ref, bias_ref,
                        *, blocks_per_seq):
    step = pl.program_id(0)
    tq, nk, pair = WINDOW, 2 * WINDOW, 2 * HEAD_DIM

    @pl.when(step == 0)
    def _():
        bkt = bkt_ref[...]
        bias_ref[...] = jnp.zeros_like(bias_ref)

        def add_bucket(b, carry):
            eq = bkt == b
            for h in range(N_HEADS):
                bias_ref[h] += jnp.where(eq, rb_ref[b, h], 0.0)
            return carry

        lax.fori_loop(0, T5_BUCKETS, add_bucket, 0)
        for h in range(N_HEADS):
            bias_ref[h] = jnp.where(bkt < 0, NEG_INF, bias_ref[h])

    no_prev = step % blocks_per_seq == 0
    dead = jnp.logical_and(no_prev, lax.broadcasted_iota(jnp.int32, (nk, tq), 0) < WINDOW)
    lane_half = lax.broadcasted_iota(jnp.int32, (nk, pair), 1) // HEAD_DIM
    row_half = lax.broadcasted_iota(jnp.int32, (pair, tq), 0) // HEAD_DIM

    for kvh in range(N_KV_HEADS):
        tile = slice((kvh // 2) * pair, (kvh // 2 + 1) * pair)
        half = kvh % 2

        def both_halves(prev_ref, cur_ref):
            x = jnp.concatenate([prev_ref[:, tile], cur_ref[:, tile]], axis=0)
            own = jnp.where(lane_half == half, x, 0.0)
            other = pltpu.roll(own, HEAD_DIM, axis=1)
            return (own, other) if half == 0 else (other, own)

        k_lo, k_hi = both_halves(kp_ref, kc_ref)
        v_lo, v_hi = both_halves(vp_ref, vc_ref)
        k2 = jnp.concatenate([k_lo, k_hi], axis=0).astype(BF16)
        v_lo_t = v_lo.T.astype(BF16)
        v_hi_t = v_hi.T.astype(BF16)

        def head_pair(j):
            cols = slice(j * pair, (j + 1) * pair)
            qp = (q_ref[:, cols] * ATT_SCALE).astype(BF16)
            s2 = _dot_nt(k2, qp)
            yield
            probs, dens = [], []
            for t in range(2):
                h = 2 * j + t
                s = s2[t * nk:(t + 1) * nk] + bias_ref[h]
                s = jnp.where(dead, NEG_INF, s)
                sink = sink_ref[h]
                m = jnp.maximum(jnp.max(s, axis=0, keepdims=True), sink)
                p = jnp.exp(s - m)
                dens.append(jnp.sum(p, axis=0, keepdims=True) + jnp.exp(sink - m))
                probs.append(p.astype(BF16))
            o2 = _dot(v_lo_t, probs[0]) + _dot(v_hi_t, probs[1])
            yield
            o2 = o2 / jnp.where(row_half == 0, dens[0], dens[1])
            o_ref[:, cols] = o2.T

        pairs_per_kv = GQA_GROUP // 2
        _round_robin([head_pair(kvh * pairs_per_kv + jj) for jj in range(pairs_per_kv)])


def _attention_prompt(qkv, rel_bias, sinks):
    bkt_p, bkt_c = _t5_bucket_table(WINDOW)
    bkt_t = np.ascontiguousarray(np.concatenate([bkt_p, bkt_c], axis=1).T)
    kcol = ATT_DIM // KV_DIM
    smem = pl.BlockSpec(memory_space=pltpu.SMEM)
    prev = lambda i: jnp.maximum(i - 1, 0)
    return pl.pallas_call(
        functools.partial(_attn_prompt_kernel, blocks_per_seq=SEQ // WINDOW),
        grid=(N_PROMPT // WINDOW,),
        in_specs=[smem, smem,
                  pl.BlockSpec((2 * WINDOW, WINDOW), lambda i: (0, 0)),
                  pl.BlockSpec((WINDOW, ATT_DIM), lambda i: (i, 0)),
                  pl.BlockSpec((WINDOW, KV_DIM), lambda i: (i, kcol)),
                  pl.BlockSpec((WINDOW, KV_DIM), lambda i: (i, kcol + 1)),
                  pl.BlockSpec((WINDOW, KV_DIM), lambda i: (prev(i), kcol)),
                  pl.BlockSpec((WINDOW, KV_DIM), lambda i: (prev(i), kcol + 1))],
        out_specs=pl.BlockSpec((WINDOW, ATT_DIM), lambda i: (i, 0)),
        out_shape=jax.ShapeDtypeStruct((N_PROMPT, ATT_DIM), F32),
        scratch_shapes=[pltpu.VMEM((N_HEADS, 2 * WINDOW, WINDOW), F32)],
        compiler_params=_params(1),
        name="swa_attention_prompt",
    )(rel_bias, sinks, jnp.asarray(bkt_t), qkv, qkv, qkv, qkv, qkv)


def _rwkv_proj_kernel(*refs, tm, seq_len, has_first):
    refs = list(refs)
    x_ref, halo_ref, gain_ref = refs[:3]
    pos = 3
    first_ref = None
    if has_first:
        first_ref = refs[pos]
        pos += 1
    (mu_ref, wr_ref, wk_ref, wv_ref, w1_ref, w2_ref, w0_ref, a1_ref, a2_ref, a0_ref, g1_ref, g2_ref,
     r_ref, k_ref, v_ref, ld_ref, a_ref, g_ref,
     ext_ref, xr_ref, xk_ref, xv_ref, hw_ref, ha_ref, hg_ref) = refs[pos:]

    @pl.when(pl.program_id(1) == 0)
    def _():
        gain = gain_ref[...]
        u = _rms_norm(x_ref[...], gain)
        ext_ref[0:SHIFT_HALO, :] = _rms_norm(halo_ref[...], gain)
        ext_ref[SHIFT_HALO:, :] = u
        prev = ext_ref[pl.ds(SHIFT_HALO - 1, tm), :]
        row = pl.program_id(0) * tm + lax.broadcasted_iota(jnp.int32, (tm, 1), 0)
        starts = row % seq_len == 0
        prev = jnp.where(starts, first_ref[...] if has_first else 0.0, prev)
        dx = prev - u
        mix = lambda i: (u + dx * mu_ref[i:i + 1, :]).astype(BF16)
        xr_ref[...] = mix(0)
        hw_ref[...] = jnp.tanh(_dot(mix(1), w1_ref[...])).astype(BF16)
        xk_ref[...] = mix(2)
        xv_ref[...] = mix(3)
        ha_ref[...] = _dot(mix(4), a1_ref[...]).astype(BF16)
        hg_ref[...] = jax.nn.sigmoid(_dot(mix(5), g1_ref[...])).astype(BF16)

    r_ref[...] = _dot(xr_ref[...], wr_ref[...])
    k_ref[...] = _dot(xk_ref[...], wk_ref[...])
    v_ref[...] = _dot(xv_ref[...], wv_ref[...])
    z = -(w0_ref[...] + _dot(hw_ref[...], w2_ref[...]))
    softplus = jnp.maximum(z, 0.0) + jnp.log1p(jnp.exp(-jnp.abs(z)))
    ld_ref[...] = -jnp.exp(-softplus - 0.5)
    a_ref[...] = jax.nn.sigmoid(a0_ref[...] + _dot(ha_ref[...], a2_ref[...]))
    g_ref[...] = _dot(hg_ref[...], g2_ref[...])


def _rwkv_proj(x, gain, weights, *, row0, rows, tm, seq_len, first=None):
    mu, w_r, w_k, w_v, w1, w2, w0, a1, a2, a0, g1, g2 = weights
    d = x.shape[1]
    tn = TN_PROJ
    gl = g1.shape[1]
    off = row0 // tm
    halo_blocks = tm // SHIFT_HALO
    halo_off = row0 // SHIFT_HALO
    row = lambda i, j: (i + off, 0)
    col = lambda i, j: (0, j)
    fixed = lambda i, j: (0, 0)
    in_specs = [pl.BlockSpec((tm, d), row),
                pl.BlockSpec((SHIFT_HALO, d), lambda i, j: (jnp.maximum(halo_off + i * halo_blocks - 1, 0), 0)),
                pl.BlockSpec((1, d), fixed)]
    args = [x, x, gain.reshape(1, d)]
    if first is not None:
        in_specs.append(pl.BlockSpec((tm, d), lambda i, j: (i, 0)))
        args.append(first)
    in_specs += [pl.BlockSpec((6, d), fixed),
                 pl.BlockSpec((d, tn), col), pl.BlockSpec((d, tn), col), pl.BlockSpec((d, tn), col),
                 pl.BlockSpec((d, LORA_PAD), fixed), pl.BlockSpec((LORA_PAD, tn), col), pl.BlockSpec((1, tn), col),
                 pl.BlockSpec((d, LORA_PAD), fixed), pl.BlockSpec((LORA_PAD, tn), col), pl.BlockSpec((1, tn), col),
                 pl.BlockSpec((d, gl), fixed), pl.BlockSpec((gl, tn), col)]
    args += [mu, w_r, w_k, w_v, w1, w2, w0.reshape(1, d), a1, a2, a0.reshape(1, d), g1, g2]
    out = jax.ShapeDtypeStruct((rows, d), F32)
    return pl.pallas_call(
        functools.partial(_rwkv_proj_kernel, tm=tm, seq_len=seq_len, has_first=first is not None),
        grid=(rows // tm, d // tn),
        in_specs=in_specs,
        out_specs=[pl.BlockSpec((tm, tn), lambda i, j: (i, j))] * 6,
        out_shape=[out] * 6,
        scratch_shapes=[pltpu.VMEM((tm + SHIFT_HALO, d), F32)] + [pltpu.VMEM((tm, d), BF16)] * 3
        + [pltpu.VMEM((tm, LORA_PAD), BF16)] * 2 + [pltpu.VMEM((tm, gl), BF16)],
        compiler_params=_params(2),
        name="rwkv_projections",
    )(*args)


def _scan_kernel(*refs, chunk, has_init):
    (r_ref, k_ref, v_ref, ld_ref, a_ref, g_ref, kk_ref, ka_ref, rk_ref, lnw_ref, lnb_ref) = refs[:11]
    refs = refs[11:]
    if has_init:
        s0_ref, refs = refs[0], refs[1:]
    z_ref, sout_ref, s_ref = refs
    c_idx = pl.program_id(1)
    cc, n, w = chunk, RWKV_N, SCAN_LANES
    hc = SCAN_HEADS * cc
    n_groups = RWKV_HEADS // SCAN_HEADS

    @pl.when(c_idx == 0)
    def _():
        s_ref[...] = jnp.zeros_like(s_ref)
        if has_init:
            for h in range(RWKV_HEADS):
                gi, hh = divmod(h, SCAN_HEADS)
                s_ref[gi, hh * n:(hh + 1) * n, hh * n:(hh + 1) * n] = s0_ref[h]

    def iota(shape, axis):
        return lax.broadcasted_iota(jnp.int32, shape, axis)

    head_diag = iota((w, w), 0) // n == iota((w, w), 1) // n
    tril = (iota((cc, cc), 0) >= iota((cc, cc), 1)).astype(BF16)
    t_row = iota((cc, hc), 0)
    s_col = iota((cc, hc), 1) % cc
    strict = t_row > s_col
    causal = t_row >= s_col
    lanes_diag = iota((hc, w), 0) // cc == iota((hc, w), 1) // n
    tokens_diag = iota((hc, hc), 0) // cc == iota((hc, hc), 1) // cc
    lane_head = iota((cc, w), 1) // n

    def block_diag(x, diag):
        return jnp.where(diag, jnp.concatenate([x] * SCAN_HEADS, axis=0), 0.0).astype(BF16)

    def bd_lanes(x):
        return block_diag(x, lanes_diag)

    def bd_tokens(x):
        return block_diag(x, tokens_diag)

    def head_sum(x):
        out = jnp.zeros_like(x)
        for hh in range(SCAN_HEADS):
            mine = lane_head == hh
            out = jnp.where(mine, jnp.sum(jnp.where(mine, x, 0.0), axis=-1, keepdims=True), out)
        return out

    def group(gi):
        c = slice(gi * w, (gi + 1) * w)
        r, k, v, ld, a = r_ref[:, c], k_ref[:, c], v_ref[:, c], ld_ref[:, c], a_ref[:, c]

        p1 = ld.astype(BF16)
        r1 = ld - p1.astype(F32)
        p2 = r1.astype(BF16)
        p3 = (r1 - p2.astype(F32)).astype(BF16)
        cum = _dot(tril, jnp.concatenate([p1, p2, p3], axis=1))
        yield
        kk = k * kk_ref[:, c]
        kmod = k * (1.0 + (a - 1.0) * ka_ref[:, c])
        kap = kk / jnp.maximum(jnp.sqrt(head_sum(kk * kk)), 1e-12)
        b = kap * a

        lcum = cum[:, :w] + cum[:, w:2 * w] + cum[:, 2 * w:]
        lend = lcum[cc - 1:cc, :]
        e_inc = jnp.exp(lcum)
        e_exc = jnp.exp(lcum - ld)
        e_neg = jnp.exp(-lcum)
        e_end = jnp.exp(lend - lcum)

        kq = kap * e_exc
        rq = r * e_inc
        qr = jnp.concatenate([kq, rq], axis=0).astype(BF16)
        bk_bd = jnp.concatenate([bd_lanes(b * e_neg), bd_lanes(kmod * e_neg)], axis=0)
        gram = _dot_nt(qr, bk_bd)
        yield
        s_bd = s_ref[gi]
        qs = _dot_nt(qr, s_bd.astype(BF16))
        yield
        a_w = jnp.where(strict, gram[:cc, :hc], 0.0)
        bk_w = jnp.where(strict, gram[:cc, hc:], 0.0)
        cb_w = jnp.where(causal, gram[cc:, :hc], 0.0)
        ck_w = jnp.where(causal, gram[cc:, hc:], 0.0)
        v_bd = bd_lanes(v)

        bv = _dot(bk_w.astype(BF16), v_bd)
        yield
        x = -(qs[:cc] + bv)
        ax = _dot(a_w.astype(BF16), bd_lanes(x))
        yield
        x = x - ax
        a_pow = a_w
        power = 2
        while power < cc:
            a_pow = _dot(a_pow.astype(BF16), bd_tokens(a_pow))
            yield
            ax = _dot(a_pow.astype(BF16), bd_lanes(x))
            yield
            x = x + ax
            power *= 2
        u = x

        y_in = _dot(jnp.concatenate([cb_w, ck_w], axis=1).astype(BF16),
                    jnp.concatenate([bd_lanes(u), v_bd], axis=0))
        yield
        upd = _dot_tn(jnp.concatenate([u, v], axis=0).astype(BF16),
                      jnp.concatenate([b * e_end, kmod * e_end], axis=0).astype(BF16))
        yield
        s_ref[gi] = s_bd * jnp.exp(lend) + jnp.where(head_diag, upd, 0.0)

        y = qs[cc:] + y_in
        mean = head_sum(y) * (1.0 / n)
        dev = y - mean
        var = head_sum(dev * dev) * (1.0 / n)
        yn = dev * lax.rsqrt(var + GN_EPS) * lnw_ref[:, c] + lnb_ref[:, c]
        bonus = head_sum(r * kmod * rk_ref[:, c]) * v
        z_ref[:, c] = (yn + bonus) * g_ref[:, c]

    _round_robin([group(gi) for gi in range(n_groups)])

    @pl.when(c_idx == pl.num_programs(1) - 1)
    def _():
        for h in range(RWKV_HEADS):
            gi, hh = divmod(h, SCAN_HEADS)
            sout_ref[h] = s_ref[gi, hh * n:(hh + 1) * n, hh * n:(hh + 1) * n]


def _rwkv_scan(proj, k_k, k_a, r_k, ln_w, ln_b, *, n_seq, chunk, n_chunks, s0=None):
    d = D_MODEL
    tok = pl.BlockSpec((chunk, d), lambda b, c: (b * n_chunks + c, 0))
    par = pl.BlockSpec((1, d), lambda b, c: (0, 0))
    state = pl.BlockSpec((None, RWKV_HEADS, RWKV_N, RWKV_N), lambda b, c: (b, 0, 0, 0))
    in_specs = [tok] * 6 + [par] * 5
    args = list(proj) + [p.reshape(1, d) for p in (k_k, k_a, r_k, ln_w, ln_b)]
    if s0 is not None:
        in_specs.append(state)
        args.append(s0)
    n_groups = RWKV_HEADS // SCAN_HEADS
    return pl.pallas_call(
        functools.partial(_scan_kernel, chunk=chunk, has_init=s0 is not None),
        grid=(n_seq, n_chunks),
        in_specs=in_specs,
        out_specs=[tok, state],
        out_shape=[jax.ShapeDtypeStruct((n_seq * n_chunks * chunk, d), F32),
                   jax.ShapeDtypeStruct((n_seq, RWKV_HEADS, RWKV_N, RWKV_N), F32)],
        scratch_shapes=[pltpu.VMEM((n_groups, SCAN_LANES, SCAN_LANES), F32)],
        compiler_params=_params(2),
        name="rwkv_scan",
    )(*args)


def _pad_cols(w):
    return jnp.pad(w, ((0, 0), (0, LORA_PAD - w.shape[1])))


def _pad_rows(w):
    return jnp.pad(w, ((0, LORA_PAD - w.shape[0]), (0, 0)))


def kernel(x_prompt, x_sample, state_pool, cache_win_k, cache_win_v, state_shift, state_wkv, norm_ffn1, norm_mix, norm_ffn2, norm_final, ffn_w_gate, ffn_w_up, ffn_w_down, pool_w, pool_scale, att_w_qkv, att_b_qkv, att_w_o, att_b_o, att_sinks, rel_bias, rwkv_mu, rwkv_w_r, rwkv_w_k, rwkv_w_v, rwkv_w_o, rwkv_w0, rwkv_w1, rwkv_w2, rwkv_a0, rwkv_a1, rwkv_a2, rwkv_g1, rwkv_g2, rwkv_k_k, rwkv_k_a, rwkv_r_k, rwkv_ln_w, rwkv_ln_b):
    d = D_MODEL
    x = jnp.concatenate([x_prompt.reshape(N_PROMPT, d), x_sample.reshape(N_SAMPLE, d)], axis=0)
    ffn_w = (ffn_w_gate, ffn_w_up, ffn_w_down)
    w_bf = tuple(w[0, 0].astype(BF16) for w in ffn_w)
    pool_p, pool_s, wk_p, wv_p, wk_s, wv_s, sh_p, sh_s, wkv_p, wkv_s = ([] for _ in range(10))

    for l in range(DEPTH):
        j, kind = divmod(l, N_MIXERS)
        x, w_bf = _ffn(x, norm_ffn1[l], w_bf, (ffn_w, l, 1))
        gain = norm_mix[l]
        if kind == 0:
            w_pool = pool_w[j].astype(BF16)
            u_s = _norm_rows(x, gain, tm=N_SAMPLE, row0=N_PROMPT, rows=N_SAMPLE).reshape(DEC_BATCH, DEC_SEQ, d)
            ext = jnp.concatenate([jnp.zeros((DEC_BATCH, 1, d), F32), state_pool[j], u_s], axis=1)
            pool_p.append(_norm_tails(x, gain)[:, -POOL_BUF:])
            pool_s.append(ext[:, -POOL_BUF:])
            x = _pool_prompt(x, gain, w_pool, pool_scale[j])
            x = _pool_sample(ext, x, w_pool, pool_scale[j])
        elif kind == 1:
            qkv = _matmul(x, att_w_qkv[j].astype(BF16), att_b_qkv[j], tm=TM_TOK, tn=QKV_DIM, gain=gain)
            k_buf = cache_win_k[j].reshape(DEC_BATCH, WINDOW, KV_DIM)
            v_buf = cache_win_v[j].reshape(DEC_BATCH, WINDOW, KV_DIM)
            o_p = _attention_prompt(qkv, rel_bias, att_sinks[j])
            o_s = _attention_sample(qkv, rel_bias, att_sinks[j], k_buf, v_buf)
            w_o = att_w_o[j].astype(BF16)
            x = _matmul(o_p, w_o, att_b_o[j], x, tm=TM_PROMPT, tn=d, res_row0=0)
            x = _matmul(o_s, w_o, att_b_o[j], x, tm=N_SAMPLE, tn=d, res_row0=N_PROMPT)
            kv_shape = (WINDOW, N_KV_HEADS, HEAD_DIM)
            k_cols = slice(ATT_DIM, ATT_DIM + KV_DIM)
            v_cols = slice(ATT_DIM + KV_DIM, QKV_DIM)
            tails = [slice((b + 1) * SEQ - WINDOW, (b + 1) * SEQ) for b in range(BATCH)]
            wk_p.append(jnp.stack([qkv[t, k_cols] for t in tails]).reshape(BATCH, *kv_shape))
            wv_p.append(jnp.stack([qkv[t, v_cols] for t in tails]).reshape(BATCH, *kv_shape))
            new_rows = slice(N_PROMPT, N_TOK)
            k_s = jnp.concatenate([k_buf, qkv[new_rows, k_cols].reshape(DEC_BATCH, DEC_SEQ, KV_DIM)], axis=1)
            v_s = jnp.concatenate([v_buf, qkv[new_rows, v_cols].reshape(DEC_BATCH, DEC_SEQ, KV_DIM)], axis=1)
            wk_s.append(k_s[:, -WINDOW:].reshape(DEC_BATCH, *kv_shape))
            wv_s.append(v_s[:, -WINDOW:].reshape(DEC_BATCH, *kv_shape))
        else:
            weights = (rwkv_mu[j],
                       rwkv_w_r[j].astype(BF16), rwkv_w_k[j].astype(BF16), rwkv_w_v[j].astype(BF16),
                       _pad_cols(rwkv_w1[j]).astype(BF16), _pad_rows(rwkv_w2[j]).astype(BF16), rwkv_w0[j],
                       _pad_cols(rwkv_a1[j]).astype(BF16), _pad_rows(rwkv_a2[j]).astype(BF16), rwkv_a0[j],
                       rwkv_g1[j].astype(BF16), rwkv_g2[j].astype(BF16))
            first_s = jnp.repeat(state_shift[j], DEC_SEQ, axis=0)
            proj_p = _rwkv_proj(x, gain, weights, row0=0, rows=N_PROMPT, tm=TM_PROMPT, seq_len=SEQ)
            proj_s = _rwkv_proj(x, gain, weights, row0=N_PROMPT, rows=N_SAMPLE, tm=N_SAMPLE, seq_len=DEC_SEQ,
                                first=first_s)
            head_params = (rwkv_k_k[j], rwkv_k_a[j], rwkv_r_k[j], rwkv_ln_w[j], rwkv_ln_b[j])
            z_p, s_p = _rwkv_scan(proj_p, *head_params, n_seq=BATCH, chunk=SCAN_CHUNK, n_chunks=SEQ // SCAN_CHUNK)
            z_s, s_s = _rwkv_scan(proj_s, *head_params, n_seq=DEC_BATCH, chunk=DEC_SEQ, n_chunks=1,
                                  s0=state_wkv[j])
            sh_p.append(_norm_tails(x, gain)[:, -1])
            sh_s.append(_norm_rows(x, gain, tm=N_SAMPLE, row0=N_PROMPT, rows=N_SAMPLE)
                        .reshape(DEC_BATCH, DEC_SEQ, d)[:, -1])
            w_o = rwkv_w_o[j].astype(BF16)
            x = _matmul(z_p, w_o, None, x, tm=TM_PROMPT, tn=d, res_row0=0)
            x = _matmul(z_s, w_o, None, x, tm=N_SAMPLE, tn=d, res_row0=N_PROMPT)
            wkv_p.append(s_p)
            wkv_s.append(s_s)
        if l + 1 < DEPTH:
            x, w_bf = _ffn(x, norm_ffn2[l], w_bf, (ffn_w, l + 1, 0))
        else:
            y_p, y_s = _ffn(x, norm_ffn2[l], w_bf, final_gain=norm_final)

    return (y_p.reshape(BATCH, SEQ, d), y_s.reshape(DEC_BATCH, DEC_SEQ, d),
            jnp.stack(pool_p), jnp.stack(pool_s),
            jnp.stack(wk_p), jnp.stack(wv_p), jnp.stack(wk_s), jnp.stack(wv_s),
            jnp.stack(sh_p), jnp.stack(sh_s),
            jnp.stack(wkv_p), jnp.stack(wkv_s))
```

```python
import functools
import math

import numpy as np
import jax
import jax.numpy as jnp
from jax import lax
from jax.experimental import pallas as pl
from jax.experimental.pallas import tpu as pltpu

F32 = jnp.float32
BF16 = jnp.bfloat16

D_MODEL = 2048
BATCH = 2
SEQ = 4096
DEPTH = 4
DEC_BATCH = 32
DEC_SEQ = 8
PAST_LEN = 16384
N_MIXERS = 3
RMS_EPS = 1e-6
D_FF = 5632
POOL_WINDOWS = (2, 4, 8, 16)
POOL_GROUPS = 4
POOL_GROUP_DIM = D_MODEL // POOL_GROUPS
POOL_BUF = max(POOL_WINDOWS) - 1
POOL_HALO = POOL_BUF + 1
POOL_PAD = 8
assert all(w & (w - 1) == 0 for w in POOL_WINDOWS) and max(POOL_WINDOWS) // 2 <= POOL_PAD
HEAD_DIM = 64
N_HEADS = D_MODEL // HEAD_DIM
N_KV_HEADS = 4
GQA_GROUP = N_HEADS // N_KV_HEADS
ATT_DIM = N_HEADS * HEAD_DIM
KV_DIM = N_KV_HEADS * HEAD_DIM
QKV_DIM = ATT_DIM + 2 * KV_DIM
WINDOW = 128
ATT_SCALE = HEAD_DIM ** -0.5
T5_BUCKETS = 32
T5_MAX_DISTANCE = 128
NEG_INF = -1e30
RWKV_N = 64
RWKV_HEADS = D_MODEL // RWKV_N
GN_EPS = 64e-5
LORA_PAD = 128

N_PROMPT = BATCH * SEQ
N_SAMPLE = DEC_BATCH * DEC_SEQ
N_TOK = N_PROMPT + N_SAMPLE

VMEM_LIMIT_BYTES = 56 * 1024 * 1024

TM_TOK = 704
TM_PROMPT = 512
TM_POOL = 256
SHIFT_HALO = 8
TF_FFN = 512
CAST_WINDOW = (512, 512)
CAST_BLOCKS = (D_MODEL // CAST_WINDOW[0]) * (D_FF // CAST_WINDOW[1])
TN_PROJ = 512
SCAN_CHUNK = 64
SCAN_HEADS = 2
SCAN_LANES = SCAN_HEADS * RWKV_N
SCAN_SAMPLE_SEQS = 4


def _params(n_axes):
    return pltpu.CompilerParams(dimension_semantics=("arbitrary",) * n_axes,
                                vmem_limit_bytes=VMEM_LIMIT_BYTES)


def _dot(a, b):
    return jnp.dot(a, b, preferred_element_type=F32)


def _dot_nt(a, b):
    return lax.dot_general(a, b, (((1,), (1,)), ((), ())), preferred_element_type=F32)


def _dot_tn(a, b):
    return lax.dot_general(a, b, (((0,), (0,)), ((), ())), preferred_element_type=F32)


def _round_robin(gens):
    live = list(gens)
    while live:
        still = []
        for gen in live:
            try:
                next(gen)
                still.append(gen)
            except StopIteration:
                pass
        live = still


def _rms_norm(x, g):
    return x * lax.rsqrt(jnp.mean(x * x, axis=-1, keepdims=True) + RMS_EPS) * g


def _norm_kernel(x_ref, g_ref, o_ref):
    o_ref[...] = _rms_norm(x_ref[...], g_ref[...])


def _norm_rows(x, g, *, tm, row0, rows):
    d = x.shape[1]
    off = row0 // tm
    return pl.pallas_call(
        _norm_kernel,
        grid=(rows // tm,),
        in_specs=[pl.BlockSpec((tm, d), lambda i: (i + off, 0)),
                  pl.BlockSpec((1, d), lambda i: (0, 0))],
        out_specs=pl.BlockSpec((tm, d), lambda i: (i, 0)),
        out_shape=jax.ShapeDtypeStruct((rows, d), F32),
        compiler_params=_params(1),
        name="rms_norm",
    )(x, g.reshape(1, d))


def _norm_tails(x, g):
    d = x.shape[1]
    blocks_per_seq = SEQ // POOL_HALO
    out = pl.pallas_call(
        _norm_kernel,
        grid=(BATCH,),
        in_specs=[pl.BlockSpec((POOL_HALO, d), lambda b: ((b + 1) * blocks_per_seq - 1, 0)),
                  pl.BlockSpec((1, d), lambda b: (0, 0))],
        out_specs=pl.BlockSpec((POOL_HALO, d), lambda b: (b, 0)),
        out_shape=jax.ShapeDtypeStruct((BATCH * POOL_HALO, d), F32),
        compiler_params=_params(1),
        name="rms_norm_tails",
    )(x, g.reshape(1, d))
    return out.reshape(BATCH, POOL_HALO, d)


def _ffn_kernel(*refs, cast_next, lag):
    if cast_next:
        (x_ref, g_ref, wg_ref, wu_ref, wd_ref, ng_ref, nu_ref, nd_ref,
         o_ref, cg_ref, cu_ref, cd_ref, xn_ref) = refs
        acc_ref = o_ref
    else:
        x_ref, g_ref, wg_ref, wu_ref, wd_ref, fg_ref, yp_ref, ys_ref, xn_ref, acc_ref = refs
    i, j = pl.program_id(0), pl.program_id(1)

    @pl.when(j == 0)
    def _():
        xn_ref[...] = _rms_norm(x_ref[...], g_ref[...]).astype(BF16)
        acc_ref[...] = jnp.zeros_like(acc_ref)

    xn = xn_ref[...]
    gate = _dot(xn, wg_ref[...])
    up = _dot(xn, wu_ref[...])
    h = (gate * jax.nn.sigmoid(gate) * up).astype(BF16)
    acc_ref[...] += _dot(h, wd_ref[...])

    @pl.when(j == pl.num_programs(1) - 1)
    def _():
        o = x_ref[...] + 0.5 * acc_ref[...]
        if cast_next:
            o_ref[...] = o
        else:
            y = _rms_norm(o, fg_ref[...])
            yp_ref[...] = y

            @pl.when(i == pl.num_programs(0) - 1)
            def _():
                ys_ref[...] = y[y.shape[0] - N_SAMPLE:]

    if cast_next:
        step = i * pl.num_programs(1) + j
        for k, (src, dst) in enumerate(((ng_ref, cg_ref), (nu_ref, cu_ref), (nd_ref, cd_ref))):
            @pl.when(jnp.logical_and(step >= k * lag, step < k * lag + CAST_BLOCKS))
            def _(src=src, dst=dst):
                dst[...] = src[...].astype(BF16)


def _ffn(x, g, w_bf, w_next=None, final_gain=None):
    m, d = x.shape
    tm, tf = TM_TOK, TF_FFN
    grid = (m // tm, D_FF // tf)
    in_specs = [pl.BlockSpec((tm, d), lambda i, j: (i, 0)),
                pl.BlockSpec((1, d), lambda i, j: (0, 0)),
                pl.BlockSpec((d, tf), lambda i, j: (0, j)),
                pl.BlockSpec((d, tf), lambda i, j: (0, j)),
                pl.BlockSpec((tf, d), lambda i, j: (j, 0))]
    args = [x, g.reshape(1, d), *w_bf]
    scratch = [pltpu.VMEM((tm, d), BF16)]
    lag = 0
    if w_next is not None:
        (ng, nu, nd), layer, half = w_next
        out_specs = [pl.BlockSpec((tm, d), lambda i, j: (i, 0))]
        out_shape = [jax.ShapeDtypeStruct((m, d), F32)]
        slack = grid[0] * grid[1] - CAST_BLOCKS
        assert slack >= 0 and slack % 2 == 0
        lag = slack // 2
        rows, cols = CAST_WINDOW

        def block(k, n_col_blocks):
            def index(i, j):
                b = jnp.clip(i * grid[1] + j - k * lag, 0, CAST_BLOCKS - 1)
                return b // n_col_blocks, b % n_col_blocks
            return index

        for k, w in enumerate((ng, nu, nd)):
            assert (w.shape[2] // rows) * (w.shape[3] // cols) == CAST_BLOCKS
            idx = block(k, w.shape[3] // cols)
            in_specs.append(pl.BlockSpec((None, None, rows, cols),
                                         lambda i, j, idx=idx: (layer, half, *idx(i, j))))
            out_specs.append(pl.BlockSpec((rows, cols), idx))
            out_shape.append(jax.ShapeDtypeStruct(w.shape[2:], BF16))
            args.append(w)
    else:
        assert m == N_TOK and N_TOK - (grid[0] - 1) * tm >= N_SAMPLE
        in_specs.append(pl.BlockSpec((1, d), lambda i, j: (0, 0)))
        args.append(final_gain.reshape(1, d))
        out_specs = [pl.BlockSpec((tm, d), lambda i, j: (i, 0)),
                     pl.BlockSpec((N_SAMPLE, d), lambda i, j: (0, 0))]
        out_shape = [jax.ShapeDtypeStruct((N_PROMPT, d), F32), jax.ShapeDtypeStruct((N_SAMPLE, d), F32)]
        scratch.append(pltpu.VMEM((tm, d), F32))
    out = pl.pallas_call(
        functools.partial(_ffn_kernel, cast_next=w_next is not None, lag=lag),
        grid=grid,
        in_specs=in_specs,
        out_specs=out_specs,
        out_shape=out_shape,
        scratch_shapes=scratch,
        compiler_params=_params(2),
        name="ffn_half_step",
    )(*args)
    return (out[0], tuple(out[1:])) if w_next is not None else tuple(out)


def _matmul_kernel(*refs, has_gain, has_bias, has_res):
    refs = list(refs)
    lhs_ref, w_ref = refs[0], refs[1]
    pos = 2
    g_ref = b_ref = res_ref = None
    if has_gain:
        g_ref = refs[pos]
        pos += 1
    if has_bias:
        b_ref = refs[pos]
        pos += 1
    if has_res:
        res_ref = refs[pos]
        pos += 1
    o_ref, lhs_bf_ref = refs[pos], refs[pos + 1]

    @pl.when(pl.program_id(1) == 0)
    def _():
        lhs = lhs_ref[...]
        if has_gain:
            lhs = _rms_norm(lhs, g_ref[...])
        lhs_bf_ref[...] = lhs.astype(BF16)

    acc = _dot(lhs_bf_ref[...], w_ref[...])
    if has_bias:
        acc = acc + b_ref[...]
    if has_res:
        acc = res_ref[...] + acc
    o_ref[...] = acc


def _matmul(lhs, w, bias=None, res=None, *, tm, tn, res_row0=0, gain=None):
    m, k = lhs.shape
    n = w.shape[1]
    off = res_row0 // tm
    in_specs = [pl.BlockSpec((tm, k), lambda i, j: (i, 0)),
                pl.BlockSpec((k, tn), lambda i, j: (0, j))]
    args = [lhs, w]
    if gain is not None:
        in_specs.append(pl.BlockSpec((1, k), lambda i, j: (0, 0)))
        args.append(gain.reshape(1, k))
    if bias is not None:
        in_specs.append(pl.BlockSpec((1, tn), lambda i, j: (0, j)))
        args.append(bias.reshape(1, n))
    aliases = {}
    if res is not None:
        in_specs.append(pl.BlockSpec((tm, tn), lambda i, j: (i + off, j)))
        aliases = {len(args): 0}
        args.append(res)
        out_shape = jax.ShapeDtypeStruct(res.shape, F32)
    else:
        out_shape = jax.ShapeDtypeStruct((m, n), F32)
    return pl.pallas_call(
        functools.partial(_matmul_kernel, has_gain=gain is not None, has_bias=bias is not None,
                          has_res=res is not None),
        grid=(m // tm, n // tn),
        in_specs=in_specs,
        out_specs=pl.BlockSpec((tm, tn), lambda i, j: (i + off, j)),
        out_shape=out_shape,
        scratch_shapes=[pltpu.VMEM((tm, k), BF16)],
        input_output_aliases=aliases,
        compiler_params=_params(2),
        name="matmul_bias_residual",
    )(*args)


def _pool_group_out(diff, gi, x_ref, w_ref, sc_ref, o_ref):
    c = slice(gi * POOL_GROUP_DIM, (gi + 1) * POOL_GROUP_DIM)
    out = _dot(diff.astype(BF16), w_ref[gi])
    o_ref[:, c] = x_ref[:, c] + out * sc_ref[:, c]


def _pool_prompt_kernel(x_ref, halo_ref, g_ref, w_ref, sc_ref, o_ref, ext_ref, lvl_ref,
                        *, tm, tiles_per_seq, n_tiles):
    i = pl.program_id(0)
    pad, top, rows = POOL_PAD, POOL_PAD + POOL_HALO, tm + POOL_HALO

    @pl.when(i < n_tiles)
    def _():
        t_in_seq = i % tiles_per_seq
        g = g_ref[...]
        ext_ref[0:pad, :] = jnp.zeros((pad, ext_ref.shape[1]), F32)
        ext_ref[pad:top, :] = jnp.where(t_in_seq == 0, 0.0, _rms_norm(halo_ref[...], g))
        ext_ref[top:, :] = _rms_norm(x_ref[...], g)
        lvl_ref[0:pad, :] = jnp.zeros((pad, lvl_ref.shape[1]), F32)
        pos = t_in_seq * tm + lax.broadcasted_iota(jnp.int32, (tm, 1), 0)
        for gi, w in enumerate(POOL_WINDOWS):
            c = slice(gi * POOL_GROUP_DIM, (gi + 1) * POOL_GROUP_DIM)
            part = ext_ref[pl.ds(pad, rows), c] + ext_ref[pl.ds(pad - 1, rows), c]
            span = 2
            while span < w:
                lvl_ref[pl.ds(pad, rows), :] = part
                part = part + lvl_ref[pl.ds(pad - span, rows), :]
                span *= 2
            cur = ext_ref[pl.ds(top, tm), c]
            cnt = jnp.minimum(w, pos + 1).astype(F32)
            _pool_group_out(part[POOL_HALO:] / cnt - cur, gi, x_ref, w_ref, sc_ref, o_ref)

    @pl.when(i >= n_tiles)
    def _():
        o_ref[...] = x_ref[...]


def _pool_prompt(x, gain, w_pool, scale):
    d = x.shape[1]
    tm = TM_POOL
    halo_blocks = tm // POOL_HALO
    return pl.pallas_call(
        functools.partial(_pool_prompt_kernel, tm=tm, tiles_per_seq=SEQ // tm, n_tiles=N_PROMPT // tm),
        grid=(N_TOK // tm,),
        in_specs=[pl.BlockSpec((tm, d), lambda i: (i, 0)),
                  pl.BlockSpec((POOL_HALO, d), lambda i: (jnp.maximum(i * halo_blocks - 1, 0), 0)),
                  pl.BlockSpec((1, d), lambda i: (0, 0)),
                  pl.BlockSpec((POOL_GROUPS, POOL_GROUP_DIM, POOL_GROUP_DIM), lambda i: (0, 0, 0)),
                  pl.BlockSpec((1, d), lambda i: (0, 0))],
        out_specs=pl.BlockSpec((tm, d), lambda i: (i, 0)),
        out_shape=jax.ShapeDtypeStruct(x.shape, F32),
        scratch_shapes=[pltpu.VMEM((POOL_PAD + POOL_HALO + tm, d), F32),
                        pltpu.VMEM((POOL_PAD + POOL_HALO + tm, POOL_GROUP_DIM), F32)],
        compiler_params=_params(1),
        name="pool_prompt",
    )(x, x, gain.reshape(1, d), w_pool, scale.reshape(1, d))


def _pool_sample_kernel(ext_ref, x_ref, w_ref, sc_ref, o_ref):
    for gi, w in enumerate(POOL_WINDOWS):
        c = slice(gi * POOL_GROUP_DIM, (gi + 1) * POOL_GROUP_DIM)
        cur = ext_ref[:, pl.ds(POOL_HALO, DEC_SEQ), c]
        s = cur
        for back in range(1, w):
            s = s + ext_ref[:, pl.ds(POOL_HALO - back, DEC_SEQ), c]
        cnt = min(w, PAST_LEN + 1)
        diff = (s / float(cnt) - cur).reshape(N_SAMPLE, POOL_GROUP_DIM)
        _pool_group_out(diff, gi, x_ref, w_ref, sc_ref, o_ref)


def _pool_sample(ext, x, w_pool, scale):
    d = x.shape[1]
    blk = N_PROMPT // N_SAMPLE
    return pl.pallas_call(
        _pool_sample_kernel,
        grid=(1,),
        in_specs=[pl.BlockSpec(ext.shape, lambda i: (0, 0, 0)),
                  pl.BlockSpec((N_SAMPLE, d), lambda i: (blk, 0)),
                  pl.BlockSpec((POOL_GROUPS, POOL_GROUP_DIM, POOL_GROUP_DIM), lambda i: (0, 0, 0)),
                  pl.BlockSpec((1, d), lambda i: (0, 0))],
        out_specs=pl.BlockSpec((N_SAMPLE, d), lambda i: (blk, 0)),
        out_shape=jax.ShapeDtypeStruct(x.shape, F32),
        input_output_aliases={1: 0},
        compiler_params=_params(1),
        name="pool_sample",
    )(ext, x, w_pool, scale.reshape(1, d))


def _t5_bucket_table(tq):
    qi = np.arange(tq)[:, None]
    kj = np.arange(WINDOW + tq)[None, :]
    dist = qi + WINDOW - kj
    exact = T5_BUCKETS // 2
    ratio = np.log(np.maximum(dist, 1) / exact) / math.log(T5_MAX_DISTANCE / exact)
    large = np.minimum(exact + (ratio * (T5_BUCKETS - exact)).astype(np.int64), T5_BUCKETS - 1)
    bucket = np.where(dist < exact, dist, large)
    valid = (dist >= 0) & (dist < WINDOW)
    table = np.where(valid, bucket, -1).astype(np.int32)
    return table[:, :WINDOW], table[:, WINDOW:]


def _attn_sample_kernel(rb_ref, sink_ref, bkt_p_ref, bkt_c_ref, q_ref, kc_ref, vc_ref, kp_ref, vp_ref,
                        o_ref, bias_p_ref, bias_c_ref, sink_col_ref, *, tq):
    step = pl.program_id(0)

    @pl.when(step == 0)
    def _():
        bkt_p = bkt_p_ref[...]
        bkt_c = bkt_c_ref[...]
        bias_p_ref[...] = jnp.zeros_like(bias_p_ref)
        bias_c_ref[...] = jnp.zeros_like(bias_c_ref)

        def add_bucket(b, carry):
            eq_p = bkt_p == b
            eq_c = bkt_c == b
            for h in range(N_HEADS):
                kvh, g = divmod(h, GQA_GROUP)
                r = slice(g * tq, (g + 1) * tq)
                val = rb_ref[b, h]
                bias_p_ref[kvh, r, :] += jnp.where(eq_p, val, 0.0)
                bias_c_ref[kvh, r, :] += jnp.where(eq_c, val, 0.0)
            return carry

        lax.fori_loop(0, T5_BUCKETS, add_bucket, 0)
        for h in range(N_HEADS):
            kvh, g = divmod(h, GQA_GROUP)
            r = slice(g * tq, (g + 1) * tq)
            bias_p_ref[kvh, r, :] = jnp.where(bkt_p < 0, NEG_INF, bias_p_ref[kvh, r, :])
            bias_c_ref[kvh, r, :] = jnp.where(bkt_c < 0, NEG_INF, bias_c_ref[kvh, r, :])
            sink_col_ref[kvh, r, :] = jnp.full((tq, 1), sink_ref[h], F32)

    def kv_head(kvh):
        heads = [kvh * GQA_GROUP + g for g in range(GQA_GROUP)]
        qs = jnp.concatenate([q_ref[:, h * HEAD_DIM:(h + 1) * HEAD_DIM] for h in heads], axis=0).astype(BF16)
        c = slice(kvh * HEAD_DIM, (kvh + 1) * HEAD_DIM)
        qk_p = _dot_nt(qs, kp_ref[:, c].astype(BF16))
        qk_c = _dot_nt(qs, kc_ref[:, c].astype(BF16))
        yield
        s_p = qk_p * ATT_SCALE + bias_p_ref[kvh]
        s_c = qk_c * ATT_SCALE + bias_c_ref[kvh]
        sink = sink_col_ref[kvh]
        m = jnp.maximum(jnp.maximum(jnp.max(s_p, axis=-1, keepdims=True),
                                    jnp.max(s_c, axis=-1, keepdims=True)), sink)
        p_p = jnp.exp(s_p - m)
        p_c = jnp.exp(s_c - m)
        den = (jnp.sum(p_p, axis=-1, keepdims=True) + jnp.sum(p_c, axis=-1, keepdims=True)
               + jnp.exp(sink - m))
        o_p = _dot(p_p.astype(BF16), vp_ref[:, c].astype(BF16))
        o_c = _dot(p_c.astype(BF16), vc_ref[:, c].astype(BF16))
        yield
        o = (o_p + o_c) / den
        for g, h in enumerate(heads):
            o_ref[:, h * HEAD_DIM:(h + 1) * HEAD_DIM] = o[g * tq:(g + 1) * tq, :]

    _round_robin([kv_head(kvh) for kvh in range(N_KV_HEADS)])


def _attention_sample(qkv, rel_bias, sinks, k_prev, v_prev):
    tq = DEC_SEQ
    bkt_p, bkt_c = _t5_bucket_table(tq)
    off = N_PROMPT // tq
    kcol = ATT_DIM // KV_DIM
    smem = pl.BlockSpec(memory_space=pltpu.SMEM)
    cache = pl.BlockSpec((None, WINDOW, KV_DIM), lambda i: (i, 0, 0))
    rows = GQA_GROUP * tq
    return pl.pallas_call(
        functools.partial(_attn_sample_kernel, tq=tq),
        grid=(DEC_BATCH,),
        in_specs=[smem, smem,
                  pl.BlockSpec((tq, WINDOW), lambda i: (0, 0)),
                  pl.BlockSpec((tq, tq), lambda i: (0, 0)),
                  pl.BlockSpec((tq, ATT_DIM), lambda i: (i + off, 0)),
                  pl.BlockSpec((tq, KV_DIM), lambda i: (i + off, kcol)),
                  pl.BlockSpec((tq, KV_DIM), lambda i: (i + off, kcol + 1)),
                  cache, cache],
        out_specs=pl.BlockSpec((tq, ATT_DIM), lambda i: (i, 0)),
        out_shape=jax.ShapeDtypeStruct((N_SAMPLE, ATT_DIM), F32),
        scratch_shapes=[pltpu.VMEM((N_KV_HEADS, rows, WINDOW), F32),
                        pltpu.VMEM((N_KV_HEADS, rows, tq), F32),
                        pltpu.VMEM((N_KV_HEADS, rows, 1), F32)],
        compiler_params=_params(1),
        name="swa_attention_sample",
    )(rel_bias, sinks, jnp.asarray(bkt_p), jnp.asarray(bkt_c), qkv, qkv, qkv, k_prev, v_prev)


def _attn_prompt_kernel(rb_ref, sink_ref, bkt_ref, q_ref, kc_ref, vc_ref, kp_ref, vp_ref, o_ref, bias_ref,
                        *, blocks_per_seq):
    step = pl.program_id(0)
    tq, nk, pair = WINDOW, 2 * WINDOW, 2 * HEAD_DIM

    @pl.when(step == 0)
    def _():
        bkt = bkt_ref[...]
        bias_ref[...] = jnp.zeros_like(bias_ref)

        def add_bucket(b, carry):
            eq = bkt == b
            for h in range(N_HEADS):
                bias_ref[h] += jnp.where(eq, rb_ref[b, h], 0.0)
            return carry

        lax.fori_loop(0, T5_BUCKETS, add_bucket, 0)
        for h in range(N_HEADS):
            bias_ref[h] = jnp.where(bkt < 0, NEG_INF, bias_ref[h])

    no_prev = step % blocks_per_seq == 0
    dead = jnp.logical_and(no_prev, lax.broadcasted_iota(jnp.int32, (nk, tq), 0) < WINDOW)
    lane_half = lax.broadcasted_iota(jnp.int32, (nk, pair), 1) // HEAD_DIM
    row_half = lax.broadcasted_iota(jnp.int32, (pair, tq), 0) // HEAD_DIM

    for kvh in range(N_KV_HEADS):
        tile = slice((kvh // 2) * pair, (kvh // 2 + 1) * pair)
        half = kvh % 2

        def both_halves(prev_ref, cur_ref):
            x = jnp.concatenate([prev_ref[:, tile], cur_ref[:, tile]], axis=0)
            own = jnp.where(lane_half == half, x, 0.0)
            other = pltpu.roll(own, HEAD_DIM, axis=1)
            return (own, other) if half == 0 else (other, own)

        k_lo, k_hi = both_halves(kp_ref, kc_ref)
        v_lo, v_hi = both_halves(vp_ref, vc_ref)
        k2 = jnp.concatenate([k_lo, k_hi], axis=0).astype(BF16)
        v_lo_t = v_lo.T.astype(BF16)
        v_hi_t = v_hi.T.astype(BF16)

        def head_pair(j):
            cols = slice(j * pair, (j + 1) * pair)
            qp = (q_ref[:, cols] * ATT_SCALE).astype(BF16)
            s2 = _dot_nt(k2, qp)
            yield
            probs, dens = [], []
            for t in range(2):
                h = 2 * j + t
                s = s2[t * nk:(t + 1) * nk] + bias_ref[h]
                s = jnp.where(dead, NEG_INF, s)
                sink = sink_ref[h]
                m = jnp.maximum(jnp.max(s, axis=0, keepdims=True), sink)
                p = jnp.exp(s - m)
                dens.append(jnp.sum(p, axis=0, keepdims=True) + jnp.exp(sink - m))
                probs.append(p.astype(BF16))
            o2 = _dot(v_lo_t, probs[0]) + _dot(v_hi_t, probs[1])
            yield
            o2 = o2 / jnp.where(row_half == 0, dens[0], dens[1])
            o_ref[:, cols] = o2.T

        pairs_per_kv = GQA_GROUP // 2
        _round_robin([head_pair(kvh * pairs_per_kv + jj) for jj in range(pairs_per_kv)])


def _attention_prompt(qkv, rel_bias, sinks):
    bkt_p, bkt_c = _t5_bucket_table(WINDOW)
    bkt_t = np.ascontiguousarray(np.concatenate([bkt_p, bkt_c], axis=1).T)
    kcol = ATT_DIM // KV_DIM
    smem = pl.BlockSpec(memory_space=pltpu.SMEM)
    prev = lambda i: jnp.maximum(i - 1, 0)
    return pl.pallas_call(
        functools.partial(_attn_prompt_kernel, blocks_per_seq=SEQ // WINDOW),
        grid=(N_PROMPT // WINDOW,),
        in_specs=[smem, smem,
                  pl.BlockSpec((2 * WINDOW, WINDOW), lambda i: (0, 0)),
                  pl.BlockSpec((WINDOW, ATT_DIM), lambda i: (i, 0)),
                  pl.BlockSpec((WINDOW, KV_DIM), lambda i: (i, kcol)),
                  pl.BlockSpec((WINDOW, KV_DIM), lambda i: (i, kcol + 1)),
                  pl.BlockSpec((WINDOW, KV_DIM), lambda i: (prev(i), kcol)),
                  pl.BlockSpec((WINDOW, KV_DIM), lambda i: (prev(i), kcol + 1))],
        out_specs=pl.BlockSpec((WINDOW, ATT_DIM), lambda i: (i, 0)),
        out_shape=jax.ShapeDtypeStruct((N_PROMPT, ATT_DIM), F32),
        scratch_shapes=[pltpu.VMEM((N_HEADS, 2 * WINDOW, WINDOW), F32)],
        compiler_params=_params(1),
        name="swa_attention_prompt",
    )(rel_bias, sinks, jnp.asarray(bkt_t), qkv, qkv, qkv, qkv, qkv)


def _rwkv_proj_kernel(*refs, tm, seq_len, has_first):
    refs = list(refs)
    x_ref, halo_ref, gain_ref = refs[:3]
    pos = 3
    first_ref = None
    if has_first:
        first_ref = refs[pos]
        pos += 1
    (mu_ref, wr_ref, wk_ref, wv_ref, w1_ref, w2_ref, w0_ref, a1_ref, a2_ref, a0_ref, g1_ref, g2_ref,
     r_ref, k_ref, v_ref, ld_ref, a_ref, g_ref,
     ext_ref, xr_ref, xk_ref, xv_ref, hw_ref, ha_ref, hg_ref) = refs[pos:]

    @pl.when(pl.program_id(1) == 0)
    def _():
        gain = gain_ref[...]
        u = _rms_norm(x_ref[...], gain)
        ext_ref[0:SHIFT_HALO, :] = _rms_norm(halo_ref[...], gain)
        ext_ref[SHIFT_HALO:, :] = u
        prev = ext_ref[pl.ds(SHIFT_HALO - 1, tm), :]
        row = pl.program_id(0) * tm + lax.broadcasted_iota(jnp.int32, (tm, 1), 0)
        starts = row % seq_len == 0
        prev = jnp.where(starts, first_ref[...] if has_first else 0.0, prev)
        dx = prev - u
        mix = lambda i: (u + dx * mu_ref[i:i + 1, :]).astype(BF16)
        xr_ref[...] = mix(0)
        hw_ref[...] = jnp.tanh(_dot(mix(1), w1_ref[...])).astype(BF16)
        xk_ref[...] = mix(2)
        xv_ref[...] = mix(3)
        ha_ref[...] = _dot(mix(4), a1_ref[...]).astype(BF16)
        hg_ref[...] = jax.nn.sigmoid(_dot(mix(5), g1_ref[...])).astype(BF16)

    r_ref[...] = _dot(xr_ref[...], wr_ref[...])
    k_ref[...] = _dot(xk_ref[...], wk_ref[...])
    v_ref[...] = _dot(xv_ref[...], wv_ref[...])
    z = -(w0_ref[...] + _dot(hw_ref[...], w2_ref[...]))
    softplus = jnp.maximum(z, 0.0) + jnp.log1p(jnp.exp(-jnp.abs(z)))
    ld_ref[...] = -jnp.exp(-softplus - 0.5)
    a_ref[...] = jax.nn.sigmoid(a0_ref[...] + _dot(ha_ref[...], a2_ref[...]))
    g_ref[...] = _dot(hg_ref[...], g2_ref[...])


def _rwkv_proj(x, gain, weights, *, row0, rows, tm, seq_len, first=None):
    mu, w_r, w_k, w_v, w1, w2, w0, a1, a2, a0, g1, g2 = weights
    d = x.shape[1]
    tn = TN_PROJ
    gl = g1.shape[1]
    off = row0 // tm
    halo_blocks = tm // SHIFT_HALO
    halo_off = row0 // SHIFT_HALO
    row = lambda i, j: (i + off, 0)
    col = lambda i, j: (0, j)
    fixed = lambda i, j: (0, 0)
    in_specs = [pl.BlockSpec((tm, d), row),
                pl.BlockSpec((SHIFT_HALO, d), lambda i, j: (jnp.maximum(halo_off + i * halo_blocks - 1, 0), 0)),
                pl.BlockSpec((1, d), fixed)]
    args = [x, x, gain.reshape(1, d)]
    if first is not None:
        in_specs.append(pl.BlockSpec((tm, d), lambda i, j: (i, 0)))
        args.append(first)
    in_specs += [pl.BlockSpec((6, d), fixed),
                 pl.BlockSpec((d, tn), col), pl.BlockSpec((d, tn), col), pl.BlockSpec((d, tn), col),
                 pl.BlockSpec((d, LORA_PAD), fixed), pl.BlockSpec((LORA_PAD, tn), col), pl.BlockSpec((1, tn), col),
                 pl.BlockSpec((d, LORA_PAD), fixed), pl.BlockSpec((LORA_PAD, tn), col), pl.BlockSpec((1, tn), col),
                 pl.BlockSpec((d, gl), fixed), pl.BlockSpec((gl, tn), col)]
    args += [mu, w_r, w_k, w_v, w1, w2, w0.reshape(1, d), a1, a2, a0.reshape(1, d), g1, g2]
    out = jax.ShapeDtypeStruct((rows, d), F32)
    return pl.pallas_call(
        functools.partial(_rwkv_proj_kernel, tm=tm, seq_len=seq_len, has_first=first is not None),
        grid=(rows // tm, d // tn),
        in_specs=in_specs,
        out_specs=[pl.BlockSpec((tm, tn), lambda i, j: (i, j))] * 6,
        out_shape=[out] * 6,
        scratch_shapes=[pltpu.VMEM((tm + SHIFT_HALO, d), F32)] + [pltpu.VMEM((tm, d), BF16)] * 3
        + [pltpu.VMEM((tm, LORA_PAD), BF16)] * 2 + [pltpu.VMEM((tm, gl), BF16)],
        compiler_params=_params(2),
        name="rwkv_projections",
    )(*args)


def _scan_kernel(*refs, chunk, has_init):
    (r_ref, k_ref, v_ref, ld_ref, a_ref, g_ref, kk_ref, ka_ref, rk_ref, lnw_ref, lnb_ref) = refs[:11]
    refs = refs[11:]
    if has_init:
        s0_ref, refs = refs[0], refs[1:]
    z_ref, sout_ref, s_ref = refs
    c_idx = pl.program_id(1)
    cc, n, w = chunk, RWKV_N, SCAN_LANES
    hc = SCAN_HEADS * cc
    n_groups = RWKV_HEADS // SCAN_HEADS
    n_sb = z_ref.shape[0]

    @pl.when(c_idx == 0)
    def _():
        s_ref[...] = jnp.zeros_like(s_ref)
        if has_init:
            for sb in range(n_sb):
                for h in range(RWKV_HEADS):
                    gi, hh = divmod(h, SCAN_HEADS)
                    s_ref[sb, gi, hh * n:(hh + 1) * n, hh * n:(hh + 1) * n] = s0_ref[sb, h]

    def iota(shape, axis):
        return lax.broadcasted_iota(jnp.int32, shape, axis)

    head_diag = iota((w, w), 0) // n == iota((w, w), 1) // n
    tril = (iota((cc, cc), 0) >= iota((cc, cc), 1)).astype(BF16)
    t_row = iota((cc, hc), 0)
    s_col = iota((cc, hc), 1) % cc
    strict = t_row > s_col
    causal = t_row >= s_col
    lanes_diag = iota((hc, w), 0) // cc == iota((hc, w), 1) // n
    tokens_diag = iota((hc, hc), 0) // cc == iota((hc, hc), 1) // cc
    lane_head = iota((cc, w), 1) // n

    def block_diag(x, diag):
        return jnp.where(diag, jnp.concatenate([x] * SCAN_HEADS, axis=0), 0.0).astype(BF16)

    def bd_lanes(x):
        return block_diag(x, lanes_diag)

    def bd_tokens(x):
        return block_diag(x, tokens_diag)

    def head_sum(x):
        out = jnp.zeros_like(x)
        for hh in range(SCAN_HEADS):
            mine = lane_head == hh
            out = jnp.where(mine, jnp.sum(jnp.where(mine, x, 0.0), axis=-1, keepdims=True), out)
        return out

    def group(sb, gi):
        c = slice(gi * w, (gi + 1) * w)
        r, k, v, ld, a = r_ref[sb, :, c], k_ref[sb, :, c], v_ref[sb, :, c], ld_ref[sb, :, c], a_ref[sb, :, c]

        p1 = ld.astype(BF16)
        r1 = ld - p1.astype(F32)
        p2 = r1.astype(BF16)
        p3 = (r1 - p2.astype(F32)).astype(BF16)
        cum = _dot(tril, jnp.concatenate([p1, p2, p3], axis=1))
        yield
        kk = k * kk_ref[:, c]
        kmod = k * (1.0 + (a - 1.0) * ka_ref[:, c])
        kap = kk / jnp.maximum(jnp.sqrt(head_sum(kk * kk)), 1e-12)
        b = kap * a

        lcum = cum[:, :w] + cum[:, w:2 * w] + cum[:, 2 * w:]
        lend = lcum[cc - 1:cc, :]
        e_inc = jnp.exp(lcum)
        e_exc = jnp.exp(lcum - ld)
        e_neg = jnp.exp(-lcum)
        e_end = jnp.exp(lend - lcum)

        kq = kap * e_exc
        rq = r * e_inc
        qr = jnp.concatenate([kq, rq], axis=0).astype(BF16)
        bk_bd = jnp.concatenate([bd_lanes(b * e_neg), bd_lanes(kmod * e_neg)], axis=0)
        gram = _dot_nt(qr, bk_bd)
        yield
        s_bd = s_ref[sb, gi]
        qs = _dot_nt(qr, s_bd.astype(BF16))
        yield
        a_w = jnp.where(strict, gram[:cc, :hc], 0.0)
        bk_w = jnp.where(strict, gram[:cc, hc:], 0.0)
        cb_w = jnp.where(causal, gram[cc:, :hc], 0.0)
        ck_w = jnp.where(causal, gram[cc:, hc:], 0.0)
        v_bd = bd_lanes(v)

        bv = _dot(bk_w.astype(BF16), v_bd)
        yield
        x = -(qs[:cc] + bv)
        ax = _dot(a_w.astype(BF16), bd_lanes(x))
        yield
        x = x - ax
        a_pow = a_w
        power = 2
        while power < cc:
            a_pow = _dot(a_pow.astype(BF16), bd_tokens(a_pow))
            yield
            ax = _dot(a_pow.astype(BF16), bd_lanes(x))
            yield
            x = x + ax
            power *= 2
        u = x

        y_in = _dot(jnp.concatenate([cb_w, ck_w], axis=1).astype(BF16),
                    jnp.concatenate([bd_lanes(u), v_bd], axis=0))
        yield
        upd = _dot_tn(jnp.concatenate([u, v], axis=0).astype(BF16),
                      jnp.concatenate([b * e_end, kmod * e_end], axis=0).astype(BF16))
        yield
        s_ref[sb, gi] = s_bd * jnp.exp(lend) + jnp.where(head_diag, upd, 0.0)

        y = qs[cc:] + y_in
        mean = head_sum(y) * (1.0 / n)
        dev = y - mean
        var = head_sum(dev * dev) * (1.0 / n)
        yn = dev * lax.rsqrt(var + GN_EPS) * lnw_ref[:, c] + lnb_ref[:, c]
        bonus = head_sum(r * kmod * rk_ref[:, c]) * v
        z_ref[sb, :, c] = (yn + bonus) * g_ref[sb, :, c]

    _round_robin([group(sb, gi) for sb in range(n_sb) for gi in range(n_groups)])

    @pl.when(c_idx == pl.num_programs(1) - 1)
    def _():
        for sb in range(n_sb):
            for h in range(RWKV_HEADS):
                gi, hh = divmod(h, SCAN_HEADS)
                sout_ref[sb, h] = s_ref[sb, gi, hh * n:(hh + 1) * n, hh * n:(hh + 1) * n]


def _rwkv_scan(proj, k_k, k_a, r_k, ln_w, ln_b, *, n_seq, per_step, chunk, n_chunks, s0=None):
    d = D_MODEL
    t = chunk * n_chunks
    tok = pl.BlockSpec((per_step, chunk, d), lambda b, c: (b, c, 0))
    par = pl.BlockSpec((1, d), lambda b, c: (0, 0))
    state = pl.BlockSpec((per_step, RWKV_HEADS, RWKV_N, RWKV_N), lambda b, c: (b, 0, 0, 0))
    in_specs = [tok] * 6 + [par] * 5
    args = [p.reshape(n_seq, t, d) for p in proj] + [p.reshape(1, d) for p in (k_k, k_a, r_k, ln_w, ln_b)]
    if s0 is not None:
        in_specs.append(state)
        args.append(s0)
    n_groups = RWKV_HEADS // SCAN_HEADS
    z, s_out = pl.pallas_call(
        functools.partial(_scan_kernel, chunk=chunk, has_init=s0 is not None),
        grid=(n_seq // per_step, n_chunks),
        in_specs=in_specs,
        out_specs=[tok, state],
        out_shape=[jax.ShapeDtypeStruct((n_seq, t, d), F32),
                   jax.ShapeDtypeStruct((n_seq, RWKV_HEADS, RWKV_N, RWKV_N), F32)],
        scratch_shapes=[pltpu.VMEM((per_step, n_groups, SCAN_LANES, SCAN_LANES), F32)],
        compiler_params=_params(2),
        name="rwkv_scan",
    )(*args)
    return z.reshape(n_seq * t, d), s_out


def _pad_cols(w):
    return jnp.pad(w, ((0, 0), (0, LORA_PAD - w.shape[1])))


def _pad_rows(w):
    return jnp.pad(w, ((0, LORA_PAD - w.shape[0]), (0, 0)))


def kernel(x_prompt, x_sample, state_pool, cache_win_k, cache_win_v, state_shift, state_wkv, norm_ffn1, norm_mix, norm_ffn2, norm_final, ffn_w_gate, ffn_w_up, ffn_w_down, pool_w, pool_scale, att_w_qkv, att_b_qkv, att_w_o, att_b_o, att_sinks, rel_bias, rwkv_mu, rwkv_w_r, rwkv_w_k, rwkv_w_v, rwkv_w_o, rwkv_w0, rwkv_w1, rwkv_w2, rwkv_a0, rwkv_a1, rwkv_a2, rwkv_g1, rwkv_g2, rwkv_k_k, rwkv_k_a, rwkv_r_k, rwkv_ln_w, rwkv_ln_b):
    d = D_MODEL
    x = jnp.concatenate([x_prompt.reshape(N_PROMPT, d), x_sample.reshape(N_SAMPLE, d)], axis=0)
    ffn_w = (ffn_w_gate, ffn_w_up, ffn_w_down)
    w_bf = tuple(w[0, 0].astype(BF16) for w in ffn_w)
    pool_p, pool_s, wk_p, wv_p, wk_s, wv_s, sh_p, sh_s, wkv_p, wkv_s = ([] for _ in range(10))

    for l in range(DEPTH):
        j, kind = divmod(l, N_MIXERS)
        x, w_bf = _ffn(x, norm_ffn1[l], w_bf, (ffn_w, l, 1))
        gain = norm_mix[l]
        if kind == 0:
            w_pool = pool_w[j].astype(BF16)
            u_s = _norm_rows(x, gain, tm=N_SAMPLE, row0=N_PROMPT, rows=N_SAMPLE).reshape(DEC_BATCH, DEC_SEQ, d)
            ext = jnp.concatenate([jnp.zeros((DEC_BATCH, 1, d), F32), state_pool[j], u_s], axis=1)
            pool_p.append(_norm_tails(x, gain)[:, -POOL_BUF:])
            pool_s.append(ext[:, -POOL_BUF:])
            x = _pool_prompt(x, gain, w_pool, pool_scale[j])
            x = _pool_sample(ext, x, w_pool, pool_scale[j])
        elif kind == 1:
            qkv = _matmul(x, att_w_qkv[j].astype(BF16), att_b_qkv[j], tm=TM_TOK, tn=QKV_DIM, gain=gain)
            k_buf = cache_win_k[j].reshape(DEC_BATCH, WINDOW, KV_DIM)
            v_buf = cache_win_v[j].reshape(DEC_BATCH, WINDOW, KV_DIM)
            o_p = _attention_prompt(qkv, rel_bias, att_sinks[j])
            o_s = _attention_sample(qkv, rel_bias, att_sinks[j], k_buf, v_buf)
            w_o = att_w_o[j].astype(BF16)
            x = _matmul(o_p, w_o, att_b_o[j], x, tm=TM_PROMPT, tn=d, res_row0=0)
            x = _matmul(o_s, w_o, att_b_o[j], x, tm=N_SAMPLE, tn=d, res_row0=N_PROMPT)
            kv_shape = (WINDOW, N_KV_HEADS, HEAD_DIM)
            k_cols = slice(ATT_DIM, ATT_DIM + KV_DIM)
            v_cols = slice(ATT_DIM + KV_DIM, QKV_DIM)
            tails = [slice((b + 1) * SEQ - WINDOW, (b + 1) * SEQ) for b in range(BATCH)]
            wk_p.append(jnp.stack([qkv[t, k_cols] for t in tails]).reshape(BATCH, *kv_shape))
            wv_p.append(jnp.stack([qkv[t, v_cols] for t in tails]).reshape(BATCH, *kv_shape))
            new_rows = slice(N_PROMPT, N_TOK)
            k_s = jnp.concatenate([k_buf, qkv[new_rows, k_cols].reshape(DEC_BATCH, DEC_SEQ, KV_DIM)], axis=1)
            v_s = jnp.concatenate([v_buf, qkv[new_rows, v_cols].reshape(DEC_BATCH, DEC_SEQ, KV_DIM)], axis=1)
            wk_s.append(k_s[:, -WINDOW:].reshape(DEC_BATCH, *kv_shape))
            wv_s.append(v_s[:, -WINDOW:].reshape(DEC_BATCH, *kv_shape))
        else:
            weights = (rwkv_mu[j],
                       rwkv_w_r[j].astype(BF16), rwkv_w_k[j].astype(BF16), rwkv_w_v[j].astype(BF16),
                       _pad_cols(rwkv_w1[j]).astype(BF16), _pad_rows(rwkv_w2[j]).astype(BF16), rwkv_w0[j],
                       _pad_cols(rwkv_a1[j]).astype(BF16), _pad_rows(rwkv_a2[j]).astype(BF16), rwkv_a0[j],
                       rwkv_g1[j].astype(BF16), rwkv_g2[j].astype(BF16))
            first_s = jnp.repeat(state_shift[j], DEC_SEQ, axis=0)
            proj_p = _rwkv_proj(x, gain, weights, row0=0, rows=N_PROMPT, tm=TM_PROMPT, seq_len=SEQ)
            proj_s = _rwkv_proj(x, gain, weights, row0=N_PROMPT, rows=N_SAMPLE, tm=N_SAMPLE, seq_len=DEC_SEQ,
                                first=first_s)
            head_params = (rwkv_k_k[j], rwkv_k_a[j], rwkv_r_k[j], rwkv_ln_w[j], rwkv_ln_b[j])
            z_p, s_p = _rwkv_scan(proj_p, *head_params, n_seq=BATCH, per_step=BATCH, chunk=SCAN_CHUNK,
                                  n_chunks=SEQ // SCAN_CHUNK)
            z_s, s_s = _rwkv_scan(proj_s, *head_params, n_seq=DEC_BATCH, per_step=SCAN_SAMPLE_SEQS, chunk=DEC_SEQ,
                                  n_chunks=1, s0=state_wkv[j])
            sh_p.append(_norm_tails(x, gain)[:, -1])
            sh_s.append(_norm_rows(x, gain, tm=N_SAMPLE, row0=N_PROMPT, rows=N_SAMPLE)
                        .reshape(DEC_BATCH, DEC_SEQ, d)[:, -1])
            w_o = rwkv_w_o[j].astype(BF16)
            x = _matmul(z_p, w_o, None, x, tm=TM_PROMPT, tn=d, res_row0=0)
            x = _matmul(z_s, w_o, None, x, tm=N_SAMPLE, tn=d, res_row0=N_PROMPT)
            wkv_p.append(s_p)
            wkv_s.append(s_s)
        if l + 1 < DEPTH:
            x, w_bf = _ffn(x, norm_ffn2[l], w_bf, (ffn_w, l + 1, 0))
        else:
            y_p, y_s = _ffn(x, norm_ffn2[l], w_bf, final_gain=norm_final)

    return (y_p.reshape(BATCH, SEQ, d), y_s.reshape(DEC_BATCH, DEC_SEQ, d),
            jnp.stack(pool_p), jnp.stack(pool_s),
            jnp.stack(wk_p), jnp.stack(wv_p), jnp.stack(wk_s), jnp.stack(wv_s),
            jnp.stack(sh_p), jnp.stack(sh_s),
            jnp.stack(wkv_p), jnp.stack(wkv_s))
```

```python
import functools
import math

import numpy as np
import jax
import jax.numpy as jnp
from jax import lax
from jax.experimental import pallas as pl
from jax.experimental.pallas import tpu as pltpu

F32 = jnp.float32
BF16 = jnp.bfloat16

D_MODEL = 2048
BATCH = 2
SEQ = 4096
DEPTH = 4
DEC_BATCH = 32
DEC_SEQ = 8
PAST_LEN = 16384
N_MIXERS = 3
RMS_EPS = 1e-6
D_FF = 5632
POOL_WINDOWS = (2, 4, 8, 16)
POOL_GROUPS = 4
POOL_GROUP_DIM = D_MODEL // POOL_GROUPS
POOL_BUF = max(POOL_WINDOWS) - 1
POOL_HALO = POOL_BUF + 1
POOL_PAD = 8
assert all(w & (w - 1) == 0 for w in POOL_WINDOWS) and max(POOL_WINDOWS) // 2 <= POOL_PAD
HEAD_DIM = 64
N_HEADS = D_MODEL // HEAD_DIM
N_KV_HEADS = 4
GQA_GROUP = N_HEADS // N_KV_HEADS
ATT_DIM = N_HEADS * HEAD_DIM
KV_DIM = N_KV_HEADS * HEAD_DIM
QKV_DIM = ATT_DIM + 2 * KV_DIM
WINDOW = 128
ATT_SCALE = HEAD_DIM ** -0.5
T5_BUCKETS = 32
T5_MAX_DISTANCE = 128
NEG_INF = -1e30
RWKV_N = 64
RWKV_HEADS = D_MODEL // RWKV_N
GN_EPS = 64e-5
LORA_PAD = 128

N_PROMPT = BATCH * SEQ
N_SAMPLE = DEC_BATCH * DEC_SEQ
N_TOK = N_PROMPT + N_SAMPLE

VMEM_LIMIT_BYTES = 56 * 1024 * 1024

TM_TOK = 704
TM_PROMPT = 512
TM_POOL = 256
SHIFT_HALO = 8
TF_FFN = 512
CAST_STEPS = 128
CAST_WINDOWS = ((64, 1408), (64, 1408), (176, 512))
TN_PROJ = 512
SCAN_CHUNK = 64
SCAN_HEADS = 2
SCAN_LANES = SCAN_HEADS * RWKV_N
SCAN_SAMPLE_SEQS = 4


def _params(n_axes):
    return pltpu.CompilerParams(dimension_semantics=("arbitrary",) * n_axes,
                                vmem_limit_bytes=VMEM_LIMIT_BYTES)


def _dot(a, b):
    return jnp.dot(a, b, preferred_element_type=F32)


def _dot_nt(a, b):
    return lax.dot_general(a, b, (((1,), (1,)), ((), ())), preferred_element_type=F32)


def _dot_tn(a, b):
    return lax.dot_general(a, b, (((0,), (0,)), ((), ())), preferred_element_type=F32)


def _round_robin(gens):
    live = list(gens)
    while live:
        still = []
        for gen in live:
            try:
                next(gen)
                still.append(gen)
            except StopIteration:
                pass
        live = still


def _rms_norm(x, g):
    return x * lax.rsqrt(jnp.mean(x * x, axis=-1, keepdims=True) + RMS_EPS) * g


def _norm_kernel(x_ref, g_ref, o_ref):
    o_ref[...] = _rms_norm(x_ref[...], g_ref[...])


def _norm_rows(x, g, *, tm, row0, rows):
    d = x.shape[1]
    off = row0 // tm
    return pl.pallas_call(
        _norm_kernel,
        grid=(rows // tm,),
        in_specs=[pl.BlockSpec((tm, d), lambda i: (i + off, 0)),
                  pl.BlockSpec((1, d), lambda i: (0, 0))],
        out_specs=pl.BlockSpec((tm, d), lambda i: (i, 0)),
        out_shape=jax.ShapeDtypeStruct((rows, d), F32),
        compiler_params=_params(1),
        name="rms_norm",
    )(x, g.reshape(1, d))


def _norm_tails(x, g):
    d = x.shape[1]
    blocks_per_seq = SEQ // POOL_HALO
    out = pl.pallas_call(
        _norm_kernel,
        grid=(BATCH,),
        in_specs=[pl.BlockSpec((POOL_HALO, d), lambda b: ((b + 1) * blocks_per_seq - 1, 0)),
                  pl.BlockSpec((1, d), lambda b: (0, 0))],
        out_specs=pl.BlockSpec((POOL_HALO, d), lambda b: (b, 0)),
        out_shape=jax.ShapeDtypeStruct((BATCH * POOL_HALO, d), F32),
        compiler_params=_params(1),
        name="rms_norm_tails",
    )(x, g.reshape(1, d))
    return out.reshape(BATCH, POOL_HALO, d)


def _ffn_kernel(*refs, cast_next):
    if cast_next:
        (x_ref, g_ref, wg_ref, wu_ref, wd_ref, ng_ref, nu_ref, nd_ref,
         o_ref, cg_ref, cu_ref, cd_ref, xn_ref) = refs
        acc_ref = o_ref
    else:
        x_ref, g_ref, wg_ref, wu_ref, wd_ref, fg_ref, yp_ref, ys_ref, xn_ref, acc_ref = refs
    i, j = pl.program_id(0), pl.program_id(1)

    @pl.when(j == 0)
    def _():
        xn_ref[...] = _rms_norm(x_ref[...], g_ref[...]).astype(BF16)
        acc_ref[...] = jnp.zeros_like(acc_ref)

    riders = ((ng_ref, cg_ref), (nu_ref, cu_ref), (nd_ref, cd_ref)) if cast_next else ()

    def convert(pairs):
        for src, dst in pairs:
            dst[...] = src[...].astype(BF16)

    xn = xn_ref[...]
    gate = _dot(xn, wg_ref[...])
    convert(riders[:2])
    up = _dot(xn, wu_ref[...])
    convert(riders[2:])
    h = (gate * jax.nn.sigmoid(gate) * up).astype(BF16)
    acc_ref[...] += _dot(h, wd_ref[...])

    @pl.when(j == pl.num_programs(1) - 1)
    def _():
        o = x_ref[...] + 0.5 * acc_ref[...]
        if cast_next:
            o_ref[...] = o
        else:
            y = _rms_norm(o, fg_ref[...])
            yp_ref[...] = y

            @pl.when(i == pl.num_programs(0) - 1)
            def _():
                ys_ref[...] = y[y.shape[0] - N_SAMPLE:]


def _ffn(x, g, w_bf, w_next=None, final_gain=None):
    m, d = x.shape
    tm, tf = TM_TOK, TF_FFN
    grid = (m // tm, D_FF // tf)
    in_specs = [pl.BlockSpec((tm, d), lambda i, j: (i, 0)),
                pl.BlockSpec((1, d), lambda i, j: (0, 0)),
                pl.BlockSpec((d, tf), lambda i, j: (0, j)),
                pl.BlockSpec((d, tf), lambda i, j: (0, j)),
                pl.BlockSpec((tf, d), lambda i, j: (j, 0))]
    args = [x, g.reshape(1, d), *w_bf]
    scratch = [pltpu.VMEM((tm, d), BF16)]
    if w_next is not None:
        (ng, nu, nd), layer, half = w_next
        out_specs = [pl.BlockSpec((tm, d), lambda i, j: (i, 0))]
        out_shape = [jax.ShapeDtypeStruct((m, d), F32)]
        assert grid[0] * grid[1] >= CAST_STEPS

        def block(n_col_blocks):
            def index(i, j):
                b = jnp.minimum(i * grid[1] + j, CAST_STEPS - 1)
                return b // n_col_blocks, b % n_col_blocks
            return index

        for w, (rows, cols) in zip((ng, nu, nd), CAST_WINDOWS):
            assert (w.shape[2] // rows) * (w.shape[3] // cols) == CAST_STEPS
            idx = block(w.shape[3] // cols)
            in_specs.append(pl.BlockSpec((None, None, rows, cols),
                                         lambda i, j, idx=idx: (layer, half, *idx(i, j))))
            out_specs.append(pl.BlockSpec((rows, cols), idx))
            out_shape.append(jax.ShapeDtypeStruct(w.shape[2:], BF16))
            args.append(w)
    else:
        assert m == N_TOK and N_TOK - (grid[0] - 1) * tm >= N_SAMPLE
        in_specs.append(pl.BlockSpec((1, d), lambda i, j: (0, 0)))
        args.append(final_gain.reshape(1, d))
        out_specs = [pl.BlockSpec((tm, d), lambda i, j: (i, 0)),
                     pl.BlockSpec((N_SAMPLE, d), lambda i, j: (0, 0))]
        out_shape = [jax.ShapeDtypeStruct((N_PROMPT, d), F32), jax.ShapeDtypeStruct((N_SAMPLE, d), F32)]
        scratch.append(pltpu.VMEM((tm, d), F32))
    out = pl.pallas_call(
        functools.partial(_ffn_kernel, cast_next=w_next is not None),
        grid=grid,
        in_specs=in_specs,
        out_specs=out_specs,
        out_shape=out_shape,
        scratch_shapes=scratch,
        compiler_params=_params(2),
        name="ffn_half_step",
    )(*args)
    return (out[0], tuple(out[1:])) if w_next is not None else tuple(out)


def _matmul_kernel(*refs, has_gain, has_bias, has_res):
    refs = list(refs)
    lhs_ref, w_ref = refs[0], refs[1]
    pos = 2
    g_ref = b_ref = res_ref = None
    if has_gain:
        g_ref = refs[pos]
        pos += 1
    if has_bias:
        b_ref = refs[pos]
        pos += 1
    if has_res:
        res_ref = refs[pos]
        pos += 1
    o_ref, lhs_bf_ref = refs[pos], refs[pos + 1]

    @pl.when(pl.program_id(1) == 0)
    def _():
        lhs = lhs_ref[...]
        if has_gain:
            lhs = _rms_norm(lhs, g_ref[...])
        lhs_bf_ref[...] = lhs.astype(BF16)

    acc = _dot(lhs_bf_ref[...], w_ref[...])
    if has_bias:
        acc = acc + b_ref[...]
    if has_res:
        acc = res_ref[...] + acc
    o_ref[...] = acc


def _matmul(lhs, w, bias=None, res=None, *, tm, tn, res_row0=0, gain=None):
    m, k = lhs.shape
    n = w.shape[1]
    off = res_row0 // tm
    in_specs = [pl.BlockSpec((tm, k), lambda i, j: (i, 0)),
                pl.BlockSpec((k, tn), lambda i, j: (0, j))]
    args = [lhs, w]
    if gain is not None:
        in_specs.append(pl.BlockSpec((1, k), lambda i, j: (0, 0)))
        args.append(gain.reshape(1, k))
    if bias is not None:
        in_specs.append(pl.BlockSpec((1, tn), lambda i, j: (0, j)))
        args.append(bias.reshape(1, n))
    aliases = {}
    if res is not None:
        in_specs.append(pl.BlockSpec((tm, tn), lambda i, j: (i + off, j)))
        aliases = {len(args): 0}
        args.append(res)
        out_shape = jax.ShapeDtypeStruct(res.shape, F32)
    else:
        out_shape = jax.ShapeDtypeStruct((m, n), F32)
    return pl.pallas_call(
        functools.partial(_matmul_kernel, has_gain=gain is not None, has_bias=bias is not None,
                          has_res=res is not None),
        grid=(m // tm, n // tn),
        in_specs=in_specs,
        out_specs=pl.BlockSpec((tm, tn), lambda i, j: (i + off, j)),
        out_shape=out_shape,
        scratch_shapes=[pltpu.VMEM((tm, k), BF16)],
        input_output_aliases=aliases,
        compiler_params=_params(2),
        name="matmul_bias_residual",
    )(*args)


def _pool_group_out(diff, gi, x_ref, w_ref, sc_ref, o_ref):
    c = slice(gi * POOL_GROUP_DIM, (gi + 1) * POOL_GROUP_DIM)
    out = _dot(diff.astype(BF16), w_ref[gi])
    o_ref[:, c] = x_ref[:, c] + out * sc_ref[:, c]


def _pool_prompt_kernel(x_ref, halo_ref, g_ref, w_ref, sc_ref, o_ref, ext_ref, lvl_ref,
                        *, tm, tiles_per_seq, n_tiles):
    i = pl.program_id(0)
    pad, top, rows = POOL_PAD, POOL_PAD + POOL_HALO, tm + POOL_HALO

    @pl.when(i < n_tiles)
    def _():
        t_in_seq = i % tiles_per_seq
        g = g_ref[...]
        ext_ref[0:pad, :] = jnp.zeros((pad, ext_ref.shape[1]), F32)
        ext_ref[pad:top, :] = jnp.where(t_in_seq == 0, 0.0, _rms_norm(halo_ref[...], g))
        ext_ref[top:, :] = _rms_norm(x_ref[...], g)
        lvl_ref[0:pad, :] = jnp.zeros((pad, lvl_ref.shape[1]), F32)
        pos = t_in_seq * tm + lax.broadcasted_iota(jnp.int32, (tm, 1), 0)
        for gi, w in enumerate(POOL_WINDOWS):
            c = slice(gi * POOL_GROUP_DIM, (gi + 1) * POOL_GROUP_DIM)
            part = ext_ref[pl.ds(pad, rows), c] + ext_ref[pl.ds(pad - 1, rows), c]
            span = 2
            while span < w:
                lvl_ref[pl.ds(pad, rows), :] = part
                part = part + lvl_ref[pl.ds(pad - span, rows), :]
                span *= 2
            cur = ext_ref[pl.ds(top, tm), c]
            cnt = jnp.minimum(w, pos + 1).astype(F32)
            _pool_group_out(part[POOL_HALO:] / cnt - cur, gi, x_ref, w_ref, sc_ref, o_ref)

    @pl.when(i >= n_tiles)
    def _():
        o_ref[...] = x_ref[...]


def _pool_prompt(x, gain, w_pool, scale):
    d = x.shape[1]
    tm = TM_POOL
    halo_blocks = tm // POOL_HALO
    return pl.pallas_call(
        functools.partial(_pool_prompt_kernel, tm=tm, tiles_per_seq=SEQ // tm, n_tiles=N_PROMPT // tm),
        grid=(N_TOK // tm,),
        in_specs=[pl.BlockSpec((tm, d), lambda i: (i, 0)),
                  pl.BlockSpec((POOL_HALO, d), lambda i: (jnp.maximum(i * halo_blocks - 1, 0), 0)),
                  pl.BlockSpec((1, d), lambda i: (0, 0)),
                  pl.BlockSpec((POOL_GROUPS, POOL_GROUP_DIM, POOL_GROUP_DIM), lambda i: (0, 0, 0)),
                  pl.BlockSpec((1, d), lambda i: (0, 0))],
        out_specs=pl.BlockSpec((tm, d), lambda i: (i, 0)),
        out_shape=jax.ShapeDtypeStruct(x.shape, F32),
        scratch_shapes=[pltpu.VMEM((POOL_PAD + POOL_HALO + tm, d), F32),
                        pltpu.VMEM((POOL_PAD + POOL_HALO + tm, POOL_GROUP_DIM), F32)],
        compiler_params=_params(1),
        name="pool_prompt",
    )(x, x, gain.reshape(1, d), w_pool, scale.reshape(1, d))


def _pool_sample_kernel(ext_ref, x_ref, w_ref, sc_ref, o_ref):
    for gi, w in enumerate(POOL_WINDOWS):
        c = slice(gi * POOL_GROUP_DIM, (gi + 1) * POOL_GROUP_DIM)
        cur = ext_ref[:, pl.ds(POOL_HALO, DEC_SEQ), c]
        s = cur
        for back in range(1, w):
            s = s + ext_ref[:, pl.ds(POOL_HALO - back, DEC_SEQ), c]
        cnt = min(w, PAST_LEN + 1)
        diff = (s / float(cnt) - cur).reshape(N_SAMPLE, POOL_GROUP_DIM)
        _pool_group_out(diff, gi, x_ref, w_ref, sc_ref, o_ref)


def _pool_sample(ext, x, w_pool, scale):
    d = x.shape[1]
    blk = N_PROMPT // N_SAMPLE
    return pl.pallas_call(
        _pool_sample_kernel,
        grid=(1,),
        in_specs=[pl.BlockSpec(ext.shape, lambda i: (0, 0, 0)),
                  pl.BlockSpec((N_SAMPLE, d), lambda i: (blk, 0)),
                  pl.BlockSpec((POOL_GROUPS, POOL_GROUP_DIM, POOL_GROUP_DIM), lambda i: (0, 0, 0)),
                  pl.BlockSpec((1, d), lambda i: (0, 0))],
        out_specs=pl.BlockSpec((N_SAMPLE, d), lambda i: (blk, 0)),
        out_shape=jax.ShapeDtypeStruct(x.shape, F32),
        input_output_aliases={1: 0},
        compiler_params=_params(1),
        name="pool_sample",
    )(ext, x, w_pool, scale.reshape(1, d))


def _t5_bucket_table(tq):
    qi = np.arange(tq)[:, None]
    kj = np.arange(WINDOW + tq)[None, :]
    dist = qi + WINDOW - kj
    exact = T5_BUCKETS // 2
    ratio = np.log(np.maximum(dist, 1) / exact) / math.log(T5_MAX_DISTANCE / exact)
    large = np.minimum(exact + (ratio * (T5_BUCKETS - exact)).astype(np.int64), T5_BUCKETS - 1)
    bucket = np.where(dist < exact, dist, large)
    valid = (dist >= 0) & (dist < WINDOW)
    table = np.where(valid, bucket, -1).astype(np.int32)
    return table[:, :WINDOW], table[:, WINDOW:]


def _attn_sample_kernel(rb_ref, sink_ref, bkt_p_ref, bkt_c_ref, q_ref, kc_ref, vc_ref, kp_ref, vp_ref,
                        o_ref, bias_p_ref, bias_c_ref, sink_col_ref, *, tq):
    step = pl.program_id(0)

    @pl.when(step == 0)
    def _():
        bkt_p = bkt_p_ref[...]
        bkt_c = bkt_c_ref[...]
        bias_p_ref[...] = jnp.zeros_like(bias_p_ref)
        bias_c_ref[...] = jnp.zeros_like(bias_c_ref)

        def add_bucket(b, carry):
            eq_p = bkt_p == b
            eq_c = bkt_c == b
            for h in range(N_HEADS):
                kvh, g = divmod(h, GQA_GROUP)
                r = slice(g * tq, (g + 1) * tq)
                val = rb_ref[b, h]
                bias_p_ref[kvh, r, :] += jnp.where(eq_p, val, 0.0)
                bias_c_ref[kvh, r, :] += jnp.where(eq_c, val, 0.0)
            return carry

        lax.fori_loop(0, T5_BUCKETS, add_bucket, 0)
        for h in range(N_HEADS):
            kvh, g = divmod(h, GQA_GROUP)
            r = slice(g * tq, (g + 1) * tq)
            bias_p_ref[kvh, r, :] = jnp.where(bkt_p < 0, NEG_INF, bias_p_ref[kvh, r, :])
            bias_c_ref[kvh, r, :] = jnp.where(bkt_c < 0, NEG_INF, bias_c_ref[kvh, r, :])
            sink_col_ref[kvh, r, :] = jnp.full((tq, 1), sink_ref[h], F32)

    def kv_head(kvh):
        heads = [kvh * GQA_GROUP + g for g in range(GQA_GROUP)]
        qs = jnp.concatenate([q_ref[:, h * HEAD_DIM:(h + 1) * HEAD_DIM] for h in heads], axis=0).astype(BF16)
        c = slice(kvh * HEAD_DIM, (kvh + 1) * HEAD_DIM)
        qk_p = _dot_nt(qs, kp_ref[:, c].astype(BF16))
        qk_c = _dot_nt(qs, kc_ref[:, c].astype(BF16))
        yield
        s_p = qk_p * ATT_SCALE + bias_p_ref[kvh]
        s_c = qk_c * ATT_SCALE + bias_c_ref[kvh]
        sink = sink_col_ref[kvh]
        m = jnp.maximum(jnp.maximum(jnp.max(s_p, axis=-1, keepdims=True),
                                    jnp.max(s_c, axis=-1, keepdims=True)), sink)
        p_p = jnp.exp(s_p - m)
        p_c = jnp.exp(s_c - m)
        den = (jnp.sum(p_p, axis=-1, keepdims=True) + jnp.sum(p_c, axis=-1, keepdims=True)
               + jnp.exp(sink - m))
        o_p = _dot(p_p.astype(BF16), vp_ref[:, c].astype(BF16))
        o_c = _dot(p_c.astype(BF16), vc_ref[:, c].astype(BF16))
        yield
        o = (o_p + o_c) / den
        for g, h in enumerate(heads):
            o_ref[:, h * HEAD_DIM:(h + 1) * HEAD_DIM] = o[g * tq:(g + 1) * tq, :]

    _round_robin([kv_head(kvh) for kvh in range(N_KV_HEADS)])


def _attention_sample(qkv, rel_bias, sinks, k_prev, v_prev):
    tq = DEC_SEQ
    bkt_p, bkt_c = _t5_bucket_table(tq)
    off = N_PROMPT // tq
    kcol = ATT_DIM // KV_DIM
    smem = pl.BlockSpec(memory_space=pltpu.SMEM)
    cache = pl.BlockSpec((None, WINDOW, KV_DIM), lambda i: (i, 0, 0))
    rows = GQA_GROUP * tq
    return pl.pallas_call(
        functools.partial(_attn_sample_kernel, tq=tq),
        grid=(DEC_BATCH,),
        in_specs=[smem, smem,
                  pl.BlockSpec((tq, WINDOW), lambda i: (0, 0)),
                  pl.BlockSpec((tq, tq), lambda i: (0, 0)),
                  pl.BlockSpec((tq, ATT_DIM), lambda i: (i + off, 0)),
                  pl.BlockSpec((tq, KV_DIM), lambda i: (i + off, kcol)),
                  pl.BlockSpec((tq, KV_DIM), lambda i: (i + off, kcol + 1)),
                  cache, cache],
        out_specs=pl.BlockSpec((tq, ATT_DIM), lambda i: (i, 0)),
        out_shape=jax.ShapeDtypeStruct((N_SAMPLE, ATT_DIM), F32),
        scratch_shapes=[pltpu.VMEM((N_KV_HEADS, rows, WINDOW), F32),
                        pltpu.VMEM((N_KV_HEADS, rows, tq), F32),
                        pltpu.VMEM((N_KV_HEADS, rows, 1), F32)],
        compiler_params=_params(1),
        name="swa_attention_sample",
    )(rel_bias, sinks, jnp.asarray(bkt_p), jnp.asarray(bkt_c), qkv, qkv, qkv, k_prev, v_prev)


def _attn_prompt_kernel(rb_ref, sink_ref, bkt_ref, q_ref, kc_ref, vc_ref, kp_ref, vp_ref, o_ref, bias_ref,
                        *, blocks_per_seq):
    step = pl.program_id(0)
    tq, nk, pair = WINDOW, 2 * WINDOW, 2 * HEAD_DIM

    @pl.when(step == 0)
    def _():
        bkt = bkt_ref[...]
        bias_ref[...] = jnp.zeros_like(bias_ref)

        def add_bucket(b, carry):
            eq = bkt == b
            for h in range(N_HEADS):
                bias_ref[h] += jnp.where(eq, rb_ref[b, h], 0.0)
            return carry

        lax.fori_loop(0, T5_BUCKETS, add_bucket, 0)
        for h in range(N_HEADS):
            bias_ref[h] = jnp.where(bkt < 0, NEG_INF, bias_ref[h])

    no_prev = step % blocks_per_seq == 0
    dead = jnp.logical_and(no_prev, lax.broadcasted_iota(jnp.int32, (nk, tq), 0) < WINDOW)
    lane_half = lax.broadcasted_iota(jnp.int32, (nk, pair), 1) // HEAD_DIM
    row_half = lax.broadcasted_iota(jnp.int32, (pair, tq), 0) // HEAD_DIM

    for kvh in range(N_KV_HEADS):
        tile = slice((kvh // 2) * pair, (kvh // 2 + 1) * pair)
        half = kvh % 2

        def both_halves(prev_ref, cur_ref):
            x = jnp.concatenate([prev_ref[:, tile], cur_ref[:, tile]], axis=0)
            own = jnp.where(lane_half == half, x, 0.0)
            other = pltpu.roll(own, HEAD_DIM, axis=1)
            return (own, other) if half == 0 else (other, own)

        k_lo, k_hi = both_halves(kp_ref, kc_ref)
        v_lo, v_hi = both_halves(vp_ref, vc_ref)
        k2 = jnp.concatenate([k_lo, k_hi], axis=0).astype(BF16)
        v_lo_t = v_lo.T.astype(BF16)
        v_hi_t = v_hi.T.astype(BF16)

        def head_pair(j):
            cols = slice(j * pair, (j + 1) * pair)
            qp = (q_ref[:, cols] * ATT_SCALE).astype(BF16)
            s2 = _dot_nt(k2, qp)
            yield
            probs, dens = [], []
            for t in range(2):
                h = 2 * j + t
                s = s2[t * nk:(t + 1) * nk] + bias_ref[h]
                s = jnp.where(dead, NEG_INF, s)
                sink = sink_ref[h]
                m = jnp.maximum(jnp.max(s, axis=0, keepdims=True), sink)
                p = jnp.exp(s - m)
                dens.append(jnp.sum(p, axis=0, keepdims=True) + jnp.exp(sink - m))
                probs.append(p.astype(BF16))
            o2 = _dot(v_lo_t, probs[0]) + _dot(v_hi_t, probs[1])
            yield
            o2 = o2 / jnp.where(row_half == 0, dens[0], dens[1])
            o_ref[:, cols] = o2.T

        pairs_per_kv = GQA_GROUP // 2
        _round_robin([head_pair(kvh * pairs_per_kv + jj) for jj in range(pairs_per_kv)])


def _attention_prompt(qkv, rel_bias, sinks):
    bkt_p, bkt_c = _t5_bucket_table(WINDOW)
    bkt_t = np.ascontiguousarray(np.concatenate([bkt_p, bkt_c], axis=1).T)
    kcol = ATT_DIM // KV_DIM
    smem = pl.BlockSpec(memory_space=pltpu.SMEM)
    prev = lambda i: jnp.maximum(i - 1, 0)
    return pl.pallas_call(
        functools.partial(_attn_prompt_kernel, blocks_per_seq=SEQ // WINDOW),
        grid=(N_PROMPT // WINDOW,),
        in_specs=[smem, smem,
                  pl.BlockSpec((2 * WINDOW, WINDOW), lambda i: (0, 0)),
                  pl.BlockSpec((WINDOW, ATT_DIM), lambda i: (i, 0)),
                  pl.BlockSpec((WINDOW, KV_DIM), lambda i: (i, kcol)),
                  pl.BlockSpec((WINDOW, KV_DIM), lambda i: (i, kcol + 1)),
                  pl.BlockSpec((WINDOW, KV_DIM), lambda i: (prev(i), kcol)),
                  pl.BlockSpec((WINDOW, KV_DIM), lambda i: (prev(i), kcol + 1))],
        out_specs=pl.BlockSpec((WINDOW, ATT_DIM), lambda i: (i, 0)),
        out_shape=jax.ShapeDtypeStruct((N_PROMPT, ATT_DIM), F32),
        scratch_shapes=[pltpu.VMEM((N_HEADS, 2 * WINDOW, WINDOW), F32)],
        compiler_params=_params(1),
        name="swa_attention_prompt",
    )(rel_bias, sinks, jnp.asarray(bkt_t), qkv, qkv, qkv, qkv, qkv)


def _rwkv_proj_kernel(*refs, tm, seq_len, has_first):
    refs = list(refs)
    x_ref, halo_ref, gain_ref = refs[:3]
    pos = 3
    first_ref = None
    if has_first:
        first_ref = refs[pos]
        pos += 1
    (mu_ref, wr_ref, wk_ref, wv_ref, w1_ref, w2_ref, w0_ref, a1_ref, a2_ref, a0_ref, g1_ref, g2_ref,
     r_ref, k_ref, v_ref, ld_ref, a_ref, g_ref,
     ext_ref, xr_ref, xk_ref, xv_ref, hw_ref, ha_ref, hg_ref) = refs[pos:]

    @pl.when(pl.program_id(1) == 0)
    def _():
        gain = gain_ref[...]
        u = _rms_norm(x_ref[...], gain)
        ext_ref[0:SHIFT_HALO, :] = _rms_norm(halo_ref[...], gain)
        ext_ref[SHIFT_HALO:, :] = u
        prev = ext_ref[pl.ds(SHIFT_HALO - 1, tm), :]
        row = pl.program_id(0) * tm + lax.broadcasted_iota(jnp.int32, (tm, 1), 0)
        starts = row % seq_len == 0
        prev = jnp.where(starts, first_ref[...] if has_first else 0.0, prev)
        dx = prev - u
        mix = lambda i: (u + dx * mu_ref[i:i + 1, :]).astype(BF16)
        xr_ref[...] = mix(0)
        hw_ref[...] = jnp.tanh(_dot(mix(1), w1_ref[...])).astype(BF16)
        xk_ref[...] = mix(2)
        xv_ref[...] = mix(3)
        ha_ref[...] = _dot(mix(4), a1_ref[...]).astype(BF16)
        hg_ref[...] = jax.nn.sigmoid(_dot(mix(5), g1_ref[...])).astype(BF16)

    r_ref[...] = _dot(xr_ref[...], wr_ref[...])
    k_ref[...] = _dot(xk_ref[...], wk_ref[...])
    v_ref[...] = _dot(xv_ref[...], wv_ref[...])
    z = -(w0_ref[...] + _dot(hw_ref[...], w2_ref[...]))
    softplus = jnp.maximum(z, 0.0) + jnp.log1p(jnp.exp(-jnp.abs(z)))
    ld_ref[...] = -jnp.exp(-softplus - 0.5)
    a_ref[...] = jax.nn.sigmoid(a0_ref[...] + _dot(ha_ref[...], a2_ref[...]))
    g_ref[...] = _dot(hg_ref[...], g2_ref[...])


def _rwkv_proj(x, gain, weights, *, row0, rows, tm, seq_len, first=None):
    mu, w_r, w_k, w_v, w1, w2, w0, a1, a2, a0, g1, g2 = weights
    d = x.shape[1]
    tn = TN_PROJ
    gl = g1.shape[1]
    off = row0 // tm
    halo_blocks = tm // SHIFT_HALO
    halo_off = row0 // SHIFT_HALO
    row = lambda i, j: (i + off, 0)
    col = lambda i, j: (0, j)
    fixed = lambda i, j: (0, 0)
    in_specs = [pl.BlockSpec((tm, d), row),
                pl.BlockSpec((SHIFT_HALO, d), lambda i, j: (jnp.maximum(halo_off + i * halo_blocks - 1, 0), 0)),
                pl.BlockSpec((1, d), fixed)]
    args = [x, x, gain.reshape(1, d)]
    if first is not None:
        in_specs.append(pl.BlockSpec((tm, d), lambda i, j: (i, 0)))
        args.append(first)
    in_specs += [pl.BlockSpec((6, d), fixed),
                 pl.BlockSpec((d, tn), col), pl.BlockSpec((d, tn), col), pl.BlockSpec((d, tn), col),
                 pl.BlockSpec((d, LORA_PAD), fixed), pl.BlockSpec((LORA_PAD, tn), col), pl.BlockSpec((1, tn), col),
                 pl.BlockSpec((d, LORA_PAD), fixed), pl.BlockSpec((LORA_PAD, tn), col), pl.BlockSpec((1, tn), col),
                 pl.BlockSpec((d, gl), fixed), pl.BlockSpec((gl, tn), col)]
    args += [mu, w_r, w_k, w_v, w1, w2, w0.reshape(1, d), a1, a2, a0.reshape(1, d), g1, g2]
    out = jax.ShapeDtypeStruct((rows, d), F32)
    return pl.pallas_call(
        functools.partial(_rwkv_proj_kernel, tm=tm, seq_len=seq_len, has_first=first is not None),
        grid=(rows // tm, d // tn),
        in_specs=in_specs,
        out_specs=[pl.BlockSpec((tm, tn), lambda i, j: (i, j))] * 6,
        out_shape=[out] * 6,
        scratch_shapes=[pltpu.VMEM((tm + SHIFT_HALO, d), F32)] + [pltpu.VMEM((tm, d), BF16)] * 3
        + [pltpu.VMEM((tm, LORA_PAD), BF16)] * 2 + [pltpu.VMEM((tm, gl), BF16)],
        compiler_params=_params(2),
        name="rwkv_projections",
    )(*args)


def _scan_kernel(*refs, chunk, has_init):
    (r_ref, k_ref, v_ref, ld_ref, a_ref, g_ref, kk_ref, ka_ref, rk_ref, lnw_ref, lnb_ref) = refs[:11]
    refs = refs[11:]
    if has_init:
        s0_ref, refs = refs[0], refs[1:]
    z_ref, sout_ref, s_ref = refs
    c_idx = pl.program_id(1)
    cc, n, w = chunk, RWKV_N, SCAN_LANES
    hc = SCAN_HEADS * cc
    n_groups = RWKV_HEADS // SCAN_HEADS
    n_sb = z_ref.shape[0]

    @pl.when(c_idx == 0)
    def _():
        s_ref[...] = jnp.zeros_like(s_ref)
        if has_init:
            for sb in range(n_sb):
                for h in range(RWKV_HEADS):
                    gi, hh = divmod(h, SCAN_HEADS)
                    s_ref[sb, gi, hh * n:(hh + 1) * n, hh * n:(hh + 1) * n] = s0_ref[sb, h]

    def iota(shape, axis):
        return lax.broadcasted_iota(jnp.int32, shape, axis)

    head_diag = iota((w, w), 0) // n == iota((w, w), 1) // n
    tril = (iota((cc, cc), 0) >= iota((cc, cc), 1)).astype(BF16)
    t_row = iota((cc, hc), 0)
    s_col = iota((cc, hc), 1) % cc
    strict = t_row > s_col
    causal = t_row >= s_col
    lanes_diag = iota((hc, w), 0) // cc == iota((hc, w), 1) // n
    tokens_diag = iota((hc, hc), 0) // cc == iota((hc, hc), 1) // cc
    lane_head = iota((cc, w), 1) // n

    def block_diag(x, diag):
        return jnp.where(diag, jnp.concatenate([x] * SCAN_HEADS, axis=0), 0.0).astype(BF16)

    def bd_lanes(x):
        return block_diag(x, lanes_diag)

    def bd_tokens(x):
        return block_diag(x, tokens_diag)

    def head_sum(x):
        out = jnp.zeros_like(x)
        for hh in range(SCAN_HEADS):
            mine = lane_head == hh
            out = jnp.where(mine, jnp.sum(jnp.where(mine, x, 0.0), axis=-1, keepdims=True), out)
        return out

    def group(sb, gi):
        c = slice(gi * w, (gi + 1) * w)
        r, k, v, ld, a = r_ref[sb, :, c], k_ref[sb, :, c], v_ref[sb, :, c], ld_ref[sb, :, c], a_ref[sb, :, c]

        p1 = ld.astype(BF16)
        r1 = ld - p1.astype(F32)
        p2 = r1.astype(BF16)
        p3 = (r1 - p2.astype(F32)).astype(BF16)
        cum = _dot(tril, jnp.concatenate([p1, p2, p3], axis=1))
        yield
        kk = k * kk_ref[:, c]
        kmod = k * (1.0 + (a - 1.0) * ka_ref[:, c])
        kap = kk / jnp.maximum(jnp.sqrt(head_sum(kk * kk)), 1e-12)
        b = kap * a

        lcum = cum[:, :w] + cum[:, w:2 * w] + cum[:, 2 * w:]
        lend = lcum[cc - 1:cc, :]
        e_inc = jnp.exp(lcum)
        e_exc = jnp.exp(lcum - ld)
        e_neg = jnp.exp(-lcum)
        e_end = jnp.exp(lend - lcum)

        kq = kap * e_exc
        rq = r * e_inc
        qr = jnp.concatenate([kq, rq], axis=0).astype(BF16)
        bk_bd = jnp.concatenate([bd_lanes(b * e_neg), bd_lanes(kmod * e_neg)], axis=0)
        gram = _dot_nt(qr, bk_bd)
        yield
        s_bd = s_ref[sb, gi]
        qs = _dot_nt(qr, s_bd.astype(BF16))
        yield
        a_w = jnp.where(strict, gram[:cc, :hc], 0.0)
        bk_w = jnp.where(strict, gram[:cc, hc:], 0.0)
        cb_w = jnp.where(causal, gram[cc:, :hc], 0.0)
        ck_w = jnp.where(causal, gram[cc:, hc:], 0.0)
        v_bd = bd_lanes(v)

        bv = _dot(bk_w.astype(BF16), v_bd)
        yield
        x = -(qs[:cc] + bv)
        ax = _dot(a_w.astype(BF16), bd_lanes(x))
        yield
        x = x - ax
        a_pow = a_w
        power = 2
        while power < cc:
            a_pow = _dot(a_pow.astype(BF16), bd_tokens(a_pow))
            yield
            ax = _dot(a_pow.astype(BF16), bd_lanes(x))
            yield
            x = x + ax
            power *= 2
        u = x

        y_in = _dot(jnp.concatenate([cb_w, ck_w], axis=1).astype(BF16),
                    jnp.concatenate([bd_lanes(u), v_bd], axis=0))
        yield
        upd = _dot_tn(jnp.concatenate([u, v], axis=0).astype(BF16),
                      jnp.concatenate([b * e_end, kmod * e_end], axis=0).astype(BF16))
        yield
        s_ref[sb, gi] = s_bd * jnp.exp(lend) + jnp.where(head_diag, upd, 0.0)

        y = qs[cc:] + y_in
        mean = head_sum(y) * (1.0 / n)
        dev = y - mean
        var = head_sum(dev * dev) * (1.0 / n)
        yn = dev * lax.rsqrt(var + GN_EPS) * lnw_ref[:, c] + lnb_ref[:, c]
        bonus = head_sum(r * kmod * rk_ref[:, c]) * v
        z_ref[sb, :, c] = (yn + bonus) * g_ref[sb, :, c]

    _round_robin([group(sb, gi) for sb in range(n_sb) for gi in range(n_groups)])

    @pl.when(c_idx == pl.num_programs(1) - 1)
    def _():
        for sb in range(n_sb):
            for h in range(RWKV_HEADS):
                gi, hh = divmod(h, SCAN_HEADS)
                sout_ref[sb, h] = s_ref[sb, gi, hh * n:(hh + 1) * n, hh * n:(hh + 1) * n]


def _rwkv_scan(proj, k_k, k_a, r_k, ln_w, ln_b, *, n_seq, per_step, chunk, n_chunks, s0=None):
    d = D_MODEL
    t = chunk * n_chunks
    tok = pl.BlockSpec((per_step, chunk, d), lambda b, c: (b, c, 0))
    par = pl.BlockSpec((1, d), lambda b, c: (0, 0))
    state = pl.BlockSpec((per_step, RWKV_HEADS, RWKV_N, RWKV_N), lambda b, c: (b, 0, 0, 0))
    in_specs = [tok] * 6 + [par] * 5
    args = [p.reshape(n_seq, t, d) for p in proj] + [p.reshape(1, d) for p in (k_k, k_a, r_k, ln_w, ln_b)]
    if s0 is not None:
        in_specs.append(state)
        args.append(s0)
    n_groups = RWKV_HEADS // SCAN_HEADS
    z, s_out = pl.pallas_call(
        functools.partial(_scan_kernel, chunk=chunk, has_init=s0 is not None),
        grid=(n_seq // per_step, n_chunks),
        in_specs=in_specs,
        out_specs=[tok, state],
        out_shape=[jax.ShapeDtypeStruct((n_seq, t, d), F32),
                   jax.ShapeDtypeStruct((n_seq, RWKV_HEADS, RWKV_N, RWKV_N), F32)],
        scratch_shapes=[pltpu.VMEM((per_step, n_groups, SCAN_LANES, SCAN_LANES), F32)],
        compiler_params=_params(2),
        name="rwkv_scan",
    )(*args)
    return z.reshape(n_seq * t, d), s_out


def _pad_cols(w):
    return jnp.pad(w, ((0, 0), (0, LORA_PAD - w.shape[1])))


def _pad_rows(w):
    return jnp.pad(w, ((0, LORA_PAD - w.shape[0]), (0, 0)))


def kernel(x_prompt, x_sample, state_pool, cache_win_k, cache_win_v, state_shift, state_wkv, norm_ffn1, norm_mix, norm_ffn2, norm_final, ffn_w_gate, ffn_w_up, ffn_w_down, pool_w, pool_scale, att_w_qkv, att_b_qkv, att_w_o, att_b_o, att_sinks, rel_bias, rwkv_mu, rwkv_w_r, rwkv_w_k, rwkv_w_v, rwkv_w_o, rwkv_w0, rwkv_w1, rwkv_w2, rwkv_a0, rwkv_a1, rwkv_a2, rwkv_g1, rwkv_g2, rwkv_k_k, rwkv_k_a, rwkv_r_k, rwkv_ln_w, rwkv_ln_b):
    d = D_MODEL
    x = jnp.concatenate([x_prompt.reshape(N_PROMPT, d), x_sample.reshape(N_SAMPLE, d)], axis=0)
    ffn_w = (ffn_w_gate, ffn_w_up, ffn_w_down)
    w_bf = tuple(w[0, 0].astype(BF16) for w in ffn_w)
    pool_p, pool_s, wk_p, wv_p, wk_s, wv_s, sh_p, sh_s, wkv_p, wkv_s = ([] for _ in range(10))

    for l in range(DEPTH):
        j, kind = divmod(l, N_MIXERS)
        x, w_bf = _ffn(x, norm_ffn1[l], w_bf, (ffn_w, l, 1))
        gain = norm_mix[l]
        if kind == 0:
            w_pool = pool_w[j].astype(BF16)
            u_s = _norm_rows(x, gain, tm=N_SAMPLE, row0=N_PROMPT, rows=N_SAMPLE).reshape(DEC_BATCH, DEC_SEQ, d)
            ext = jnp.concatenate([jnp.zeros((DEC_BATCH, 1, d), F32), state_pool[j], u_s], axis=1)
            pool_p.append(_norm_tails(x, gain)[:, -POOL_BUF:])
            pool_s.append(ext[:, -POOL_BUF:])
            x = _pool_prompt(x, gain, w_pool, pool_scale[j])
            x = _pool_sample(ext, x, w_pool, pool_scale[j])
        elif kind == 1:
            qkv = _matmul(x, att_w_qkv[j].astype(BF16), att_b_qkv[j], tm=TM_TOK, tn=QKV_DIM, gain=gain)
            k_buf = cache_win_k[j].reshape(DEC_BATCH, WINDOW, KV_DIM)
            v_buf = cache_win_v[j].reshape(DEC_BATCH, WINDOW, KV_DIM)
            o_p = _attention_prompt(qkv, rel_bias, att_sinks[j])
            o_s = _attention_sample(qkv, rel_bias, att_sinks[j], k_buf, v_buf)
            w_o = att_w_o[j].astype(BF16)
            x = _matmul(o_p, w_o, att_b_o[j], x, tm=TM_PROMPT, tn=d, res_row0=0)
            x = _matmul(o_s, w_o, att_b_o[j], x, tm=N_SAMPLE, tn=d, res_row0=N_PROMPT)
            kv_shape = (WINDOW, N_KV_HEADS, HEAD_DIM)
            k_cols = slice(ATT_DIM, ATT_DIM + KV_DIM)
            v_cols = slice(ATT_DIM + KV_DIM, QKV_DIM)
            tails = [slice((b + 1) * SEQ - WINDOW, (b + 1) * SEQ) for b in range(BATCH)]
            wk_p.append(jnp.stack([qkv[t, k_cols] for t in tails]).reshape(BATCH, *kv_shape))
            wv_p.append(jnp.stack([qkv[t, v_cols] for t in tails]).reshape(BATCH, *kv_shape))
            new_rows = slice(N_PROMPT, N_TOK)
            k_s = jnp.concatenate([k_buf, qkv[new_rows, k_cols].reshape(DEC_BATCH, DEC_SEQ, KV_DIM)], axis=1)
            v_s = jnp.concatenate([v_buf, qkv[new_rows, v_cols].reshape(DEC_BATCH, DEC_SEQ, KV_DIM)], axis=1)
            wk_s.append(k_s[:, -WINDOW:].reshape(DEC_BATCH, *kv_shape))
            wv_s.append(v_s[:, -WINDOW:].reshape(DEC_BATCH, *kv_shape))
        else:
            weights = (rwkv_mu[j],
                       rwkv_w_r[j].astype(BF16), rwkv_w_k[j].astype(BF16), rwkv_w_v[j].astype(BF16),
                       _pad_cols(rwkv_w1[j]).astype(BF16), _pad_rows(rwkv_w2[j]).astype(BF16), rwkv_w0[j],
                       _pad_cols(rwkv_a1[j]).astype(BF16), _pad_rows(rwkv_a2[j]).astype(BF16), rwkv_a0[j],
                       rwkv_g1[j].astype(BF16), rwkv_g2[j].astype(BF16))
            first_s = jnp.repeat(state_shift[j], DEC_SEQ, axis=0)
            proj_p = _rwkv_proj(x, gain, weights, row0=0, rows=N_PROMPT, tm=TM_PROMPT, seq_len=SEQ)
            proj_s = _rwkv_proj(x, gain, weights, row0=N_PROMPT, rows=N_SAMPLE, tm=N_SAMPLE, seq_len=DEC_SEQ,
                                first=first_s)
            head_params = (rwkv_k_k[j], rwkv_k_a[j], rwkv_r_k[j], rwkv_ln_w[j], rwkv_ln_b[j])
            z_p, s_p = _rwkv_scan(proj_p, *head_params, n_seq=BATCH, per_step=BATCH, chunk=SCAN_CHUNK,
                                  n_chunks=SEQ // SCAN_CHUNK)
            z_s, s_s = _rwkv_scan(proj_s, *head_params, n_seq=DEC_BATCH, per_step=SCAN_SAMPLE_SEQS, chunk=DEC_SEQ,
                                  n_chunks=1, s0=state_wkv[j])
            sh_p.append(_norm_tails(x, gain)[:, -1])
            sh_s.append(_norm_rows(x, gain, tm=N_SAMPLE, row0=N_PROMPT, rows=N_SAMPLE)
                        .reshape(DEC_BATCH, DEC_SEQ, d)[:, -1])
            w_o = rwkv_w_o[j].astype(BF16)
            x = _matmul(z_p, w_o, None, x, tm=TM_PROMPT, tn=d, res_row0=0)
            x = _matmul(z_s, w_o, None, x, tm=N_SAMPLE, tn=d, res_row0=N_PROMPT)
            wkv_p.append(s_p)
            wkv_s.append(s_s)
        if l + 1 < DEPTH:
            x, w_bf = _ffn(x, norm_ffn2[l], w_bf, (ffn_w, l + 1, 0))
        else:
            y_p, y_s = _ffn(x, norm_ffn2[l], w_bf, final_gain=norm_final)

    return (y_p.reshape(BATCH, SEQ, d), y_s.reshape(DEC_BATCH, DEC_SEQ, d),
            jnp.stack(pool_p), jnp.stack(pool_s),
            jnp.stack(wk_p), jnp.stack(wv_p), jnp.stack(wk_s), jnp.stack(wv_s),
            jnp.stack(sh_p), jnp.stack(sh_s),
            jnp.stack(wkv_p), jnp.stack(wkv_s))
```

```python
import functools
import math

import numpy as np
import jax
import jax.numpy as jnp
from jax import lax
from jax.experimental import pallas as pl
from jax.experimental.pallas import tpu as pltpu

F32 = jnp.float32
BF16 = jnp.bfloat16

D_MODEL = 2048
BATCH = 2
SEQ = 4096
DEPTH = 4
DEC_BATCH = 32
DEC_SEQ = 8
PAST_LEN = 16384
N_MIXERS = 3
RMS_EPS = 1e-6
D_FF = 5632
POOL_WINDOWS = (2, 4, 8, 16)
POOL_GROUPS = 4
POOL_GROUP_DIM = D_MODEL // POOL_GROUPS
POOL_BUF = max(POOL_WINDOWS) - 1
POOL_HALO = POOL_BUF + 1
POOL_PAD = 8
assert all(w & (w - 1) == 0 for w in POOL_WINDOWS) and max(POOL_WINDOWS) // 2 <= POOL_PAD
HEAD_DIM = 64
N_HEADS = D_MODEL // HEAD_DIM
N_KV_HEADS = 4
GQA_GROUP = N_HEADS // N_KV_HEADS
ATT_DIM = N_HEADS * HEAD_DIM
KV_DIM = N_KV_HEADS * HEAD_DIM
QKV_DIM = ATT_DIM + 2 * KV_DIM
WINDOW = 128
ATT_SCALE = HEAD_DIM ** -0.5
T5_BUCKETS = 32
T5_MAX_DISTANCE = 128
NEG_INF = -1e30
RWKV_N = 64
RWKV_HEADS = D_MODEL // RWKV_N
GN_EPS = 64e-5
LORA_PAD = 128

N_PROMPT = BATCH * SEQ
N_SAMPLE = DEC_BATCH * DEC_SEQ
N_TOK = N_PROMPT + N_SAMPLE

VMEM_LIMIT_BYTES = 56 * 1024 * 1024

TM_TOK = 704
TM_PROMPT = 512
TM_POOL = 256
SHIFT_HALO = 8
TF_FFN = 512
CAST_STEPS = 128
CAST_WINDOWS = ((64, 1408), (64, 1408), (176, 512))
TN_PROJ = 512
SCAN_CHUNK = 64
SCAN_HEADS = 2
SCAN_LANES = SCAN_HEADS * RWKV_N
SCAN_SAMPLE_SEQS = 4


def _params(n_axes):
    return pltpu.CompilerParams(dimension_semantics=("arbitrary",) * n_axes,
                                vmem_limit_bytes=VMEM_LIMIT_BYTES)


def _dot(a, b):
    return jnp.dot(a, b, preferred_element_type=F32)


def _dot_nt(a, b):
    return lax.dot_general(a, b, (((1,), (1,)), ((), ())), preferred_element_type=F32)


def _dot_tn(a, b):
    return lax.dot_general(a, b, (((0,), (0,)), ((), ())), preferred_element_type=F32)


def _round_robin(gens):
    live = list(gens)
    while live:
        still = []
        for gen in live:
            try:
                next(gen)
                still.append(gen)
            except StopIteration:
                pass
        live = still


def _rms_norm(x, g):
    return x * lax.rsqrt(jnp.mean(x * x, axis=-1, keepdims=True) + RMS_EPS) * g


def _norm_kernel(x_ref, g_ref, o_ref):
    o_ref[...] = _rms_norm(x_ref[...], g_ref[...])


def _norm_rows(x, g, *, tm, row0, rows):
    d = x.shape[1]
    off = row0 // tm
    return pl.pallas_call(
        _norm_kernel,
        grid=(rows // tm,),
        in_specs=[pl.BlockSpec((tm, d), lambda i: (i + off, 0)),
                  pl.BlockSpec((1, d), lambda i: (0, 0))],
        out_specs=pl.BlockSpec((tm, d), lambda i: (i, 0)),
        out_shape=jax.ShapeDtypeStruct((rows, d), F32),
        compiler_params=_params(1),
        name="rms_norm",
    )(x, g.reshape(1, d))


def _norm_tails(x, g):
    d = x.shape[1]
    blocks_per_seq = SEQ // POOL_HALO
    out = pl.pallas_call(
        _norm_kernel,
        grid=(BATCH,),
        in_specs=[pl.BlockSpec((POOL_HALO, d), lambda b: ((b + 1) * blocks_per_seq - 1, 0)),
                  pl.BlockSpec((1, d), lambda b: (0, 0))],
        out_specs=pl.BlockSpec((POOL_HALO, d), lambda b: (b, 0)),
        out_shape=jax.ShapeDtypeStruct((BATCH * POOL_HALO, d), F32),
        compiler_params=_params(1),
        name="rms_norm_tails",
    )(x, g.reshape(1, d))
    return out.reshape(BATCH, POOL_HALO, d)


def _ffn_kernel(*refs, cast_next):
    if cast_next:
        (x_ref, g_ref, wg_ref, wu_ref, wd_ref, ng_ref, nu_ref, nd_ref,
         o_ref, cg_ref, cu_ref, cd_ref, xn_ref) = refs
        acc_ref = o_ref
    else:
        x_ref, g_ref, wg_ref, wu_ref, wd_ref, fg_ref, yp_ref, ys_ref, xn_ref, acc_ref = refs
    i, j = pl.program_id(0), pl.program_id(1)

    @pl.when(j == 0)
    def _():
        xn_ref[...] = _rms_norm(x_ref[...], g_ref[...]).astype(BF16)
        acc_ref[...] = jnp.zeros_like(acc_ref)

    riders = ((ng_ref, cg_ref), (nu_ref, cu_ref), (nd_ref, cd_ref)) if cast_next else ()

    def convert(pairs):
        for src, dst in pairs:
            dst[...] = src[...].astype(BF16)

    xn = xn_ref[...]
    gate = _dot(xn, wg_ref[...])
    convert(riders[:2])
    up = _dot(xn, wu_ref[...])
    convert(riders[2:])
    h = (gate * jax.nn.sigmoid(gate) * up).astype(BF16)
    acc_ref[...] += _dot(h, wd_ref[...])

    @pl.when(j == pl.num_programs(1) - 1)
    def _():
        o = x_ref[...] + 0.5 * acc_ref[...]
        if cast_next:
            o_ref[...] = o
        else:
            y = _rms_norm(o, fg_ref[...])
            yp_ref[...] = y

            @pl.when(i == pl.num_programs(0) - 1)
            def _():
                ys_ref[...] = y[y.shape[0] - N_SAMPLE:]


def _ffn(x, g, w_bf, w_next=None, final_gain=None):
    m, d = x.shape
    tm, tf = TM_TOK, TF_FFN
    grid = (m // tm, D_FF // tf)
    in_specs = [pl.BlockSpec((tm, d), lambda i, j: (i, 0)),
                pl.BlockSpec((1, d), lambda i, j: (0, 0)),
                pl.BlockSpec((d, tf), lambda i, j: (0, j)),
                pl.BlockSpec((d, tf), lambda i, j: (0, j)),
                pl.BlockSpec((tf, d), lambda i, j: (j, 0))]
    args = [x, g.reshape(1, d), *w_bf]
    scratch = [pltpu.VMEM((tm, d), BF16)]
    if w_next is not None:
        (ng, nu, nd), layer, half = w_next
        out_specs = [pl.BlockSpec((tm, d), lambda i, j: (i, 0))]
        out_shape = [jax.ShapeDtypeStruct((m, d), F32)]
        assert grid[0] * grid[1] >= CAST_STEPS

        def block(n_col_blocks):
            def index(i, j):
                b = jnp.minimum(i * grid[1] + j, CAST_STEPS - 1)
                return b // n_col_blocks, b % n_col_blocks
            return index

        for w, (rows, cols) in zip((ng, nu, nd), CAST_WINDOWS):
            assert (w.shape[2] // rows) * (w.shape[3] // cols) == CAST_STEPS
            idx = block(w.shape[3] // cols)
            in_specs.append(pl.BlockSpec((None, None, rows, cols),
                                         lambda i, j, idx=idx: (layer, half, *idx(i, j))))
            out_specs.append(pl.BlockSpec((rows, cols), idx))
            out_shape.append(jax.ShapeDtypeStruct(w.shape[2:], BF16))
            args.append(w)
    else:
        assert m == N_TOK and N_TOK - (grid[0] - 1) * tm >= N_SAMPLE
        in_specs.append(pl.BlockSpec((1, d), lambda i, j: (0, 0)))
        args.append(final_gain.reshape(1, d))
        out_specs = [pl.BlockSpec((tm, d), lambda i, j: (i, 0)),
                     pl.BlockSpec((N_SAMPLE, d), lambda i, j: (0, 0))]
        out_shape = [jax.ShapeDtypeStruct((N_PROMPT, d), F32), jax.ShapeDtypeStruct((N_SAMPLE, d), F32)]
        scratch.append(pltpu.VMEM((tm, d), F32))
    out = pl.pallas_call(
        functools.partial(_ffn_kernel, cast_next=w_next is not None),
        grid=grid,
        in_specs=in_specs,
        out_specs=out_specs,
        out_shape=out_shape,
        scratch_shapes=scratch,
        compiler_params=_params(2),
        name="ffn_half_step",
    )(*args)
    return (out[0], tuple(out[1:])) if w_next is not None else tuple(out)


def _matmul_kernel(*refs, has_gain, has_bias, has_res):
    refs = list(refs)
    lhs_ref, w_ref = refs[0], refs[1]
    pos = 2
    g_ref = b_ref = res_ref = None
    if has_gain:
        g_ref = refs[pos]
        pos += 1
    if has_bias:
        b_ref = refs[pos]
        pos += 1
    if has_res:
        res_ref = refs[pos]
        pos += 1
    o_ref, lhs_bf_ref = refs[pos], refs[pos + 1]

    @pl.when(pl.program_id(1) == 0)
    def _():
        lhs = lhs_ref[...]
        if has_gain:
            lhs = _rms_norm(lhs, g_ref[...])
        lhs_bf_ref[...] = lhs.astype(BF16)

    acc = _dot(lhs_bf_ref[...], w_ref[...])
    if has_bias:
        acc = acc + b_ref[...]
    if has_res:
        acc = res_ref[...] + acc
    o_ref[...] = acc


def _matmul(lhs, w, bias=None, res=None, *, tm, tn, res_row0=0, gain=None):
    m, k = lhs.shape
    n = w.shape[1]
    off = res_row0 // tm
    in_specs = [pl.BlockSpec((tm, k), lambda i, j: (i, 0)),
                pl.BlockSpec((k, tn), lambda i, j: (0, j))]
    args = [lhs, w]
    if gain is not None:
        in_specs.append(pl.BlockSpec((1, k), lambda i, j: (0, 0)))
        args.append(gain.reshape(1, k))
    if bias is not None:
        in_specs.append(pl.BlockSpec((1, tn), lambda i, j: (0, j)))
        args.append(bias.reshape(1, n))
    aliases = {}
    if res is not None:
        in_specs.append(pl.BlockSpec((tm, tn), lambda i, j: (i + off, j)))
        aliases = {len(args): 0}
        args.append(res)
        out_shape = jax.ShapeDtypeStruct(res.shape, F32)
    else:
        out_shape = jax.ShapeDtypeStruct((m, n), F32)
    return pl.pallas_call(
        functools.partial(_matmul_kernel, has_gain=gain is not None, has_bias=bias is not None,
                          has_res=res is not None),
        grid=(m // tm, n // tn),
        in_specs=in_specs,
        out_specs=pl.BlockSpec((tm, tn), lambda i, j: (i + off, j)),
        out_shape=out_shape,
        scratch_shapes=[pltpu.VMEM((tm, k), BF16)],
        input_output_aliases=aliases,
        compiler_params=_params(2),
        name="matmul_bias_residual",
    )(*args)


def _pool_group_out(diff, gi, x_ref, w_ref, sc_ref, o_ref):
    c = slice(gi * POOL_GROUP_DIM, (gi + 1) * POOL_GROUP_DIM)
    out = _dot(diff.astype(BF16), w_ref[gi])
    o_ref[:, c] = x_ref[:, c] + out * sc_ref[:, c]


def _pool_prompt_kernel(x_ref, halo_ref, g_ref, w_ref, sc_ref, o_ref, ext_ref, lvl_ref,
                        *, tm, tiles_per_seq, n_tiles):
    i = pl.program_id(0)
    pad, top, rows = POOL_PAD, POOL_PAD + POOL_HALO, tm + POOL_HALO

    @pl.when(i < n_tiles)
    def _():
        t_in_seq = i % tiles_per_seq
        g = g_ref[...]
        ext_ref[0:pad, :] = jnp.zeros((pad, ext_ref.shape[1]), F32)
        ext_ref[pad:top, :] = jnp.where(t_in_seq == 0, 0.0, _rms_norm(halo_ref[...], g))
        ext_ref[top:, :] = _rms_norm(x_ref[...], g)
        lvl_ref[0:pad, :] = jnp.zeros((pad, lvl_ref.shape[1]), F32)
        pos = t_in_seq * tm + lax.broadcasted_iota(jnp.int32, (tm, 1), 0)
        for gi, w in enumerate(POOL_WINDOWS):
            c = slice(gi * POOL_GROUP_DIM, (gi + 1) * POOL_GROUP_DIM)
            part = ext_ref[pl.ds(pad, rows), c] + ext_ref[pl.ds(pad - 1, rows), c]
            span = 2
            while span < w:
                lvl_ref[pl.ds(pad, rows), :] = part
                part = part + lvl_ref[pl.ds(pad - span, rows), :]
                span *= 2
            cur = ext_ref[pl.ds(top, tm), c]
            cnt = jnp.minimum(w, pos + 1).astype(F32)
            _pool_group_out(part[POOL_HALO:] / cnt - cur, gi, x_ref, w_ref, sc_ref, o_ref)

    @pl.when(i >= n_tiles)
    def _():
        o_ref[...] = x_ref[...]


def _pool_prompt(x, gain, w_pool, scale):
    d = x.shape[1]
    tm = TM_POOL
    halo_blocks = tm // POOL_HALO
    return pl.pallas_call(
        functools.partial(_pool_prompt_kernel, tm=tm, tiles_per_seq=SEQ // tm, n_tiles=N_PROMPT // tm),
        grid=(N_TOK // tm,),
        in_specs=[pl.BlockSpec((tm, d), lambda i: (i, 0)),
                  pl.BlockSpec((POOL_HALO, d), lambda i: (jnp.maximum(i * halo_blocks - 1, 0), 0)),
                  pl.BlockSpec((1, d), lambda i: (0, 0)),
                  pl.BlockSpec((POOL_GROUPS, POOL_GROUP_DIM, POOL_GROUP_DIM), lambda i: (0, 0, 0)),
                  pl.BlockSpec((1, d), lambda i: (0, 0))],
        out_specs=pl.BlockSpec((tm, d), lambda i: (i, 0)),
        out_shape=jax.ShapeDtypeStruct(x.shape, F32),
        scratch_shapes=[pltpu.VMEM((POOL_PAD + POOL_HALO + tm, d), F32),
                        pltpu.VMEM((POOL_PAD + POOL_HALO + tm, POOL_GROUP_DIM), F32)],
        compiler_params=_params(1),
        name="pool_prompt",
    )(x, x, gain.reshape(1, d), w_pool, scale.reshape(1, d))


def _pool_sample_kernel(ext_ref, x_ref, w_ref, sc_ref, o_ref):
    for gi, w in enumerate(POOL_WINDOWS):
        c = slice(gi * POOL_GROUP_DIM, (gi + 1) * POOL_GROUP_DIM)
        cur = ext_ref[:, pl.ds(POOL_HALO, DEC_SEQ), c]
        s = cur
        for back in range(1, w):
            s = s + ext_ref[:, pl.ds(POOL_HALO - back, DEC_SEQ), c]
        cnt = min(w, PAST_LEN + 1)
        diff = (s / float(cnt) - cur).reshape(N_SAMPLE, POOL_GROUP_DIM)
        _pool_group_out(diff, gi, x_ref, w_ref, sc_ref, o_ref)


def _pool_sample(ext, x, w_pool, scale):
    d = x.shape[1]
    blk = N_PROMPT // N_SAMPLE
    return pl.pallas_call(
        _pool_sample_kernel,
        grid=(1,),
        in_specs=[pl.BlockSpec(ext.shape, lambda i: (0, 0, 0)),
                  pl.BlockSpec((N_SAMPLE, d), lambda i: (blk, 0)),
                  pl.BlockSpec((POOL_GROUPS, POOL_GROUP_DIM, POOL_GROUP_DIM), lambda i: (0, 0, 0)),
                  pl.BlockSpec((1, d), lambda i: (0, 0))],
        out_specs=pl.BlockSpec((N_SAMPLE, d), lambda i: (blk, 0)),
        out_shape=jax.ShapeDtypeStruct(x.shape, F32),
        input_output_aliases={1: 0},
        compiler_params=_params(1),
        name="pool_sample",
    )(ext, x, w_pool, scale.reshape(1, d))


def _t5_bucket_table(tq):
    qi = np.arange(tq)[:, None]
    kj = np.arange(WINDOW + tq)[None, :]
    dist = qi + WINDOW - kj
    exact = T5_BUCKETS // 2
    ratio = np.log(np.maximum(dist, 1) / exact) / math.log(T5_MAX_DISTANCE / exact)
    large = np.minimum(exact + (ratio * (T5_BUCKETS - exact)).astype(np.int64), T5_BUCKETS - 1)
    bucket = np.where(dist < exact, dist, large)
    valid = (dist >= 0) & (dist < WINDOW)
    table = np.where(valid, bucket, -1).astype(np.int32)
    return table[:, :WINDOW], table[:, WINDOW:]


def _attn_sample_kernel(rb_ref, sink_ref, bkt_p_ref, bkt_c_ref, q_ref, kc_ref, vc_ref, kp_ref, vp_ref,
                        o_ref, bias_p_ref, bias_c_ref, sink_col_ref, *, tq):
    step = pl.program_id(0)

    @pl.when(step == 0)
    def _():
        bkt_p = bkt_p_ref[...]
        bkt_c = bkt_c_ref[...]
        bias_p_ref[...] = jnp.zeros_like(bias_p_ref)
        bias_c_ref[...] = jnp.zeros_like(bias_c_ref)

        def add_bucket(b, carry):
            eq_p = bkt_p == b
            eq_c = bkt_c == b
            for h in range(N_HEADS):
                kvh, g = divmod(h, GQA_GROUP)
                r = slice(g * tq, (g + 1) * tq)
                val = rb_ref[b, h]
                bias_p_ref[kvh, r, :] += jnp.where(eq_p, val, 0.0)
                bias_c_ref[kvh, r, :] += jnp.where(eq_c, val, 0.0)
            return carry

        lax.fori_loop(0, T5_BUCKETS, add_bucket, 0)
        for h in range(N_HEADS):
            kvh, g = divmod(h, GQA_GROUP)
            r = slice(g * tq, (g + 1) * tq)
            bias_p_ref[kvh, r, :] = jnp.where(bkt_p < 0, NEG_INF, bias_p_ref[kvh, r, :])
            bias_c_ref[kvh, r, :] = jnp.where(bkt_c < 0, NEG_INF, bias_c_ref[kvh, r, :])
            sink_col_ref[kvh, r, :] = jnp.full((tq, 1), sink_ref[h], F32)

    def kv_head(kvh):
        heads = [kvh * GQA_GROUP + g for g in range(GQA_GROUP)]
        qs = jnp.concatenate([q_ref[:, h * HEAD_DIM:(h + 1) * HEAD_DIM] for h in heads], axis=0).astype(BF16)
        c = slice(kvh * HEAD_DIM, (kvh + 1) * HEAD_DIM)
        qk_p = _dot_nt(qs, kp_ref[:, c].astype(BF16))
        qk_c = _dot_nt(qs, kc_ref[:, c].astype(BF16))
        yield
        s_p = qk_p * ATT_SCALE + bias_p_ref[kvh]
        s_c = qk_c * ATT_SCALE + bias_c_ref[kvh]
        sink = sink_col_ref[kvh]
        m = jnp.maximum(jnp.maximum(jnp.max(s_p, axis=-1, keepdims=True),
                                    jnp.max(s_c, axis=-1, keepdims=True)), sink)
        p_p = jnp.exp(s_p - m)
        p_c = jnp.exp(s_c - m)
        den = (jnp.sum(p_p, axis=-1, keepdims=True) + jnp.sum(p_c, axis=-1, keepdims=True)
               + jnp.exp(sink - m))
        o_p = _dot(p_p.astype(BF16), vp_ref[:, c].astype(BF16))
        o_c = _dot(p_c.astype(BF16), vc_ref[:, c].astype(BF16))
        yield
        o = (o_p + o_c) / den
        for g, h in enumerate(heads):
            o_ref[:, h * HEAD_DIM:(h + 1) * HEAD_DIM] = o[g * tq:(g + 1) * tq, :]

    _round_robin([kv_head(kvh) for kvh in range(N_KV_HEADS)])


def _attention_sample(qkv, rel_bias, sinks, k_prev, v_prev):
    tq = DEC_SEQ
    bkt_p, bkt_c = _t5_bucket_table(tq)
    off = N_PROMPT // tq
    kcol = ATT_DIM // KV_DIM
    smem = pl.BlockSpec(memory_space=pltpu.SMEM)
    cache = pl.BlockSpec((None, WINDOW, KV_DIM), lambda i: (i, 0, 0))
    rows = GQA_GROUP * tq
    return pl.pallas_call(
        functools.partial(_attn_sample_kernel, tq=tq),
        grid=(DEC_BATCH,),
        in_specs=[smem, smem,
                  pl.BlockSpec((tq, WINDOW), lambda i: (0, 0)),
                  pl.BlockSpec((tq, tq), lambda i: (0, 0)),
                  pl.BlockSpec((tq, ATT_DIM), lambda i: (i + off, 0)),
                  pl.BlockSpec((tq, KV_DIM), lambda i: (i + off, kcol)),
                  pl.BlockSpec((tq, KV_DIM), lambda i: (i + off, kcol + 1)),
                  cache, cache],
        out_specs=pl.BlockSpec((tq, ATT_DIM), lambda i: (i, 0)),
        out_shape=jax.ShapeDtypeStruct((N_SAMPLE, ATT_DIM), F32),
        scratch_shapes=[pltpu.VMEM((N_KV_HEADS, rows, WINDOW), F32),
                        pltpu.VMEM((N_KV_HEADS, rows, tq), F32),
                        pltpu.VMEM((N_KV_HEADS, rows, 1), F32)],
        compiler_params=_params(1),
        name="swa_attention_sample",
    )(rel_bias, sinks, jnp.asarray(bkt_p), jnp.asarray(bkt_c), qkv, qkv, qkv, k_prev, v_prev)


def _attn_prompt_kernel(rb_ref, sink_ref, bkt_ref, q_ref, kc_ref, vc_ref, kp_ref, vp_ref, o_ref, bias_ref,
                        *, blocks_per_seq):
    step = pl.program_id(0)
    tq, nk, pair = WINDOW, 2 * WINDOW, 2 * HEAD_DIM

    @pl.when(step == 0)
    def _():
        bkt = bkt_ref[...]
        bias_ref[0] = jnp.zeros(bias_ref.shape[1:], F32)

        def add_bucket(b, carry):
            eq = bkt == b
            for h in range(N_HEADS):
                bias_ref[0, h] += jnp.where(eq, rb_ref[b, h], 0.0)
            return carry

        lax.fori_loop(0, T5_BUCKETS, add_bucket, 0)
        no_prev_key = lax.broadcasted_iota(jnp.int32, (nk, tq), 0) < WINDOW
        for h in range(N_HEADS):
            masked = jnp.where(bkt < 0, NEG_INF, bias_ref[0, h])
            bias_ref[0, h] = masked
            bias_ref[1, h] = jnp.where(no_prev_key, NEG_INF, masked)

    first = (step % blocks_per_seq == 0).astype(jnp.int32)
    lane_half = lax.broadcasted_iota(jnp.int32, (nk, pair), 1) // HEAD_DIM
    row_half = lax.broadcasted_iota(jnp.int32, (pair, tq), 0) // HEAD_DIM

    for kvh in range(N_KV_HEADS):
        tile = slice((kvh // 2) * pair, (kvh // 2 + 1) * pair)
        half = kvh % 2

        def both_halves(prev_ref, cur_ref):
            x = jnp.concatenate([prev_ref[:, tile], cur_ref[:, tile]], axis=0)
            own = jnp.where(lane_half == half, x, 0.0)
            other = pltpu.roll(own, HEAD_DIM, axis=1)
            return (own, other) if half == 0 else (other, own)

        k_lo, k_hi = both_halves(kp_ref, kc_ref)
        v_lo, v_hi = both_halves(vp_ref, vc_ref)
        k2 = jnp.concatenate([k_lo, k_hi], axis=0).astype(BF16)
        v_lo_t = v_lo.T.astype(BF16)
        v_hi_t = v_hi.T.astype(BF16)

        def head_pair(j):
            cols = slice(j * pair, (j + 1) * pair)
            qp = (q_ref[:, cols] * ATT_SCALE).astype(BF16)
            s2 = _dot_nt(k2, qp)
            yield
            probs, dens = [], []
            for t in range(2):
                h = 2 * j + t
                s = s2[t * nk:(t + 1) * nk] + bias_ref[first, h]
                sink = sink_ref[h]
                m = jnp.maximum(jnp.max(s, axis=0, keepdims=True), sink)
                p = jnp.exp(s - m)
                dens.append(jnp.sum(p, axis=0, keepdims=True) + jnp.exp(sink - m))
                probs.append(p.astype(BF16))
            o2 = _dot(v_lo_t, probs[0]) + _dot(v_hi_t, probs[1])
            yield
            o2 = o2 / jnp.where(row_half == 0, dens[0], dens[1])
            o_ref[:, cols] = o2.T

        pairs_per_kv = GQA_GROUP // 2
        _round_robin([head_pair(kvh * pairs_per_kv + jj) for jj in range(pairs_per_kv)])


def _attention_prompt(qkv, rel_bias, sinks):
    bkt_p, bkt_c = _t5_bucket_table(WINDOW)
    bkt_t = np.ascontiguousarray(np.concatenate([bkt_p, bkt_c], axis=1).T)
    kcol = ATT_DIM // KV_DIM
    smem = pl.BlockSpec(memory_space=pltpu.SMEM)
    prev = lambda i: jnp.maximum(i - 1, 0)
    return pl.pallas_call(
        functools.partial(_attn_prompt_kernel, blocks_per_seq=SEQ // WINDOW),
        grid=(N_PROMPT // WINDOW,),
        in_specs=[smem, smem,
                  pl.BlockSpec((2 * WINDOW, WINDOW), lambda i: (0, 0)),
                  pl.BlockSpec((WINDOW, ATT_DIM), lambda i: (i, 0)),
                  pl.BlockSpec((WINDOW, KV_DIM), lambda i: (i, kcol)),
                  pl.BlockSpec((WINDOW, KV_DIM), lambda i: (i, kcol + 1)),
                  pl.BlockSpec((WINDOW, KV_DIM), lambda i: (prev(i), kcol)),
                  pl.BlockSpec((WINDOW, KV_DIM), lambda i: (prev(i), kcol + 1))],
        out_specs=pl.BlockSpec((WINDOW, ATT_DIM), lambda i: (i, 0)),
        out_shape=jax.ShapeDtypeStruct((N_PROMPT, ATT_DIM), F32),
        scratch_shapes=[pltpu.VMEM((2, N_HEADS, 2 * WINDOW, WINDOW), F32)],
        compiler_params=_params(1),
        name="swa_attention_prompt",
    )(rel_bias, sinks, jnp.asarray(bkt_t), qkv, qkv, qkv, qkv, qkv)


def _rwkv_proj_kernel(*refs, tm, seq_len, has_first):
    refs = list(refs)
    x_ref, halo_ref, gain_ref = refs[:3]
    pos = 3
    first_ref = None
    if has_first:
        first_ref = refs[pos]
        pos += 1
    (mu_ref, wr_ref, wk_ref, wv_ref, w1_ref, w2_ref, w0_ref, a1_ref, a2_ref, a0_ref, g1_ref, g2_ref,
     r_ref, k_ref, v_ref, ld_ref, a_ref, g_ref,
     ext_ref, xr_ref, xk_ref, xv_ref, hw_ref, ha_ref, hg_ref) = refs[pos:]

    @pl.when(pl.program_id(1) == 0)
    def _():
        gain = gain_ref[...]
        u = _rms_norm(x_ref[...], gain)
        ext_ref[0:SHIFT_HALO, :] = _rms_norm(halo_ref[...], gain)
        ext_ref[SHIFT_HALO:, :] = u
        prev = ext_ref[pl.ds(SHIFT_HALO - 1, tm), :]
        row = pl.program_id(0) * tm + lax.broadcasted_iota(jnp.int32, (tm, 1), 0)
        starts = row % seq_len == 0
        prev = jnp.where(starts, first_ref[...] if has_first else 0.0, prev)
        dx = prev - u
        mix = lambda i: (u + dx * mu_ref[i:i + 1, :]).astype(BF16)
        xr_ref[...] = mix(0)
        hw_ref[...] = jnp.tanh(_dot(mix(1), w1_ref[...])).astype(BF16)
        xk_ref[...] = mix(2)
        xv_ref[...] = mix(3)
        ha_ref[...] = _dot(mix(4), a1_ref[...]).astype(BF16)
        hg_ref[...] = jax.nn.sigmoid(_dot(mix(5), g1_ref[...])).astype(BF16)

    r_ref[...] = _dot(xr_ref[...], wr_ref[...])
    k_ref[...] = _dot(xk_ref[...], wk_ref[...])
    v_ref[...] = _dot(xv_ref[...], wv_ref[...])
    z = -(w0_ref[...] + _dot(hw_ref[...], w2_ref[...]))
    softplus = jnp.maximum(z, 0.0) + jnp.log1p(jnp.exp(-jnp.abs(z)))
    ld_ref[...] = -jnp.exp(-softplus - 0.5)
    a_ref[...] = jax.nn.sigmoid(a0_ref[...] + _dot(ha_ref[...], a2_ref[...]))
    g_ref[...] = _dot(hg_ref[...], g2_ref[...])


def _rwkv_proj(x, gain, weights, *, row0, rows, tm, seq_len, first=None):
    mu, w_r, w_k, w_v, w1, w2, w0, a1, a2, a0, g1, g2 = weights
    d = x.shape[1]
    tn = TN_PROJ
    gl = g1.shape[1]
    off = row0 // tm
    halo_blocks = tm // SHIFT_HALO
    halo_off = row0 // SHIFT_HALO
    row = lambda i, j: (i + off, 0)
    col = lambda i, j: (0, j)
    fixed = lambda i, j: (0, 0)
    in_specs = [pl.BlockSpec((tm, d), row),
                pl.BlockSpec((SHIFT_HALO, d), lambda i, j: (jnp.maximum(halo_off + i * halo_blocks - 1, 0), 0)),
                pl.BlockSpec((1, d), fixed)]
    args = [x, x, gain.reshape(1, d)]
    if first is not None:
        in_specs.append(pl.BlockSpec((tm, d), lambda i, j: (i, 0)))
        args.append(first)
    in_specs += [pl.BlockSpec((6, d), fixed),
                 pl.BlockSpec((d, tn), col), pl.BlockSpec((d, tn), col), pl.BlockSpec((d, tn), col),
                 pl.BlockSpec((d, LORA_PAD), fixed), pl.BlockSpec((LORA_PAD, tn), col), pl.BlockSpec((1, tn), col),
                 pl.BlockSpec((d, LORA_PAD), fixed), pl.BlockSpec((LORA_PAD, tn), col), pl.BlockSpec((1, tn), col),
                 pl.BlockSpec((d, gl), fixed), pl.BlockSpec((gl, tn), col)]
    args += [mu, w_r, w_k, w_v, w1, w2, w0.reshape(1, d), a1, a2, a0.reshape(1, d), g1, g2]
    out = jax.ShapeDtypeStruct((rows, d), F32)
    return pl.pallas_call(
        functools.partial(_rwkv_proj_kernel, tm=tm, seq_len=seq_len, has_first=first is not None),
        grid=(rows // tm, d // tn),
        in_specs=in_specs,
        out_specs=[pl.BlockSpec((tm, tn), lambda i, j: (i, j))] * 6,
        out_shape=[out] * 6,
        scratch_shapes=[pltpu.VMEM((tm + SHIFT_HALO, d), F32)] + [pltpu.VMEM((tm, d), BF16)] * 3
        + [pltpu.VMEM((tm, LORA_PAD), BF16)] * 2 + [pltpu.VMEM((tm, gl), BF16)],
        compiler_params=_params(2),
        name="rwkv_projections",
    )(*args)


def _scan_kernel(*refs, chunk, has_init):
    (r_ref, k_ref, v_ref, ld_ref, a_ref, g_ref, kk_ref, ka_ref, rk_ref, lnw_ref, lnb_ref) = refs[:11]
    refs = refs[11:]
    if has_init:
        s0_ref, refs = refs[0], refs[1:]
    z_ref, sout_ref, s_ref = refs
    c_idx = pl.program_id(1)
    cc, n, w = chunk, RWKV_N, SCAN_LANES
    hc = SCAN_HEADS * cc
    n_groups = RWKV_HEADS // SCAN_HEADS
    n_sb = z_ref.shape[0]

    @pl.when(c_idx == 0)
    def _():
        s_ref[...] = jnp.zeros_like(s_ref)
        if has_init:
            for sb in range(n_sb):
                for h in range(RWKV_HEADS):
                    gi, hh = divmod(h, SCAN_HEADS)
                    s_ref[sb, gi, hh * n:(hh + 1) * n, hh * n:(hh + 1) * n] = s0_ref[sb, h]

    def iota(shape, axis):
        return lax.broadcasted_iota(jnp.int32, shape, axis)

    head_diag = iota((w, w), 0) // n == iota((w, w), 1) // n
    tril = (iota((cc, cc), 0) >= iota((cc, cc), 1)).astype(BF16)
    t_row = iota((cc, hc), 0)
    s_col = iota((cc, hc), 1) % cc
    strict = t_row > s_col
    causal = t_row >= s_col
    lanes_diag = iota((hc, w), 0) // cc == iota((hc, w), 1) // n
    tokens_diag = iota((hc, hc), 0) // cc == iota((hc, hc), 1) // cc
    lane_head = iota((cc, w), 1) // n

    def block_diag(x, diag):
        return jnp.where(diag, jnp.concatenate([x] * SCAN_HEADS, axis=0), 0.0).astype(BF16)

    def bd_lanes(x):
        return block_diag(x, lanes_diag)

    def bd_tokens(x):
        return block_diag(x, tokens_diag)

    def head_sum(x):
        out = jnp.zeros_like(x)
        for hh in range(SCAN_HEADS):
            mine = lane_head == hh
            out = jnp.where(mine, jnp.sum(jnp.where(mine, x, 0.0), axis=-1, keepdims=True), out)
        return out

    def group(sb, gi):
        c = slice(gi * w, (gi + 1) * w)
        r, k, v, ld, a = r_ref[sb, :, c], k_ref[sb, :, c], v_ref[sb, :, c], ld_ref[sb, :, c], a_ref[sb, :, c]

        p1 = ld.astype(BF16)
        r1 = ld - p1.astype(F32)
        p2 = r1.astype(BF16)
        p3 = (r1 - p2.astype(F32)).astype(BF16)
        cum = _dot(tril, jnp.concatenate([p1, p2, p3], axis=1))
        yield
        kk = k * kk_ref[:, c]
        kmod = k * (1.0 + (a - 1.0) * ka_ref[:, c])
        kap = kk / jnp.maximum(jnp.sqrt(head_sum(kk * kk)), 1e-12)
        b = kap * a

        lcum = cum[:, :w] + cum[:, w:2 * w] + cum[:, 2 * w:]
        lend = lcum[cc - 1:cc, :]
        e_inc = jnp.exp(lcum)
        e_exc = jnp.exp(lcum - ld)
        e_neg = jnp.exp(-lcum)
        e_end = jnp.exp(lend - lcum)

        kq = kap * e_exc
        rq = r * e_inc
        qr = jnp.concatenate([kq, rq], axis=0).astype(BF16)
        s_bd = s_ref[sb, gi]
        bk_bd = jnp.concatenate([bd_lanes(b * e_neg), bd_lanes(kmod * e_neg)], axis=0)
        if (2 * hc) % 128 == 0:
            gram = _dot_nt(qr, jnp.concatenate([bk_bd, s_bd.astype(BF16)], axis=0))
            yield
            qs = gram[:, 2 * hc:]
        else:
            gram = _dot_nt(qr, bk_bd)
            yield
            qs = _dot_nt(qr, s_bd.astype(BF16))
            yield
        a_w = jnp.where(strict, gram[:cc, :hc], 0.0)
        bk_w = jnp.where(strict, gram[:cc, hc:2 * hc], 0.0)
        cb_w = jnp.where(causal, gram[cc:, :hc], 0.0)
        ck_w = jnp.where(causal, gram[cc:, hc:2 * hc], 0.0)

        kv = _dot(jnp.concatenate([bk_w, ck_w], axis=0).astype(BF16), bd_lanes(v))
        yield
        x = -(qs[:cc] + kv[:cc])
        ax = _dot(a_w.astype(BF16), bd_lanes(x))
        yield
        x = x - ax
        a_pow = a_w
        power = 2
        while power < cc:
            a_pow = _dot(a_pow.astype(BF16), bd_tokens(a_pow))
            yield
            ax = _dot(a_pow.astype(BF16), bd_lanes(x))
            yield
            x = x + ax
            power *= 2
        u = x

        y_in = _dot(cb_w.astype(BF16), bd_lanes(u)) + kv[cc:]
        yield
        upd = _dot_tn(jnp.concatenate([u, v], axis=0).astype(BF16),
                      jnp.concatenate([b * e_end, kmod * e_end], axis=0).astype(BF16))
        yield
        s_ref[sb, gi] = s_bd * jnp.exp(lend) + jnp.where(head_diag, upd, 0.0)

        y = qs[cc:] + y_in
        mean = head_sum(y) * (1.0 / n)
        dev = y - mean
        var = head_sum(dev * dev) * (1.0 / n)
        yn = dev * lax.rsqrt(var + GN_EPS) * lnw_ref[:, c] + lnb_ref[:, c]
        bonus = head_sum(r * kmod * rk_ref[:, c]) * v
        z_ref[sb, :, c] = (yn + bonus) * g_ref[sb, :, c]

    _round_robin([group(sb, gi) for sb in range(n_sb) for gi in range(n_groups)])

    @pl.when(c_idx == pl.num_programs(1) - 1)
    def _():
        for sb in range(n_sb):
            for h in range(RWKV_HEADS):
                gi, hh = divmod(h, SCAN_HEADS)
                sout_ref[sb, h] = s_ref[sb, gi, hh * n:(hh + 1) * n, hh * n:(hh + 1) * n]


def _rwkv_scan(proj, k_k, k_a, r_k, ln_w, ln_b, *, n_seq, per_step, chunk, n_chunks, s0=None):
    d = D_MODEL
    t = chunk * n_chunks
    tok = pl.BlockSpec((per_step, chunk, d), lambda b, c: (b, c, 0))
    par = pl.BlockSpec((1, d), lambda b, c: (0, 0))
    state = pl.BlockSpec((per_step, RWKV_HEADS, RWKV_N, RWKV_N), lambda b, c: (b, 0, 0, 0))
    in_specs = [tok] * 6 + [par] * 5
    args = [p.reshape(n_seq, t, d) for p in proj] + [p.reshape(1, d) for p in (k_k, k_a, r_k, ln_w, ln_b)]
    if s0 is not None:
        in_specs.append(state)
        args.append(s0)
    n_groups = RWKV_HEADS // SCAN_HEADS
    z, s_out = pl.pallas_call(
        functools.partial(_scan_kernel, chunk=chunk, has_init=s0 is not None),
        grid=(n_seq // per_step, n_chunks),
        in_specs=in_specs,
        out_specs=[tok, state],
        out_shape=[jax.ShapeDtypeStruct((n_seq, t, d), F32),
                   jax.ShapeDtypeStruct((n_seq, RWKV_HEADS, RWKV_N, RWKV_N), F32)],
        scratch_shapes=[pltpu.VMEM((per_step, n_groups, SCAN_LANES, SCAN_LANES), F32)],
        compiler_params=_params(2),
        name="rwkv_scan",
    )(*args)
    return z.reshape(n_seq * t, d), s_out


def _pad_cols(w):
    return jnp.pad(w, ((0, 0), (0, LORA_PAD - w.shape[1])))


def _pad_rows(w):
    return jnp.pad(w, ((0, LORA_PAD - w.shape[0]), (0, 0)))


def kernel(x_prompt, x_sample, state_pool, cache_win_k, cache_win_v, state_shift, state_wkv, norm_ffn1, norm_mix, norm_ffn2, norm_final, ffn_w_gate, ffn_w_up, ffn_w_down, pool_w, pool_scale, att_w_qkv, att_b_qkv, att_w_o, att_b_o, att_sinks, rel_bias, rwkv_mu, rwkv_w_r, rwkv_w_k, rwkv_w_v, rwkv_w_o, rwkv_w0, rwkv_w1, rwkv_w2, rwkv_a0, rwkv_a1, rwkv_a2, rwkv_g1, rwkv_g2, rwkv_k_k, rwkv_k_a, rwkv_r_k, rwkv_ln_w, rwkv_ln_b):
    d = D_MODEL
    x = jnp.concatenate([x_prompt.reshape(N_PROMPT, d), x_sample.reshape(N_SAMPLE, d)], axis=0)
    ffn_w = (ffn_w_gate, ffn_w_up, ffn_w_down)
    w_bf = tuple(w[0, 0].astype(BF16) for w in ffn_w)
    pool_p, pool_s, wk_p, wv_p, wk_s, wv_s, sh_p, sh_s, wkv_p, wkv_s = ([] for _ in range(10))

    for l in range(DEPTH):
        j, kind = divmod(l, N_MIXERS)
        x, w_bf = _ffn(x, norm_ffn1[l], w_bf, (ffn_w, l, 1))
        gain = norm_mix[l]
        if kind == 0:
            w_pool = pool_w[j].astype(BF16)
            u_s = _norm_rows(x, gain, tm=N_SAMPLE, row0=N_PROMPT, rows=N_SAMPLE).reshape(DEC_BATCH, DEC_SEQ, d)
            ext = jnp.concatenate([jnp.zeros((DEC_BATCH, 1, d), F32), state_pool[j], u_s], axis=1)
            pool_p.append(_norm_tails(x, gain)[:, -POOL_BUF:])
            pool_s.append(ext[:, -POOL_BUF:])
            x = _pool_prompt(x, gain, w_pool, pool_scale[j])
            x = _pool_sample(ext, x, w_pool, pool_scale[j])
        elif kind == 1:
            qkv = _matmul(x, att_w_qkv[j].astype(BF16), att_b_qkv[j], tm=TM_TOK, tn=QKV_DIM, gain=gain)
            k_buf = cache_win_k[j].reshape(DEC_BATCH, WINDOW, KV_DIM)
            v_buf = cache_win_v[j].reshape(DEC_BATCH, WINDOW, KV_DIM)
            o_p = _attention_prompt(qkv, rel_bias, att_sinks[j])
            o_s = _attention_sample(qkv, rel_bias, att_sinks[j], k_buf, v_buf)
            w_o = att_w_o[j].astype(BF16)
            x = _matmul(o_p, w_o, att_b_o[j], x, tm=TM_PROMPT, tn=d, res_row0=0)
            x = _matmul(o_s, w_o, att_b_o[j], x, tm=N_SAMPLE, tn=d, res_row0=N_PROMPT)
            kv_shape = (WINDOW, N_KV_HEADS, HEAD_DIM)
            k_cols = slice(ATT_DIM, ATT_DIM + KV_DIM)
            v_cols = slice(ATT_DIM + KV_DIM, QKV_DIM)
            tails = [slice((b + 1) * SEQ - WINDOW, (b + 1) * SEQ) for b in range(BATCH)]
            wk_p.append(jnp.stack([qkv[t, k_cols] for t in tails]).reshape(BATCH, *kv_shape))
            wv_p.append(jnp.stack([qkv[t, v_cols] for t in tails]).reshape(BATCH, *kv_shape))
            new_rows = slice(N_PROMPT, N_TOK)
            k_s = jnp.concatenate([k_buf, qkv[new_rows, k_cols].reshape(DEC_BATCH, DEC_SEQ, KV_DIM)], axis=1)
            v_s = jnp.concatenate([v_buf, qkv[new_rows, v_cols].reshape(DEC_BATCH, DEC_SEQ, KV_DIM)], axis=1)
            wk_s.append(k_s[:, -WINDOW:].reshape(DEC_BATCH, *kv_shape))
            wv_s.append(v_s[:, -WINDOW:].reshape(DEC_BATCH, *kv_shape))
        else:
            weights = (rwkv_mu[j],
                       rwkv_w_r[j].astype(BF16), rwkv_w_k[j].astype(BF16), rwkv_w_v[j].astype(BF16),
                       _pad_cols(rwkv_w1[j]).astype(BF16), _pad_rows(rwkv_w2[j]).astype(BF16), rwkv_w0[j],
                       _pad_cols(rwkv_a1[j]).astype(BF16), _pad_rows(rwkv_a2[j]).astype(BF16), rwkv_a0[j],
                       rwkv_g1[j].astype(BF16), rwkv_g2[j].astype(BF16))
            first_s = jnp.repeat(state_shift[j], DEC_SEQ, axis=0)
            proj_p = _rwkv_proj(x, gain, weights, row0=0, rows=N_PROMPT, tm=TM_PROMPT, seq_len=SEQ)
            proj_s = _rwkv_proj(x, gain, weights, row0=N_PROMPT, rows=N_SAMPLE, tm=N_SAMPLE, seq_len=DEC_SEQ,
                                first=first_s)
            head_params = (rwkv_k_k[j], rwkv_k_a[j], rwkv_r_k[j], rwkv_ln_w[j], rwkv_ln_b[j])
            z_p, s_p = _rwkv_scan(proj_p, *head_params, n_seq=BATCH, per_step=BATCH, chunk=SCAN_CHUNK,
                                  n_chunks=SEQ // SCAN_CHUNK)
            z_s, s_s = _rwkv_scan(proj_s, *head_params, n_seq=DEC_BATCH, per_step=SCAN_SAMPLE_SEQS, chunk=DEC_SEQ,
                                  n_chunks=1, s0=state_wkv[j])
            sh_p.append(_norm_tails(x, gain)[:, -1])
            sh_s.append(_norm_rows(x, gain, tm=N_SAMPLE, row0=N_PROMPT, rows=N_SAMPLE)
                        .reshape(DEC_BATCH, DEC_SEQ, d)[:, -1])
            w_o = rwkv_w_o[j].astype(BF16)
            x = _matmul(z_p, w_o, None, x, tm=TM_PROMPT, tn=d, res_row0=0)
            x = _matmul(z_s, w_o, None, x, tm=N_SAMPLE, tn=d, res_row0=N_PROMPT)
            wkv_p.append(s_p)
            wkv_s.append(s_s)
        if l + 1 < DEPTH:
            x, w_bf = _ffn(x, norm_ffn2[l], w_bf, (ffn_w, l + 1, 0))
        else:
            y_p, y_s = _ffn(x, norm_ffn2[l], w_bf, final_gain=norm_final)

    return (y_p.reshape(BATCH, SEQ, d), y_s.reshape(DEC_BATCH, DEC_SEQ, d),
            jnp.stack(pool_p), jnp.stack(pool_s),
            jnp.stack(wk_p), jnp.stack(wv_p), jnp.stack(wk_s), jnp.stack(wv_s),
            jnp.stack(sh_p), jnp.stack(sh_s),
            jnp.stack(wkv_p), jnp.stack(wkv_s))
```

```python
import functools
import math

import numpy as np
import jax
import jax.numpy as jnp
from jax import lax
from jax.experimental import pallas as pl
from jax.experimental.pallas import tpu as pltpu

F32 = jnp.float32
BF16 = jnp.bfloat16

D_MODEL = 2048
BATCH = 2
SEQ = 4096
DEPTH = 4
DEC_BATCH = 32
DEC_SEQ = 8
PAST_LEN = 16384
N_MIXERS = 3
RMS_EPS = 1e-6
D_FF = 5632
POOL_WINDOWS = (2, 4, 8, 16)
POOL_GROUPS = 4
POOL_GROUP_DIM = D_MODEL // POOL_GROUPS
POOL_BUF = max(POOL_WINDOWS) - 1
POOL_HALO = POOL_BUF + 1
POOL_PAD = 8
assert all(w & (w - 1) == 0 for w in POOL_WINDOWS) and max(POOL_WINDOWS) // 2 <= POOL_PAD
HEAD_DIM = 64
N_HEADS = D_MODEL // HEAD_DIM
N_KV_HEADS = 4
GQA_GROUP = N_HEADS // N_KV_HEADS
ATT_DIM = N_HEADS * HEAD_DIM
KV_DIM = N_KV_HEADS * HEAD_DIM
QKV_DIM = ATT_DIM + 2 * KV_DIM
WINDOW = 128
ATT_SCALE = HEAD_DIM ** -0.5
T5_BUCKETS = 32
T5_MAX_DISTANCE = 128
NEG_INF = -1e30
RWKV_N = 64
RWKV_HEADS = D_MODEL // RWKV_N
GN_EPS = 64e-5
LORA_PAD = 128

N_PROMPT = BATCH * SEQ
N_SAMPLE = DEC_BATCH * DEC_SEQ
N_TOK = N_PROMPT + N_SAMPLE

VMEM_LIMIT_BYTES = 56 * 1024 * 1024

TM_TOK = 704
TM_PROMPT = 512
TM_POOL = 256
SHIFT_HALO = 8
TF_FFN = 512
CAST_STEPS = 128
CAST_WINDOWS = ((64, 1408), (64, 1408), (176, 512))
TN_PROJ = 512
SCAN_CHUNK = 64
SCAN_HEADS = 2
SCAN_LANES = SCAN_HEADS * RWKV_N
SCAN_SAMPLE_SEQS = 4


def _params(n_axes):
    return pltpu.CompilerParams(dimension_semantics=("arbitrary",) * n_axes,
                                vmem_limit_bytes=VMEM_LIMIT_BYTES)


def _dot(a, b):
    return jnp.dot(a, b, preferred_element_type=F32)


def _dot_nt(a, b):
    return lax.dot_general(a, b, (((1,), (1,)), ((), ())), preferred_element_type=F32)


def _dot_tn(a, b):
    return lax.dot_general(a, b, (((0,), (0,)), ((), ())), preferred_element_type=F32)


def _round_robin(gens):
    live = list(gens)
    while live:
        still = []
        for gen in live:
            try:
                next(gen)
                still.append(gen)
            except StopIteration:
                pass
        live = still


def _rms_norm(x, g):
    return x * lax.rsqrt(jnp.mean(x * x, axis=-1, keepdims=True) + RMS_EPS) * g


def _norm_kernel(x_ref, g_ref, o_ref):
    o_ref[...] = _rms_norm(x_ref[...], g_ref[...])


def _norm_rows(x, g, *, tm, row0, rows):
    d = x.shape[1]
    off = row0 // tm
    return pl.pallas_call(
        _norm_kernel,
        grid=(rows // tm,),
        in_specs=[pl.BlockSpec((tm, d), lambda i: (i + off, 0)),
                  pl.BlockSpec((1, d), lambda i: (0, 0))],
        out_specs=pl.BlockSpec((tm, d), lambda i: (i, 0)),
        out_shape=jax.ShapeDtypeStruct((rows, d), F32),
        compiler_params=_params(1),
        name="rms_norm",
    )(x, g.reshape(1, d))


def _norm_tails(x, g):
    d = x.shape[1]
    blocks_per_seq = SEQ // POOL_HALO
    out = pl.pallas_call(
        _norm_kernel,
        grid=(BATCH,),
        in_specs=[pl.BlockSpec((POOL_HALO, d), lambda b: ((b + 1) * blocks_per_seq - 1, 0)),
                  pl.BlockSpec((1, d), lambda b: (0, 0))],
        out_specs=pl.BlockSpec((POOL_HALO, d), lambda b: (b, 0)),
        out_shape=jax.ShapeDtypeStruct((BATCH * POOL_HALO, d), F32),
        compiler_params=_params(1),
        name="rms_norm_tails",
    )(x, g.reshape(1, d))
    return out.reshape(BATCH, POOL_HALO, d)


def _ffn_kernel(*refs, cast_next):
    if cast_next:
        (x_ref, g_ref, wg_ref, wu_ref, wd_ref, ng_ref, nu_ref, nd_ref,
         o_ref, cg_ref, cu_ref, cd_ref, xn_ref) = refs
        acc_ref = o_ref
    else:
        x_ref, g_ref, wg_ref, wu_ref, wd_ref, fg_ref, yp_ref, ys_ref, xn_ref, acc_ref = refs
    i, j = pl.program_id(0), pl.program_id(1)

    @pl.when(j == 0)
    def _():
        xn_ref[...] = _rms_norm(x_ref[...], g_ref[...]).astype(BF16)
        acc_ref[...] = jnp.zeros_like(acc_ref)

    riders = ((ng_ref, cg_ref), (nu_ref, cu_ref), (nd_ref, cd_ref)) if cast_next else ()

    def convert(pairs):
        for src, dst in pairs:
            dst[...] = src[...].astype(BF16)

    xn = xn_ref[...]
    gate = _dot(xn, wg_ref[...])
    convert(riders[:2])
    up = _dot(xn, wu_ref[...])
    convert(riders[2:])
    h = (gate * jax.nn.sigmoid(gate) * up).astype(BF16)
    acc_ref[...] += _dot(h, wd_ref[...])

    @pl.when(j == pl.num_programs(1) - 1)
    def _():
        o = x_ref[...] + 0.5 * acc_ref[...]
        if cast_next:
            o_ref[...] = o
        else:
            y = _rms_norm(o, fg_ref[...])
            yp_ref[...] = y

            @pl.when(i == pl.num_programs(0) - 1)
            def _():
                ys_ref[...] = y[y.shape[0] - N_SAMPLE:]


def _ffn(x, g, w_bf, w_next=None, final_gain=None):
    m, d = x.shape
    tm, tf = TM_TOK, TF_FFN
    grid = (m // tm, D_FF // tf)
    in_specs = [pl.BlockSpec((tm, d), lambda i, j: (i, 0)),
                pl.BlockSpec((1, d), lambda i, j: (0, 0)),
                pl.BlockSpec((d, tf), lambda i, j: (0, j)),
                pl.BlockSpec((d, tf), lambda i, j: (0, j)),
                pl.BlockSpec((tf, d), lambda i, j: (j, 0))]
    args = [x, g.reshape(1, d), *w_bf]
    scratch = [pltpu.VMEM((tm, d), BF16)]
    if w_next is not None:
        (ng, nu, nd), layer, half = w_next
        out_specs = [pl.BlockSpec((tm, d), lambda i, j: (i, 0))]
        out_shape = [jax.ShapeDtypeStruct((m, d), F32)]
        assert grid[0] * grid[1] >= CAST_STEPS

        def block(n_col_blocks):
            def index(i, j):
                b = jnp.minimum(i * grid[1] + j, CAST_STEPS - 1)
                return b // n_col_blocks, b % n_col_blocks
            return index

        for w, (rows, cols) in zip((ng, nu, nd), CAST_WINDOWS):
            assert (w.shape[2] // rows) * (w.shape[3] // cols) == CAST_STEPS
            idx = block(w.shape[3] // cols)
            in_specs.append(pl.BlockSpec((None, None, rows, cols),
                                         lambda i, j, idx=idx: (layer, half, *idx(i, j))))
            out_specs.append(pl.BlockSpec((rows, cols), idx))
            out_shape.append(jax.ShapeDtypeStruct(w.shape[2:], BF16))
            args.append(w)
    else:
        assert m == N_TOK and N_TOK - (grid[0] - 1) * tm >= N_SAMPLE
        in_specs.append(pl.BlockSpec((1, d), lambda i, j: (0, 0)))
        args.append(final_gain.reshape(1, d))
        out_specs = [pl.BlockSpec((tm, d), lambda i, j: (i, 0)),
                     pl.BlockSpec((N_SAMPLE, d), lambda i, j: (0, 0))]
        out_shape = [jax.ShapeDtypeStruct((N_PROMPT, d), F32), jax.ShapeDtypeStruct((N_SAMPLE, d), F32)]
        scratch.append(pltpu.VMEM((tm, d), F32))
    out = pl.pallas_call(
        functools.partial(_ffn_kernel, cast_next=w_next is not None),
        grid=grid,
        in_specs=in_specs,
        out_specs=out_specs,
        out_shape=out_shape,
        scratch_shapes=scratch,
        compiler_params=_params(2),
        name="ffn_half_step",
    )(*args)
    return (out[0], tuple(out[1:])) if w_next is not None else tuple(out)


def _matmul_kernel(*refs, has_gain, has_bias, has_res):
    refs = list(refs)
    lhs_ref, w_ref = refs[0], refs[1]
    pos = 2
    g_ref = b_ref = res_ref = None
    if has_gain:
        g_ref = refs[pos]
        pos += 1
    if has_bias:
        b_ref = refs[pos]
        pos += 1
    if has_res:
        res_ref = refs[pos]
        pos += 1
    o_ref, lhs_bf_ref = refs[pos], refs[pos + 1]

    @pl.when(pl.program_id(1) == 0)
    def _():
        lhs = lhs_ref[...]
        if has_gain:
            lhs = _rms_norm(lhs, g_ref[...])
        lhs_bf_ref[...] = lhs.astype(BF16)

    acc = _dot(lhs_bf_ref[...], w_ref[...])
    if has_bias:
        acc = acc + b_ref[...]
    if has_res:
        acc = res_ref[...] + acc
    o_ref[...] = acc


def _matmul(lhs, w, bias=None, res=None, *, tm, tn, res_row0=0, gain=None):
    m, k = lhs.shape
    n = w.shape[1]
    off = res_row0 // tm
    in_specs = [pl.BlockSpec((tm, k), lambda i, j: (i, 0)),
                pl.BlockSpec((k, tn), lambda i, j: (0, j))]
    args = [lhs, w]
    if gain is not None:
        in_specs.append(pl.BlockSpec((1, k), lambda i, j: (0, 0)))
        args.append(gain.reshape(1, k))
    if bias is not None:
        in_specs.append(pl.BlockSpec((1, tn), lambda i, j: (0, j)))
        args.append(bias.reshape(1, n))
    aliases = {}
    if res is not None:
        in_specs.append(pl.BlockSpec((tm, tn), lambda i, j: (i + off, j)))
        aliases = {len(args): 0}
        args.append(res)
        out_shape = jax.ShapeDtypeStruct(res.shape, F32)
    else:
        out_shape = jax.ShapeDtypeStruct((m, n), F32)
    return pl.pallas_call(
        functools.partial(_matmul_kernel, has_gain=gain is not None, has_bias=bias is not None,
                          has_res=res is not None),
        grid=(m // tm, n // tn),
        in_specs=in_specs,
        out_specs=pl.BlockSpec((tm, tn), lambda i, j: (i + off, j)),
        out_shape=out_shape,
        scratch_shapes=[pltpu.VMEM((tm, k), BF16)],
        input_output_aliases=aliases,
        compiler_params=_params(2),
        name="matmul_bias_residual",
    )(*args)


def _pool_group_out(diff, gi, x_ref, w_ref, sc_ref, o_ref):
    c = slice(gi * POOL_GROUP_DIM, (gi + 1) * POOL_GROUP_DIM)
    out = _dot(diff.astype(BF16), w_ref[gi])
    o_ref[:, c] = x_ref[:, c] + out * sc_ref[:, c]


def _pool_prompt_kernel(x_ref, halo_ref, g_ref, w_ref, sc_ref, o_ref, ext_ref, lvl_ref,
                        *, tm, tiles_per_seq, n_tiles):
    i = pl.program_id(0)
    pad, top, rows = POOL_PAD, POOL_PAD + POOL_HALO, tm + POOL_HALO

    @pl.when(i < n_tiles)
    def _():
        t_in_seq = i % tiles_per_seq
        g = g_ref[...]
        ext_ref[0:pad, :] = jnp.zeros((pad, ext_ref.shape[1]), F32)
        ext_ref[pad:top, :] = jnp.where(t_in_seq == 0, 0.0, _rms_norm(halo_ref[...], g))
        ext_ref[top:, :] = _rms_norm(x_ref[...], g)
        lvl_ref[0:pad, :] = jnp.zeros((pad, lvl_ref.shape[1]), F32)
        pos = t_in_seq * tm + lax.broadcasted_iota(jnp.int32, (tm, 1), 0)
        for gi, w in enumerate(POOL_WINDOWS):
            c = slice(gi * POOL_GROUP_DIM, (gi + 1) * POOL_GROUP_DIM)
            part = ext_ref[pl.ds(pad, rows), c] + ext_ref[pl.ds(pad - 1, rows), c]
            span = 2
            while span < w:
                lvl_ref[pl.ds(pad, rows), :] = part
                part = part + lvl_ref[pl.ds(pad - span, rows), :]
                span *= 2
            cur = ext_ref[pl.ds(top, tm), c]
            cnt = jnp.minimum(w, pos + 1).astype(F32)
            _pool_group_out(part[POOL_HALO:] / cnt - cur, gi, x_ref, w_ref, sc_ref, o_ref)

    @pl.when(i >= n_tiles)
    def _():
        o_ref[...] = x_ref[...]


def _pool_prompt(x, gain, w_pool, scale):
    d = x.shape[1]
    tm = TM_POOL
    halo_blocks = tm // POOL_HALO
    return pl.pallas_call(
        functools.partial(_pool_prompt_kernel, tm=tm, tiles_per_seq=SEQ // tm, n_tiles=N_PROMPT // tm),
        grid=(N_TOK // tm,),
        in_specs=[pl.BlockSpec((tm, d), lambda i: (i, 0)),
                  pl.BlockSpec((POOL_HALO, d), lambda i: (jnp.maximum(i * halo_blocks - 1, 0), 0)),
                  pl.BlockSpec((1, d), lambda i: (0, 0)),
                  pl.BlockSpec((POOL_GROUPS, POOL_GROUP_DIM, POOL_GROUP_DIM), lambda i: (0, 0, 0)),
                  pl.BlockSpec((1, d), lambda i: (0, 0))],
        out_specs=pl.BlockSpec((tm, d), lambda i: (i, 0)),
        out_shape=jax.ShapeDtypeStruct(x.shape, F32),
        scratch_shapes=[pltpu.VMEM((POOL_PAD + POOL_HALO + tm, d), F32),
                        pltpu.VMEM((POOL_PAD + POOL_HALO + tm, POOL_GROUP_DIM), F32)],
        compiler_params=_params(1),
        name="pool_prompt",
    )(x, x, gain.reshape(1, d), w_pool, scale.reshape(1, d))


def _pool_sample_kernel(ext_ref, x_ref, w_ref, sc_ref, o_ref):
    for gi, w in enumerate(POOL_WINDOWS):
        c = slice(gi * POOL_GROUP_DIM, (gi + 1) * POOL_GROUP_DIM)
        cur = ext_ref[:, pl.ds(POOL_HALO, DEC_SEQ), c]
        s = cur
        for back in range(1, w):
            s = s + ext_ref[:, pl.ds(POOL_HALO - back, DEC_SEQ), c]
        cnt = min(w, PAST_LEN + 1)
        diff = (s / float(cnt) - cur).reshape(N_SAMPLE, POOL_GROUP_DIM)
        _pool_group_out(diff, gi, x_ref, w_ref, sc_ref, o_ref)


def _pool_sample(ext, x, w_pool, scale):
    d = x.shape[1]
    blk = N_PROMPT // N_SAMPLE
    return pl.pallas_call(
        _pool_sample_kernel,
        grid=(1,),
        in_specs=[pl.BlockSpec(ext.shape, lambda i: (0, 0, 0)),
                  pl.BlockSpec((N_SAMPLE, d), lambda i: (blk, 0)),
                  pl.BlockSpec((POOL_GROUPS, POOL_GROUP_DIM, POOL_GROUP_DIM), lambda i: (0, 0, 0)),
                  pl.BlockSpec((1, d), lambda i: (0, 0))],
        out_specs=pl.BlockSpec((N_SAMPLE, d), lambda i: (blk, 0)),
        out_shape=jax.ShapeDtypeStruct(x.shape, F32),
        input_output_aliases={1: 0},
        compiler_params=_params(1),
        name="pool_sample",
    )(ext, x, w_pool, scale.reshape(1, d))


def _t5_bucket_table(tq):
    qi = np.arange(tq)[:, None]
    kj = np.arange(WINDOW + tq)[None, :]
    dist = qi + WINDOW - kj
    exact = T5_BUCKETS // 2
    ratio = np.log(np.maximum(dist, 1) / exact) / math.log(T5_MAX_DISTANCE / exact)
    large = np.minimum(exact + (ratio * (T5_BUCKETS - exact)).astype(np.int64), T5_BUCKETS - 1)
    bucket = np.where(dist < exact, dist, large)
    valid = (dist >= 0) & (dist < WINDOW)
    table = np.where(valid, bucket, -1).astype(np.int32)
    return table[:, :WINDOW], table[:, WINDOW:]


def _attn_sample_kernel(rb_ref, sink_ref, bkt_p_ref, bkt_c_ref, q_ref, kc_ref, vc_ref, kp_ref, vp_ref,
                        o_ref, bias_p_ref, bias_c_ref, sink_col_ref, *, tq):
    step = pl.program_id(0)

    @pl.when(step == 0)
    def _():
        bkt_p = bkt_p_ref[...]
        bkt_c = bkt_c_ref[...]
        bias_p_ref[...] = jnp.zeros_like(bias_p_ref)
        bias_c_ref[...] = jnp.zeros_like(bias_c_ref)

        def add_bucket(b, carry):
            eq_p = bkt_p == b
            eq_c = bkt_c == b
            for h in range(N_HEADS):
                kvh, g = divmod(h, GQA_GROUP)
                r = slice(g * tq, (g + 1) * tq)
                val = rb_ref[b, h]
                bias_p_ref[kvh, r, :] += jnp.where(eq_p, val, 0.0)
                bias_c_ref[kvh, r, :] += jnp.where(eq_c, val, 0.0)
            return carry

        lax.fori_loop(0, T5_BUCKETS, add_bucket, 0)
        for h in range(N_HEADS):
            kvh, g = divmod(h, GQA_GROUP)
            r = slice(g * tq, (g + 1) * tq)
            bias_p_ref[kvh, r, :] = jnp.where(bkt_p < 0, NEG_INF, bias_p_ref[kvh, r, :])
            bias_c_ref[kvh, r, :] = jnp.where(bkt_c < 0, NEG_INF, bias_c_ref[kvh, r, :])
            sink_col_ref[kvh, r, :] = jnp.full((tq, 1), sink_ref[h], F32)

    def kv_head(kvh):
        heads = [kvh * GQA_GROUP + g for g in range(GQA_GROUP)]
        qs = jnp.concatenate([q_ref[:, h * HEAD_DIM:(h + 1) * HEAD_DIM] for h in heads], axis=0).astype(BF16)
        c = slice(kvh * HEAD_DIM, (kvh + 1) * HEAD_DIM)
        qk_p = _dot_nt(qs, kp_ref[:, c].astype(BF16))
        qk_c = _dot_nt(qs, kc_ref[:, c].astype(BF16))
        yield
        s_p = qk_p * ATT_SCALE + bias_p_ref[kvh]
        s_c = qk_c * ATT_SCALE + bias_c_ref[kvh]
        sink = sink_col_ref[kvh]
        m = jnp.maximum(jnp.maximum(jnp.max(s_p, axis=-1, keepdims=True),
                                    jnp.max(s_c, axis=-1, keepdims=True)), sink)
        p_p = jnp.exp(s_p - m)
        p_c = jnp.exp(s_c - m)
        den = (jnp.sum(p_p, axis=-1, keepdims=True) + jnp.sum(p_c, axis=-1, keepdims=True)
               + jnp.exp(sink - m))
        o_p = _dot(p_p.astype(BF16), vp_ref[:, c].astype(BF16))
        o_c = _dot(p_c.astype(BF16), vc_ref[:, c].astype(BF16))
        yield
        o = (o_p + o_c) / den
        for g, h in enumerate(heads):
            o_ref[:, h * HEAD_DIM:(h + 1) * HEAD_DIM] = o[g * tq:(g + 1) * tq, :]

    _round_robin([kv_head(kvh) for kvh in range(N_KV_HEADS)])


def _attention_sample(qkv, rel_bias, sinks, k_prev, v_prev):
    tq = DEC_SEQ
    bkt_p, bkt_c = _t5_bucket_table(tq)
    off = N_PROMPT // tq
    kcol = ATT_DIM // KV_DIM
    smem = pl.BlockSpec(memory_space=pltpu.SMEM)
    cache = pl.BlockSpec((None, WINDOW, KV_DIM), lambda i: (i, 0, 0))
    rows = GQA_GROUP * tq
    return pl.pallas_call(
        functools.partial(_attn_sample_kernel, tq=tq),
        grid=(DEC_BATCH,),
        in_specs=[smem, smem,
                  pl.BlockSpec((tq, WINDOW), lambda i: (0, 0)),
                  pl.BlockSpec((tq, tq), lambda i: (0, 0)),
                  pl.BlockSpec((tq, ATT_DIM), lambda i: (i + off, 0)),
                  pl.BlockSpec((tq, KV_DIM), lambda i: (i + off, kcol)),
                  pl.BlockSpec((tq, KV_DIM), lambda i: (i + off, kcol + 1)),
                  cache, cache],
        out_specs=pl.BlockSpec((tq, ATT_DIM), lambda i: (i, 0)),
        out_shape=jax.ShapeDtypeStruct((N_SAMPLE, ATT_DIM), F32),
        scratch_shapes=[pltpu.VMEM((N_KV_HEADS, rows, WINDOW), F32),
                        pltpu.VMEM((N_KV_HEADS, rows, tq), F32),
                        pltpu.VMEM((N_KV_HEADS, rows, 1), F32)],
        compiler_params=_params(1),
        name="swa_attention_sample",
    )(rel_bias, sinks, jnp.asarray(bkt_p), jnp.asarray(bkt_c), qkv, qkv, qkv, k_prev, v_prev)


def _attn_prompt_kernel(rb_ref, sink_ref, bkt_ref, q_ref, kc_ref, vc_ref, kp_ref, vp_ref, o_ref, bias_ref,
                        *, blocks_per_seq):
    step = pl.program_id(0)
    tq, nk, pair = WINDOW, 2 * WINDOW, 2 * HEAD_DIM

    @pl.when(step == 0)
    def _():
        bkt = bkt_ref[...]
        bias_ref[0] = jnp.zeros(bias_ref.shape[1:], F32)

        def add_bucket(b, carry):
            eq = bkt == b
            for h in range(N_HEADS):
                bias_ref[0, h] += jnp.where(eq, rb_ref[b, h], 0.0)
            return carry

        lax.fori_loop(0, T5_BUCKETS, add_bucket, 0)
        no_prev_key = lax.broadcasted_iota(jnp.int32, (nk, tq), 0) < WINDOW
        for h in range(N_HEADS):
            masked = jnp.where(bkt < 0, NEG_INF, bias_ref[0, h])
            bias_ref[0, h] = masked
            bias_ref[1, h] = jnp.where(no_prev_key, NEG_INF, masked)

    first = (step % blocks_per_seq == 0).astype(jnp.int32)
    lane_half = lax.broadcasted_iota(jnp.int32, (nk, pair), 1) // HEAD_DIM
    row_half = lax.broadcasted_iota(jnp.int32, (pair, tq), 0) // HEAD_DIM

    for kvh in range(N_KV_HEADS):
        tile = slice((kvh // 2) * pair, (kvh // 2 + 1) * pair)
        half = kvh % 2

        def both_halves(prev_ref, cur_ref):
            x = jnp.concatenate([prev_ref[:, tile], cur_ref[:, tile]], axis=0)
            own = jnp.where(lane_half == half, x, 0.0)
            other = pltpu.roll(own, HEAD_DIM, axis=1)
            return (own, other) if half == 0 else (other, own)

        k_lo, k_hi = both_halves(kp_ref, kc_ref)
        v_lo, v_hi = both_halves(vp_ref, vc_ref)
        k2 = jnp.concatenate([k_lo, k_hi], axis=0).astype(BF16)
        v_lo_t = v_lo.T.astype(BF16)
        v_hi_t = v_hi.T.astype(BF16)

        def head_pair(j):
            cols = slice(j * pair, (j + 1) * pair)
            qp = (q_ref[:, cols] * ATT_SCALE).astype(BF16)
            s2 = _dot_nt(k2, qp)
            yield
            probs, dens = [], []
            for t in range(2):
                h = 2 * j + t
                s = s2[t * nk:(t + 1) * nk] + bias_ref[first, h]
                sink = sink_ref[h]
                m = jnp.maximum(jnp.max(s, axis=0, keepdims=True), sink)
                p = jnp.exp(s - m)
                dens.append(jnp.sum(p, axis=0, keepdims=True) + jnp.exp(sink - m))
                probs.append(p.astype(BF16))
            o2 = _dot(v_lo_t, probs[0]) + _dot(v_hi_t, probs[1])
            yield
            o2 = o2 / jnp.where(row_half == 0, dens[0], dens[1])
            o_ref[:, cols] = o2.T

        pairs_per_kv = GQA_GROUP // 2
        _round_robin([head_pair(kvh * pairs_per_kv + jj) for jj in range(pairs_per_kv)])


def _attention_prompt(qkv, rel_bias, sinks):
    bkt_p, bkt_c = _t5_bucket_table(WINDOW)
    bkt_t = np.ascontiguousarray(np.concatenate([bkt_p, bkt_c], axis=1).T)
    kcol = ATT_DIM // KV_DIM
    smem = pl.BlockSpec(memory_space=pltpu.SMEM)
    prev = lambda i: jnp.maximum(i - 1, 0)
    return pl.pallas_call(
        functools.partial(_attn_prompt_kernel, blocks_per_seq=SEQ // WINDOW),
        grid=(N_PROMPT // WINDOW,),
        in_specs=[smem, smem,
                  pl.BlockSpec((2 * WINDOW, WINDOW), lambda i: (0, 0)),
                  pl.BlockSpec((WINDOW, ATT_DIM), lambda i: (i, 0)),
                  pl.BlockSpec((WINDOW, KV_DIM), lambda i: (i, kcol)),
                  pl.BlockSpec((WINDOW, KV_DIM), lambda i: (i, kcol + 1)),
                  pl.BlockSpec((WINDOW, KV_DIM), lambda i: (prev(i), kcol)),
                  pl.BlockSpec((WINDOW, KV_DIM), lambda i: (prev(i), kcol + 1))],
        out_specs=pl.BlockSpec((WINDOW, ATT_DIM), lambda i: (i, 0)),
        out_shape=jax.ShapeDtypeStruct((N_PROMPT, ATT_DIM), F32),
        scratch_shapes=[pltpu.VMEM((2, N_HEADS, 2 * WINDOW, WINDOW), F32)],
        compiler_params=_params(1),
        name="swa_attention_prompt",
    )(rel_bias, sinks, jnp.asarray(bkt_t), qkv, qkv, qkv, qkv, qkv)


def _rwkv_proj_kernel(*refs, tm, seq_len, has_first):
    refs = list(refs)
    x_ref, halo_ref, gain_ref = refs[:3]
    pos = 3
    first_ref = None
    if has_first:
        first_ref = refs[pos]
        pos += 1
    (mu_ref, wr_ref, wk_ref, wv_ref, w1_ref, w2_ref, w0_ref, a1_ref, a2_ref, a0_ref, g1_ref, g2_ref,
     r_ref, k_ref, v_ref, ld_ref, a_ref, g_ref,
     ext_ref, xr_ref, xk_ref, xv_ref, hw_ref, ha_ref, hg_ref) = refs[pos:]

    @pl.when(pl.program_id(1) == 0)
    def _():
        gain = gain_ref[...]
        u = _rms_norm(x_ref[...], gain)
        ext_ref[0:SHIFT_HALO, :] = _rms_norm(halo_ref[...], gain)
        ext_ref[SHIFT_HALO:, :] = u
        prev = ext_ref[pl.ds(SHIFT_HALO - 1, tm), :]
        row = pl.program_id(0) * tm + lax.broadcasted_iota(jnp.int32, (tm, 1), 0)
        starts = row % seq_len == 0
        prev = jnp.where(starts, first_ref[...] if has_first else 0.0, prev)
        dx = prev - u
        mix = lambda i: (u + dx * mu_ref[i:i + 1, :]).astype(BF16)
        xr_ref[...] = mix(0)
        hw_ref[...] = jnp.tanh(_dot(mix(1), w1_ref[...])).astype(BF16)
        xk_ref[...] = mix(2)
        xv_ref[...] = mix(3)
        ha_ref[...] = _dot(mix(4), a1_ref[...]).astype(BF16)
        hg_ref[...] = jax.nn.sigmoid(_dot(mix(5), g1_ref[...])).astype(BF16)

    r_ref[...] = _dot(xr_ref[...], wr_ref[...])
    k_ref[...] = _dot(xk_ref[...], wk_ref[...])
    v_ref[...] = _dot(xv_ref[...], wv_ref[...])
    z = -(w0_ref[...] + _dot(hw_ref[...], w2_ref[...]))
    softplus = jnp.maximum(z, 0.0) + jnp.log1p(jnp.exp(-jnp.abs(z)))
    ld_ref[...] = -jnp.exp(-softplus - 0.5)
    a_ref[...] = jax.nn.sigmoid(a0_ref[...] + _dot(ha_ref[...], a2_ref[...]))
    g_ref[...] = _dot(hg_ref[...], g2_ref[...])


def _rwkv_proj(x, gain, weights, *, row0, rows, tm, seq_len, first=None):
    mu, w_r, w_k, w_v, w1, w2, w0, a1, a2, a0, g1, g2 = weights
    d = x.shape[1]
    tn = TN_PROJ
    gl = g1.shape[1]
    off = row0 // tm
    halo_blocks = tm // SHIFT_HALO
    halo_off = row0 // SHIFT_HALO
    row = lambda i, j: (i + off, 0)
    col = lambda i, j: (0, j)
    fixed = lambda i, j: (0, 0)
    in_specs = [pl.BlockSpec((tm, d), row),
                pl.BlockSpec((SHIFT_HALO, d), lambda i, j: (jnp.maximum(halo_off + i * halo_blocks - 1, 0), 0)),
                pl.BlockSpec((1, d), fixed)]
    args = [x, x, gain.reshape(1, d)]
    if first is not None:
        in_specs.append(pl.BlockSpec((tm, d), lambda i, j: (i, 0)))
        args.append(first)
    in_specs += [pl.BlockSpec((6, d), fixed),
                 pl.BlockSpec((d, tn), col), pl.BlockSpec((d, tn), col), pl.BlockSpec((d, tn), col),
                 pl.BlockSpec((d, LORA_PAD), fixed), pl.BlockSpec((LORA_PAD, tn), col), pl.BlockSpec((1, tn), col),
                 pl.BlockSpec((d, LORA_PAD), fixed), pl.BlockSpec((LORA_PAD, tn), col), pl.BlockSpec((1, tn), col),
                 pl.BlockSpec((d, gl), fixed), pl.BlockSpec((gl, tn), col)]
    args += [mu, w_r, w_k, w_v, w1, w2, w0.reshape(1, d), a1, a2, a0.reshape(1, d), g1, g2]
    out = jax.ShapeDtypeStruct((rows, d), F32)
    return pl.pallas_call(
        functools.partial(_rwkv_proj_kernel, tm=tm, seq_len=seq_len, has_first=first is not None),
        grid=(rows // tm, d // tn),
        in_specs=in_specs,
        out_specs=[pl.BlockSpec((tm, tn), lambda i, j: (i, j))] * 6,
        out_shape=[out] * 6,
        scratch_shapes=[pltpu.VMEM((tm + SHIFT_HALO, d), F32)] + [pltpu.VMEM((tm, d), BF16)] * 3
        + [pltpu.VMEM((tm, LORA_PAD), BF16)] * 2 + [pltpu.VMEM((tm, gl), BF16)],
        compiler_params=_params(2),
        name="rwkv_projections",
    )(*args)


def _scan_kernel(*refs, chunk, has_init):
    (r_ref, k_ref, v_ref, ld_ref, a_ref, g_ref, kk_ref, ka_ref, rk_ref, lnw_ref, lnb_ref) = refs[:11]
    refs = refs[11:]
    if has_init:
        s0_ref, refs = refs[0], refs[1:]
    z_ref, sout_ref, s_ref = refs
    c_idx = pl.program_id(1)
    cc, n, w = chunk, RWKV_N, SCAN_LANES
    hc = SCAN_HEADS * cc
    n_groups = RWKV_HEADS // SCAN_HEADS
    n_sb = z_ref.shape[0]

    @pl.when(c_idx == 0)
    def _():
        s_ref[...] = jnp.zeros_like(s_ref)
        if has_init:
            for sb in range(n_sb):
                for h in range(RWKV_HEADS):
                    gi, hh = divmod(h, SCAN_HEADS)
                    s_ref[sb, gi, hh * n:(hh + 1) * n, hh * n:(hh + 1) * n] = s0_ref[sb, h]

    def iota(shape, axis):
        return lax.broadcasted_iota(jnp.int32, shape, axis)

    head_diag = iota((w, w), 0) // n == iota((w, w), 1) // n
    tril = (iota((cc, cc), 0) >= iota((cc, cc), 1)).astype(BF16)
    t_row = iota((cc, hc), 0)
    s_col = iota((cc, hc), 1) % cc
    strict = t_row > s_col
    causal = t_row >= s_col
    lanes_diag = iota((hc, w), 0) // cc == iota((hc, w), 1) // n
    tokens_diag = iota((hc, hc), 0) // cc == iota((hc, hc), 1) // cc
    lane_head = iota((cc, w), 1) // n

    def block_diag(x, diag):
        return jnp.where(diag, jnp.concatenate([x] * SCAN_HEADS, axis=0), 0.0).astype(BF16)

    def bd_lanes(x):
        return block_diag(x, lanes_diag)

    def bd_tokens(x):
        return block_diag(x, tokens_diag)

    def head_sum(x):
        out = jnp.zeros_like(x)
        for hh in range(SCAN_HEADS):
            mine = lane_head == hh
            out = jnp.where(mine, jnp.sum(jnp.where(mine, x, 0.0), axis=-1, keepdims=True), out)
        return out

    def group(sb, gi):
        c = slice(gi * w, (gi + 1) * w)
        r, k, v, ld, a = r_ref[sb, :, c], k_ref[sb, :, c], v_ref[sb, :, c], ld_ref[sb, :, c], a_ref[sb, :, c]

        p1 = ld.astype(BF16)
        r1 = ld - p1.astype(F32)
        p2 = r1.astype(BF16)
        p3 = (r1 - p2.astype(F32)).astype(BF16)
        cum = _dot(tril, jnp.concatenate([p1, p2, p3], axis=1))
        yield
        kk = k * kk_ref[:, c]
        kmod = k * (1.0 + (a - 1.0) * ka_ref[:, c])
        kap = kk / jnp.maximum(jnp.sqrt(head_sum(kk * kk)), 1e-12)
        b = kap * a

        lcum = cum[:, :w] + cum[:, w:2 * w] + cum[:, 2 * w:]
        lend = lcum[cc - 1:cc, :]
        e_inc = jnp.exp(lcum)
        e_exc = jnp.exp(lcum - ld)
        e_neg = jnp.exp(-lcum)
        e_end = jnp.exp(lend - lcum)

        kq = kap * e_exc
        rq = r * e_inc
        qr = jnp.concatenate([kq, rq], axis=0).astype(BF16)
        s_bd = s_ref[sb, gi]
        bk_bd = jnp.concatenate([bd_lanes(b * e_neg), bd_lanes(kmod * e_neg)], axis=0)
        if (2 * hc) % 128 == 0:
            gram = _dot_nt(qr, jnp.concatenate([bk_bd, s_bd.astype(BF16)], axis=0))
            yield
            qs = gram[:, 2 * hc:]
        else:
            gram = _dot_nt(qr, bk_bd)
            yield
            qs = _dot_nt(qr, s_bd.astype(BF16))
            yield
        a_w = jnp.where(strict, gram[:cc, :hc], 0.0)
        bk_w = jnp.where(strict, gram[:cc, hc:2 * hc], 0.0)
        cb_w = jnp.where(causal, gram[cc:, :hc], 0.0)
        ck_w = jnp.where(causal, gram[cc:, hc:2 * hc], 0.0)

        kv = _dot(jnp.concatenate([bk_w, ck_w], axis=0).astype(BF16), bd_lanes(v))
        yield
        x = -(qs[:cc] + kv[:cc])
        a_pow, power = a_w, 1
        while power < cc:
            more = 2 * power < cc
            rhs = bd_lanes(x)
            if more:
                rhs = jnp.concatenate([rhs, bd_tokens(a_pow)], axis=1)
            prod = _dot(a_pow.astype(BF16), rhs)
            yield
            ax = prod[:, :w]
            x = x - ax if power == 1 else x + ax
            if more:
                a_pow = prod[:, w:]
            power *= 2
        u = x

        y_in = _dot(cb_w.astype(BF16), bd_lanes(u)) + kv[cc:]
        yield
        upd = _dot_tn(jnp.concatenate([u, v], axis=0).astype(BF16),
                      jnp.concatenate([b * e_end, kmod * e_end], axis=0).astype(BF16))
        yield
        s_ref[sb, gi] = s_bd * jnp.exp(lend) + jnp.where(head_diag, upd, 0.0)

        y = qs[cc:] + y_in
        mean = head_sum(y) * (1.0 / n)
        dev = y - mean
        var = head_sum(dev * dev) * (1.0 / n)
        yn = dev * lax.rsqrt(var + GN_EPS) * lnw_ref[:, c] + lnb_ref[:, c]
        bonus = head_sum(r * kmod * rk_ref[:, c]) * v
        z_ref[sb, :, c] = (yn + bonus) * g_ref[sb, :, c]

    _round_robin([group(sb, gi) for sb in range(n_sb) for gi in range(n_groups)])

    @pl.when(c_idx == pl.num_programs(1) - 1)
    def _():
        for sb in range(n_sb):
            for h in range(RWKV_HEADS):
                gi, hh = divmod(h, SCAN_HEADS)
                sout_ref[sb, h] = s_ref[sb, gi, hh * n:(hh + 1) * n, hh * n:(hh + 1) * n]


def _rwkv_scan(proj, k_k, k_a, r_k, ln_w, ln_b, *, n_seq, per_step, chunk, n_chunks, s0=None):
    d = D_MODEL
    t = chunk * n_chunks
    tok = pl.BlockSpec((per_step, chunk, d), lambda b, c: (b, c, 0))
    par = pl.BlockSpec((1, d), lambda b, c: (0, 0))
    state = pl.BlockSpec((per_step, RWKV_HEADS, RWKV_N, RWKV_N), lambda b, c: (b, 0, 0, 0))
    in_specs = [tok] * 6 + [par] * 5
    args = [p.reshape(n_seq, t, d) for p in proj] + [p.reshape(1, d) for p in (k_k, k_a, r_k, ln_w, ln_b)]
    if s0 is not None:
        in_specs.append(state)
        args.append(s0)
    n_groups = RWKV_HEADS // SCAN_HEADS
    z, s_out = pl.pallas_call(
        functools.partial(_scan_kernel, chunk=chunk, has_init=s0 is not None),
        grid=(n_seq // per_step, n_chunks),
        in_specs=in_specs,
        out_specs=[tok, state],
        out_shape=[jax.ShapeDtypeStruct((n_seq, t, d), F32),
                   jax.ShapeDtypeStruct((n_seq, RWKV_HEADS, RWKV_N, RWKV_N), F32)],
        scratch_shapes=[pltpu.VMEM((per_step, n_groups, SCAN_LANES, SCAN_LANES), F32)],
        compiler_params=_params(2),
        name="rwkv_scan",
    )(*args)
    return z.reshape(n_seq * t, d), s_out


def _pad_cols(w):
    return jnp.pad(w, ((0, 0), (0, LORA_PAD - w.shape[1])))


def _pad_rows(w):
    return jnp.pad(w, ((0, LORA_PAD - w.shape[0]), (0, 0)))


def kernel(x_prompt, x_sample, state_pool, cache_win_k, cache_win_v, state_shift, state_wkv, norm_ffn1, norm_mix, norm_ffn2, norm_final, ffn_w_gate, ffn_w_up, ffn_w_down, pool_w, pool_scale, att_w_qkv, att_b_qkv, att_w_o, att_b_o, att_sinks, rel_bias, rwkv_mu, rwkv_w_r, rwkv_w_k, rwkv_w_v, rwkv_w_o, rwkv_w0, rwkv_w1, rwkv_w2, rwkv_a0, rwkv_a1, rwkv_a2, rwkv_g1, rwkv_g2, rwkv_k_k, rwkv_k_a, rwkv_r_k, rwkv_ln_w, rwkv_ln_b):
    d = D_MODEL
    x = jnp.concatenate([x_prompt.reshape(N_PROMPT, d), x_sample.reshape(N_SAMPLE, d)], axis=0)
    ffn_w = (ffn_w_gate, ffn_w_up, ffn_w_down)
    w_bf = tuple(w[0, 0].astype(BF16) for w in ffn_w)
    pool_p, pool_s, wk_p, wv_p, wk_s, wv_s, sh_p, sh_s, wkv_p, wkv_s = ([] for _ in range(10))

    for l in range(DEPTH):
        j, kind = divmod(l, N_MIXERS)
        x, w_bf = _ffn(x, norm_ffn1[l], w_bf, (ffn_w, l, 1))
        gain = norm_mix[l]
        if kind == 0:
            w_pool = pool_w[j].astype(BF16)
            u_s = _norm_rows(x, gain, tm=N_SAMPLE, row0=N_PROMPT, rows=N_SAMPLE).reshape(DEC_BATCH, DEC_SEQ, d)
            ext = jnp.concatenate([jnp.zeros((DEC_BATCH, 1, d), F32), state_pool[j], u_s], axis=1)
            pool_p.append(_norm_tails(x, gain)[:, -POOL_BUF:])
            pool_s.append(ext[:, -POOL_BUF:])
            x = _pool_prompt(x, gain, w_pool, pool_scale[j])
            x = _pool_sample(ext, x, w_pool, pool_scale[j])
        elif kind == 1:
            qkv = _matmul(x, att_w_qkv[j].astype(BF16), att_b_qkv[j], tm=TM_TOK, tn=QKV_DIM, gain=gain)
            k_buf = cache_win_k[j].reshape(DEC_BATCH, WINDOW, KV_DIM)
            v_buf = cache_win_v[j].reshape(DEC_BATCH, WINDOW, KV_DIM)
            o_p = _attention_prompt(qkv, rel_bias, att_sinks[j])
            o_s = _attention_sample(qkv, rel_bias, att_sinks[j], k_buf, v_buf)
            w_o = att_w_o[j].astype(BF16)
            x = _matmul(o_p, w_o, att_b_o[j], x, tm=TM_PROMPT, tn=d, res_row0=0)
            x = _matmul(o_s, w_o, att_b_o[j], x, tm=N_SAMPLE, tn=d, res_row0=N_PROMPT)
            kv_shape = (WINDOW, N_KV_HEADS, HEAD_DIM)
            k_cols = slice(ATT_DIM, ATT_DIM + KV_DIM)
            v_cols = slice(ATT_DIM + KV_DIM, QKV_DIM)
            tails = [slice((b + 1) * SEQ - WINDOW, (b + 1) * SEQ) for b in range(BATCH)]
            wk_p.append(jnp.stack([qkv[t, k_cols] for t in tails]).reshape(BATCH, *kv_shape))
            wv_p.append(jnp.stack([qkv[t, v_cols] for t in tails]).reshape(BATCH, *kv_shape))
            new_rows = slice(N_PROMPT, N_TOK)
            k_s = jnp.concatenate([k_buf, qkv[new_rows, k_cols].reshape(DEC_BATCH, DEC_SEQ, KV_DIM)], axis=1)
            v_s = jnp.concatenate([v_buf, qkv[new_rows, v_cols].reshape(DEC_BATCH, DEC_SEQ, KV_DIM)], axis=1)
            wk_s.append(k_s[:, -WINDOW:].reshape(DEC_BATCH, *kv_shape))
            wv_s.append(v_s[:, -WINDOW:].reshape(DEC_BATCH, *kv_shape))
        else:
            weights = (rwkv_mu[j],
                       rwkv_w_r[j].astype(BF16), rwkv_w_k[j].astype(BF16), rwkv_w_v[j].astype(BF16),
                       _pad_cols(rwkv_w1[j]).astype(BF16), _pad_rows(rwkv_w2[j]).astype(BF16), rwkv_w0[j],
                       _pad_cols(rwkv_a1[j]).astype(BF16), _pad_rows(rwkv_a2[j]).astype(BF16), rwkv_a0[j],
                       rwkv_g1[j].astype(BF16), rwkv_g2[j].astype(BF16))
            first_s = jnp.repeat(state_shift[j], DEC_SEQ, axis=0)
            proj_p = _rwkv_proj(x, gain, weights, row0=0, rows=N_PROMPT, tm=TM_PROMPT, seq_len=SEQ)
            proj_s = _rwkv_proj(x, gain, weights, row0=N_PROMPT, rows=N_SAMPLE, tm=N_SAMPLE, seq_len=DEC_SEQ,
                                first=first_s)
            head_params = (rwkv_k_k[j], rwkv_k_a[j], rwkv_r_k[j], rwkv_ln_w[j], rwkv_ln_b[j])
            z_p, s_p = _rwkv_scan(proj_p, *head_params, n_seq=BATCH, per_step=BATCH, chunk=SCAN_CHUNK,
                                  n_chunks=SEQ // SCAN_CHUNK)
            z_s, s_s = _rwkv_scan(proj_s, *head_params, n_seq=DEC_BATCH, per_step=SCAN_SAMPLE_SEQS, chunk=DEC_SEQ,
                                  n_chunks=1, s0=state_wkv[j])
            sh_p.append(_norm_tails(x, gain)[:, -1])
            sh_s.append(_norm_rows(x, gain, tm=N_SAMPLE, row0=N_PROMPT, rows=N_SAMPLE)
                        .reshape(DEC_BATCH, DEC_SEQ, d)[:, -1])
            w_o = rwkv_w_o[j].astype(BF16)
            x = _matmul(z_p, w_o, None, x, tm=TM_PROMPT, tn=d, res_row0=0)
            x = _matmul(z_s, w_o, None, x, tm=N_SAMPLE, tn=d, res_row0=N_PROMPT)
            wkv_p.append(s_p)
            wkv_s.append(s_s)
        if l + 1 < DEPTH:
            x, w_bf = _ffn(x, norm_ffn2[l], w_bf, (ffn_w, l + 1, 0))
        else:
            y_p, y_s = _ffn(x, norm_ffn2[l], w_bf, final_gain=norm_final)

    return (y_p.reshape(BATCH, SEQ, d), y_s.reshape(DEC_BATCH, DEC_SEQ, d),
            jnp.stack(pool_p), jnp.stack(pool_s),
            jnp.stack(wk_p), jnp.stack(wv_p), jnp.stack(wk_s), jnp.stack(wv_s),
            jnp.stack(sh_p), jnp.stack(sh_s),
            jnp.stack(wkv_p), jnp.stack(wkv_s))
```
